```python
import math
import jax, jax.numpy as jnp
from jax import lax
import numpy as np

D_MODEL = 1024
BATCH = 8
SEQ = 4096
DEPTH = 2

SSD_HEADS = 16
SSD_HEAD_DIM = 64
SSD_INNER = SSD_HEADS * SSD_HEAD_DIM
SSD_GROUPS = 4
SSD_STATE = 128
SSD_CONV = 4
SSD_CHUNK = 128
SSD_XBC = SSD_INNER + 2 * SSD_GROUPS * SSD_STATE
MOBA_HEADS = 8
MOBA_HEAD_DIM = 64
MOBA_INNER = MOBA_HEADS * MOBA_HEAD_DIM
MOBA_BLOCK = 256
MOBA_TOPK = 3
MOBA_Q_CHUNK = 32
FOX_HEADS = 16
FOX_HEAD_DIM = 64
FOX_INNER = FOX_HEADS * FOX_HEAD_DIM
FOX_Q_BLOCK = 128
D_FF = 4 * D_MODEL
ROPE_THETA = 10000.0
NORM_EPS = 1e-5
N_EVEN = (DEPTH + 1) // 2
N_ODD = DEPTH // 2
EVEN_IN = SSD_INNER + SSD_XBC + SSD_HEADS + 3 * MOBA_INNER
EVEN_OUT = SSD_INNER + MOBA_INNER
ODD_IN = 3 * FOX_INNER + FOX_HEADS

kernel_name = 'hybrid_ssd_moba_fox_trunk'


def rms_norm(x, g):
    xf = x.astype(jnp.float32)
    y = xf * lax.rsqrt(jnp.mean(xf * xf, axis=-1, keepdims=True) + NORM_EPS)
    return (y * g.astype(jnp.float32)).astype(x.dtype)


def rope(x):
    s, d = x.shape[1], x.shape[3]
    half = d // 2
    inv = jnp.power(ROPE_THETA, -jnp.arange(half, dtype=jnp.float32) / half)
    ang = jnp.arange(s, dtype=jnp.float32)[:, None] * inv[None, :]
    cos = jnp.cos(ang)[None, :, None, :]
    sin = jnp.sin(ang)[None, :, None, :]
    xf = x.astype(jnp.float32)
    x1, x2 = xf[..., :half], xf[..., half:]
    return jnp.concatenate([x1 * cos - x2 * sin, x2 * cos + x1 * sin], axis=-1)


def causal_dwconv(u, w, b):
    c = u.shape[-1]
    y = lax.conv_general_dilated(u, w[:, None, :].astype(u.dtype), window_strides=(1,),
                                 padding=[(SSD_CONV - 1, 0)],
                                 dimension_numbers=('NWC', 'WIO', 'NWC'),
                                 feature_group_count=c)
    return y + b.astype(u.dtype)


def ssd_scan(x, dt, a, b_in, c_in):
    bsz, s = x.shape[0], x.shape[1]
    nc = s // SSD_CHUNK
    L = SSD_CHUNK
    r = SSD_HEADS // SSD_GROUPS
    xc = (x * dt[..., None]).reshape(bsz, nc, L, SSD_GROUPS, r, SSD_HEAD_DIM)
    ad = (dt * a).reshape(bsz, nc, L, SSD_GROUPS, r)
    bc = b_in.reshape(bsz, nc, L, SSD_GROUPS, SSD_STATE)
    cc = c_in.reshape(bsz, nc, L, SSD_GROUPS, SSD_STATE)
    acum = jnp.cumsum(ad, axis=2)
    acum_t = jnp.moveaxis(acum, 2, -1)
    causal = jnp.tril(jnp.ones((L, L), dtype=bool))
    decay = jnp.exp(jnp.where(causal, acum_t[..., :, None] - acum_t[..., None, :], -jnp.inf))
    cb = jnp.einsum('bclgn,bcsgn->bcgls', cc, bc)
    y_diag = jnp.einsum('bcgrls,bcsgrp->bclgrp', cb[:, :, :, None] * decay, xc)
    decay_to_end = jnp.exp(acum[:, :, -1:] - acum)
    states = jnp.einsum('bclgn,bclgr,bclgrp->bcgrpn', bc, decay_to_end, xc)
    chunk_decay = jnp.exp(acum[:, :, -1])

    def step(h, inp):
        st, dec = inp
        return h * dec[..., None, None] + st, h

    h0 = jnp.zeros((bsz, SSD_GROUPS, r, SSD_HEAD_DIM, SSD_STATE), jnp.float32)
    _, prev = lax.scan(step, h0, (jnp.moveaxis(states, 1, 0), jnp.moveaxis(chunk_decay, 1, 0)))
    prev = jnp.moveaxis(prev, 0, 1)
    y_off = jnp.einsum('bclgn,bcgrpn,bclgr->bclgrp', cc, prev, jnp.exp(acum))
    return (y_diag + y_off).reshape(bsz, s, SSD_HEADS, SSD_HEAD_DIM)


def ssd_mixer(z, xbc, dt_raw, conv_w, conv_b, dt_bias, a_log, d_skip, gate_norm):
    bsz, s = z.shape[0], z.shape[1]
    xbc = jax.nn.silu(causal_dwconv(xbc, conv_w, conv_b)).astype(jnp.float32)
    gn = SSD_GROUPS * SSD_STATE
    xs = xbc[..., :SSD_INNER].reshape(bsz, s, SSD_HEADS, SSD_HEAD_DIM)
    bm = xbc[..., SSD_INNER:SSD_INNER + gn].reshape(bsz, s, SSD_GROUPS, SSD_STATE)
    cm = xbc[..., SSD_INNER + gn:].reshape(bsz, s, SSD_GROUPS, SSD_STATE)
    dt = jax.nn.softplus(dt_raw.astype(jnp.float32) + dt_bias.astype(jnp.float32))
    a = -jnp.exp(a_log.astype(jnp.float32))
    y = ssd_scan(xs, dt, a, bm, cm) + xs * d_skip.astype(jnp.float32)[:, None]
    y = y.reshape(bsz, s, SSD_INNER) * jax.nn.silu(z.astype(jnp.float32))
    yg = y.reshape(bsz, s, SSD_GROUPS, SSD_INNER // SSD_GROUPS)
    yg = yg * lax.rsqrt(jnp.mean(yg * yg, axis=-1, keepdims=True) + NORM_EPS)
    return (yg.reshape(bsz, s, SSD_INNER) * gate_norm.astype(jnp.float32)).astype(z.dtype)


def moba_attention(q, k, v):
    bsz, s = q.shape[0], q.shape[1]
    nb = -(-s // MOBA_BLOCK)
    pad = nb * MOBA_BLOCK - s
    topk = min(MOBA_TOPK, nb)
    scale = MOBA_HEAD_DIM ** -0.5

    def to_blocks(t):
        t = jnp.pad(t, ((0, 0), (0, pad), (0, 0), (0, 0))).transpose(0, 2, 1, 3)
        return t.reshape(bsz, MOBA_HEADS, nb, MOBA_BLOCK, MOBA_HEAD_DIM)

    kb, vb = to_blocks(k), to_blocks(v)
    kmean = jnp.mean(kb, axis=3)
    n_chunks = s // MOBA_Q_CHUNK
    qc = q.transpose(0, 2, 1, 3).reshape(bsz, MOBA_HEADS, n_chunks, MOBA_Q_CHUNK, MOBA_HEAD_DIM)
    qc = qc.transpose(2, 0, 1, 3, 4)
    gather = jax.vmap(jax.vmap(lambda blocks, i: blocks[i]))
    blk_ids = jnp.arange(nb)
    own_offsets = jnp.arange(MOBA_BLOCK)

    def chunk_fn(args):
        q_blk, ci = args
        start = ci * MOBA_Q_CHUNK
        pos = start + jnp.arange(MOBA_Q_CHUNK)
        own = start // MOBA_BLOCK
        gate = jnp.einsum('bhqd,bhnd->bhqn', q_blk, kmean)
        gate = jnp.where(blk_ids[None, :] < own, gate, -jnp.inf)
        _, idx = lax.top_k(gate, topk)
        valid = jnp.arange(topk) < own
        k_sel = gather(kb, idx)
        s_sel = jnp.einsum('bhqd,bhqkjd->bhqkj', q_blk, k_sel) * scale
        s_sel = jnp.where(valid[:, None], s_sel, -jnp.inf).reshape(bsz, MOBA_HEADS, MOBA_Q_CHUNK, topk * MOBA_BLOCK)
        k_own = lax.dynamic_slice_in_dim(kb, own, 1, axis=2)[:, :, 0]
        v_own = lax.dynamic_slice_in_dim(vb, own, 1, axis=2)[:, :, 0]
        s_own = jnp.einsum('bhqd,bhjd->bhqj', q_blk, k_own) * scale
        own_pos = own * MOBA_BLOCK + own_offsets
        s_own = jnp.where(own_pos[None, :] <= pos[:, None], s_own, -jnp.inf)
        p = jax.nn.softmax(jnp.concatenate([s_sel, s_own], axis=-1), axis=-1)
        p_sel = p[..., :topk * MOBA_BLOCK].reshape(bsz, MOBA_HEADS, MOBA_Q_CHUNK, topk, MOBA_BLOCK)
        p_own = p[..., topk * MOBA_BLOCK:]
        v_sel = gather(vb, idx)
        return (jnp.einsum('bhqkj,bhqkjd->bhqd', p_sel, v_sel)
                + jnp.einsum('bhqj,bhjd->bhqd', p_own, v_own))

    out = lax.map(chunk_fn, (qc, jnp.arange(n_chunks)))
    return out.transpose(1, 0, 3, 2, 4).reshape(bsz, s, MOBA_HEADS, MOBA_HEAD_DIM)


def forgetting_attention(q, k, v, log_f):
    bsz, s = q.shape[0], q.shape[1]
    nqb = s // FOX_Q_BLOCK
    scale = FOX_HEAD_DIM ** -0.5
    ch = jnp.cumsum(log_f, axis=1).transpose(0, 2, 1)
    kh = k.transpose(0, 2, 1, 3)
    vh = v.transpose(0, 2, 1, 3)
    qb = q.reshape(bsz, nqb, FOX_Q_BLOCK, FOX_HEADS, FOX_HEAD_DIM).transpose(1, 0, 3, 2, 4)
    cq = ch.reshape(bsz, FOX_HEADS, nqb, FOX_Q_BLOCK).transpose(2, 0, 1, 3)
    key_pos = jnp.arange(s)

    def blk(args):
        q_blk, c_blk, bi = args
        qpos = bi * FOX_Q_BLOCK + jnp.arange(FOX_Q_BLOCK)
        logits = (jnp.einsum('bhqd,bhkd->bhqk', q_blk, kh) * scale
                  + (c_blk[..., :, None] - ch[..., None, :]))
        logits = jnp.where(key_pos[None, :] <= qpos[:, None], logits, -jnp.inf)
        return jnp.einsum('bhqk,bhkd->bhqd', jax.nn.softmax(logits, axis=-1), vh)

    out = lax.map(blk, (qb, cq, jnp.arange(nqb)))
    return out.transpose(1, 0, 3, 2, 4).reshape(bsz, s, FOX_INNER)


def ssd_moba_layer(h, w_in, conv_w, conv_b, dt_bias, a_log, d_skip, gate_norm, w_out):
    bsz, s = h.shape[0], h.shape[1]
    proj = h @ w_in
    o1 = SSD_INNER
    o2 = o1 + SSD_XBC
    o3 = o2 + SSD_HEADS
    o4 = o3 + MOBA_INNER
    o5 = o4 + MOBA_INNER
    z, xbc, dt_raw = proj[..., :o1], proj[..., o1:o2], proj[..., o2:o3]
    q = proj[..., o3:o4].reshape(bsz, s, MOBA_HEADS, MOBA_HEAD_DIM)
    k = proj[..., o4:o5].reshape(bsz, s, MOBA_HEADS, MOBA_HEAD_DIM)
    v = proj[..., o5:].reshape(bsz, s, MOBA_HEADS, MOBA_HEAD_DIM).astype(jnp.float32)
    y_ssd = ssd_mixer(z, xbc, dt_raw, conv_w, conv_b, dt_bias, a_log, d_skip, gate_norm)
    y_att = moba_attention(rope(q), rope(k), v).reshape(bsz, s, MOBA_INNER).astype(h.dtype)
    return jnp.concatenate([y_ssd, y_att], axis=-1) @ w_out


def fox_layer(h, w_in, fgate_bias, w_out):
    bsz, s = h.shape[0], h.shape[1]
    proj = (h @ w_in).astype(jnp.float32)
    q = proj[..., :FOX_INNER].reshape(bsz, s, FOX_HEADS, FOX_HEAD_DIM)
    k = proj[..., FOX_INNER:2 * FOX_INNER].reshape(bsz, s, FOX_HEADS, FOX_HEAD_DIM)
    v = proj[..., 2 * FOX_INNER:3 * FOX_INNER].reshape(bsz, s, FOX_HEADS, FOX_HEAD_DIM)
    log_f = jax.nn.log_sigmoid(proj[..., 3 * FOX_INNER:] + fgate_bias.astype(jnp.float32))
    return forgetting_attention(q, k, v, log_f).astype(h.dtype) @ w_out


def sq_relu_mlp(h, w_up, w_down):
    return jnp.square(jax.nn.relu(h @ w_up)) @ w_down


def setup_inputs(seed: int = 0) -> dict:
    key = jax.random.key(seed)
    ks = jax.random.split(key, 20)
    f32 = jnp.float32

    def nrm(k, shape, fan_in):
        return jax.random.normal(k, shape, f32) * fan_in ** -0.5

    def gain(k, shape):
        return 1.0 + 0.02 * jax.random.normal(k, shape, f32)

    x = jax.random.normal(ks[0], (BATCH, SEQ, D_MODEL), f32)
    dt = jnp.exp(jax.random.uniform(ks[5], (N_EVEN, SSD_HEADS), f32, math.log(1e-3), math.log(1e-1)))
    return {
        'x': x,
        'norm_mix_even': gain(ks[1], (N_EVEN, D_MODEL)),
        'w_in_even': nrm(ks[2], (N_EVEN, D_MODEL, EVEN_IN), D_MODEL),
        'conv_w': nrm(ks[3], (N_EVEN, SSD_CONV, SSD_XBC), SSD_CONV),
        'conv_b': 0.01 * jax.random.normal(ks[4], (N_EVEN, SSD_XBC), f32),
        'dt_bias': dt + jnp.log(-jnp.expm1(-dt)),
        'a_log': jnp.log(jax.random.uniform(ks[6], (N_EVEN, SSD_HEADS), f32, 1.0, 16.0)),
        'd_skip': 1.0 + 0.1 * jax.random.normal(ks[7], (N_EVEN, SSD_HEADS), f32),
        'ssd_gate_norm': gain(ks[8], (N_EVEN, SSD_INNER)),
        'w_out_even': nrm(ks[9], (N_EVEN, EVEN_OUT, D_MODEL), EVEN_OUT),
        'norm_mix_odd': gain(ks[10], (N_ODD, D_MODEL)),
        'w_in_odd': nrm(ks[11], (N_ODD, D_MODEL, ODD_IN), D_MODEL),
        'fgate_bias': jax.random.uniform(ks[12], (N_ODD, FOX_HEADS), f32, 1.0, 5.0),
        'w_out_odd': nrm(ks[13], (N_ODD, FOX_INNER, D_MODEL), FOX_INNER),
        'norm_mlp': gain(ks[14], (DEPTH, D_MODEL)),
        'w_up': nrm(ks[15], (DEPTH, D_MODEL, D_FF), D_MODEL),
        'w_down': nrm(ks[16], (DEPTH, D_FF, D_MODEL), D_FF),
        'final_norm': gain(ks[17], (D_MODEL,)),
    }


def reference(x, norm_mix_even, w_in_even, conv_w, conv_b, dt_bias, a_log, d_skip, ssd_gate_norm,
              w_out_even, norm_mix_odd, w_in_odd, fgate_bias, w_out_odd, norm_mlp, w_up, w_down,
              final_norm):
    for layer in range(DEPTH):
        i = layer // 2
        if layer % 2 == 0:
            mix = ssd_moba_layer(rms_norm(x, norm_mix_even[i]), w_in_even[i], conv_w[i], conv_b[i],
                                 dt_bias[i], a_log[i], d_skip[i], ssd_gate_norm[i], w_out_even[i])
        else:
            mix = fox_layer(rms_norm(x, norm_mix_odd[i]), w_in_odd[i], fgate_bias[i], w_out_odd[i])
        x = x + mix.astype(x.dtype)
        x = x + sq_relu_mlp(rms_norm(x, norm_mlp[layer]), w_up[layer], w_down[layer]).astype(x.dtype)
    return rms_norm(x, final_norm)
```

```python
import functools
import math

import jax
import jax.numpy as jnp
from jax import lax
from jax.experimental import pallas as pl
from jax.experimental.pallas import tpu as pltpu

NORM_EPS = 1e-5
ROPE_THETA = 10000.0
LANES = 128
HEAD_DIM = 64
SSD_HEADS = 16
SSD_GROUPS = 4
SSD_STATE = 128
SSD_CONV = 4
SSD_CHUNK = 128
SSD_INNER = SSD_HEADS * HEAD_DIM
SSD_GN = SSD_GROUPS * SSD_STATE
SSD_XBC = SSD_INNER + 2 * SSD_GN
MOBA_HEADS = 8
MOBA_INNER = MOBA_HEADS * HEAD_DIM
MOBA_BLOCK = 256
MOBA_TOPK = 3
FOX_HEADS = 16
FOX_INNER = FOX_HEADS * HEAD_DIM
ATT_TILE = 256
VMEM_LIMIT = 56 * 1024 * 1024

F32 = jnp.float32
BF16 = jnp.bfloat16
NEG_INF = float("-inf")


def _cparams(sem):
    return pltpu.CompilerParams(dimension_semantics=sem, vmem_limit_bytes=VMEM_LIMIT)


def _rms(x, g):
    return x * lax.rsqrt(jnp.mean(x * x, axis=-1, keepdims=True) + NORM_EPS) * g


def _sigmoid(x):
    return 1.0 / (1.0 + jnp.exp(-x))


def _softplus(x):
    return jnp.maximum(x, 0.0) + jnp.log1p(jnp.exp(-jnp.abs(x)))


def _split3(x):
    hi = x.astype(BF16)
    r = x - hi.astype(F32)
    mid = r.astype(BF16)
    lo = (r - mid.astype(F32)).astype(BF16)
    return hi, mid, lo


def _dot(a, b):
    return jnp.dot(a, b, preferred_element_type=F32)


def _dot_nt(a, b):
    return lax.dot_general(a, b, (((1,), (1,)), ((), ())), preferred_element_type=F32)


def _dot_tn(a, b):
    return lax.dot_general(a, b, (((0,), (0,)), ((), ())), preferred_element_type=F32)


def _sel_dot_rhs(m01, x):
    hi, mid, lo = _split3(x)
    return _dot(m01, hi) + _dot(m01, mid) + _dot(m01, lo)


def _sel_dot_lhs(x, m01):
    hi, mid, lo = _split3(x)
    return _dot(hi, m01) + _dot(mid, m01) + _dot(lo, m01)


def _norm_matmul_kernel(x_ref, g_ref, w_ref, *out_refs, splits):
    xn = _rms(x_ref[...], g_ref[...]).astype(BF16)
    for o_ref, (a, b) in zip(out_refs, splits):
        o_ref[...] = _dot(xn, w_ref[:, a:b]).astype(o_ref.dtype)


def _norm_matmul(x, g, w, widths, tm=256):
    t, d = x.shape
    n = w.shape[1]
    splits, a = [], 0
    for wd in widths:
        splits.append((a, a + wd))
        a += wd
    assert a == n and t % tm == 0
    return pl.pallas_call(
        functools.partial(_norm_matmul_kernel, splits=tuple(splits)),
        grid=(t // tm,),
        in_specs=[pl.BlockSpec((tm, d), lambda i: (i, 0)),
                  pl.BlockSpec((1, d), lambda i: (0, 0)),
                  pl.BlockSpec((d, n), lambda i: (0, 0))],
        out_specs=[pl.BlockSpec((tm, wd), lambda i: (i, 0)) for wd in widths],
        out_shape=[jax.ShapeDtypeStruct((t, wd), F32) for wd in widths],
        compiler_params=_cparams(("parallel",)),
        name="norm_matmul",
    )(x, g.reshape(1, d), w)


def _proj_res_kernel(*refs, n):
    a_refs, w_refs = refs[:n], refs[n:2 * n]
    res_ref, o_ref = refs[2 * n], refs[2 * n + 1]
    acc = res_ref[...]
    for a_ref, w_ref in zip(a_refs, w_refs):
        acc = acc + _dot(a_ref[...], w_ref[...])
    o_ref[...] = acc


def _proj_res(acts, ws, res, tm=512):
    t, d = res.shape
    n = len(acts)
    in_specs = ([pl.BlockSpec((tm, a.shape[1]), lambda i: (i, 0)) for a in acts]
                + [pl.BlockSpec(w.shape, lambda i: (0, 0)) for w in ws]
                + [pl.BlockSpec((tm, d), lambda i: (i, 0))])
    return pl.pallas_call(
        functools.partial(_proj_res_kernel, n=n),
        grid=(t // tm,),
        in_specs=in_specs,
        out_specs=pl.BlockSpec((tm, d), lambda i: (i, 0)),
        out_shape=jax.ShapeDtypeStruct((t, d), F32),
        compiler_params=_cparams(("parallel",)),
        name="proj_res",
    )(*acts, *ws, res)


def _mlp_kernel(x_ref, g_ref, wu_ref, wd_ref, gf_ref, o_ref, xn_scr, acc_scr, *, final_norm):
    j = pl.program_id(1)

    @pl.when(j == 0)
    def _():
        x = x_ref[...]
        xn_scr[...] = _rms(x, g_ref[...]).astype(BF16)
        acc_scr[...] = x

    h = _dot(xn_scr[...], wu_ref[...])
    h = jnp.square(jnp.maximum(h, 0.0)).astype(BF16)
    acc_scr[...] += _dot(h, wd_ref[...])

    @pl.when(j == pl.num_programs(1) - 1)
    def _():
        y = acc_scr[...]
        if final_norm:
            y = _rms(y, gf_ref[...])
        o_ref[...] = y


def _mlp(x, g, w_up, w_down, g_final, final_norm, tm=512, tf=1024):
    t, d = x.shape
    f = w_up.shape[1]
    return pl.pallas_call(
        functools.partial(_mlp_kernel, final_norm=final_norm),
        grid=(t // tm, f // tf),
        in_specs=[pl.BlockSpec((tm, d), lambda i, j: (i, 0)),
                  pl.BlockSpec((1, d), lambda i, j: (0, 0)),
                  pl.BlockSpec((d, tf), lambda i, j: (0, j)),
                  pl.BlockSpec((tf, d), lambda i, j: (j, 0)),
                  pl.BlockSpec((1, d), lambda i, j: (0, 0))],
        out_specs=pl.BlockSpec((tm, d), lambda i, j: (i, 0)),
        out_shape=jax.ShapeDtypeStruct((t, d), F32),
        scratch_shapes=[pltpu.VMEM((tm, d), BF16), pltpu.VMEM((tm, d), F32)],
        compiler_params=_cparams(("parallel", "arbitrary")),
        name="mlp",
    )(x, g.reshape(1, d), w_up, w_down, g_final.reshape(1, d))


def _ssd_kernel(z_ref, xbc_ref, dt_ref, cw_ref, cb_ref, dtb_ref, alog_ref, dskip_ref, gn_ref,
                e_ref, o_ref, ubuf, hst, y_scr):
    L = SSD_CHUNK
    c = pl.program_id(1)

    @pl.when(c == 0)
    def _():
        hst[...] = jnp.zeros_like(hst)
        ubuf[0:8, :] = jnp.zeros((8, SSD_XBC), F32)

    ubuf[8:8 + L, :] = xbc_ref[...]
    conv = cb_ref[...]
    for k in range(SSD_CONV):
        conv = conv + cw_ref[k:k + 1, :] * ubuf[5 + k:5 + k + L, :]
    ubuf[0:8, :] = ubuf[L:L + 8, :]
    xa = conv * _sigmoid(conv)
    xs = xa[:, :SSD_INNER]

    dt = _softplus(dt_ref[...] + dtb_ref[...])
    ad = dt * (-jnp.exp(alog_ref[...]))
    row = lax.broadcasted_iota(jnp.int32, (L, L), 0)
    col = lax.broadcasted_iota(jnp.int32, (L, L), 1)
    causal = row >= col
    acum = _sel_dot_rhs(causal.astype(BF16), ad)
    e01 = e_ref[...]
    dtx = _sel_dot_lhs(dt, e01)
    acx = _sel_dot_lhs(acum, e01)
    alast = acx[L - 1:L, :]
    xc = xs * dtx
    xcd = (xc * jnp.exp(alast - acx)).astype(BF16)
    eac = jnp.exp(acx)
    cdx = jnp.exp(alast)
    act = acum.T
    lane = lax.broadcasted_iota(jnp.int32, (L, LANES), 1)

    gw = SSD_INNER // SSD_GROUPS
    for g in range(SSD_GROUPS):
        bg = xa[:, SSD_INNER + g * SSD_STATE:SSD_INNER + (g + 1) * SSD_STATE].astype(BF16)
        cg = xa[:, SSD_INNER + SSD_GN + g * SSD_STATE:SSD_INNER + SSD_GN + (g + 1) * SSD_STATE].astype(BF16)
        cb = _dot_nt(cg, bg)
        hprev = hst[g]
        y_scr[:, g * gw:(g + 1) * gw] = _dot(cg, hprev.astype(BF16)) * eac[:, g * gw:(g + 1) * gw]
        st = _dot_tn(bg, xcd[:, g * gw:(g + 1) * gw])
        hst[g] = hprev * cdx[:, g * gw:(g + 1) * gw] + st
        for pr in range(gw // LANES):
            lo = g * gw + pr * LANES
            xcp = xc[:, lo:lo + LANES]
            yp = jnp.zeros((L, LANES), F32)
            for hh in range(LANES // HEAD_DIM):
                h = lo // HEAD_DIM + hh
                dec = jnp.exp(jnp.where(causal, acum[:, h:h + 1] - act[h:h + 1, :], NEG_INF))
                m = (cb * dec).astype(BF16)
                in_head = (lane >= hh * HEAD_DIM) & (lane < (hh + 1) * HEAD_DIM)
                yp = yp + _dot(m, jnp.where(in_head, xcp, 0.0).astype(BF16))
            y_scr[:, lo:lo + LANES] += yp

    y = y_scr[...] + xs * dskip_ref[...]
    z = z_ref[...]
    y = y * (z * _sigmoid(z))
    for g in range(SSD_GROUPS):
        yg = y[:, g * gw:(g + 1) * gw]
        ms = jnp.mean(yg * yg, axis=-1, keepdims=True)
        o_ref[:, g * gw:(g + 1) * gw] = (yg * lax.rsqrt(ms + NORM_EPS) * gn_ref[:, g * gw:(g + 1) * gw]).astype(o_ref.dtype)


def _pad_lanes(v):
    return jnp.pad(v.astype(F32), (0, LANES - v.shape[0])).reshape(1, LANES)


def _ssd(z, xbc, dt, conv_w, conv_b, dt_bias, a_log, d_skip, gate_norm, bsz, s):
    L = SSD_CHUNK
    nc = s // L
    e01 = (jnp.arange(LANES)[:, None] == (jnp.arange(SSD_INNER) // HEAD_DIM)[None, :]).astype(BF16)
    dsx = jnp.repeat(d_skip.astype(F32), HEAD_DIM).reshape(1, SSD_INNER)
    tok = lambda b, c: (b * nc + c, 0)
    const = lambda b, c: (0, 0)
    return pl.pallas_call(
        _ssd_kernel,
        grid=(bsz, nc),
        in_specs=[pl.BlockSpec((L, SSD_INNER), tok),
                  pl.BlockSpec((L, SSD_XBC), tok),
                  pl.BlockSpec((L, LANES), tok),
                  pl.BlockSpec((SSD_CONV, SSD_XBC), const),
                  pl.BlockSpec((1, SSD_XBC), const),
                  pl.BlockSpec((1, LANES), const),
                  pl.BlockSpec((1, LANES), const),
                  pl.BlockSpec((1, SSD_INNER), const),
                  pl.BlockSpec((1, SSD_INNER), const),
                  pl.BlockSpec((LANES, SSD_INNER), const)],
        out_specs=pl.BlockSpec((L, SSD_INNER), tok),
        out_shape=jax.ShapeDtypeStruct((bsz * s, SSD_INNER), BF16),
        scratch_shapes=[pltpu.VMEM((8 + L, SSD_XBC), F32),
                        pltpu.VMEM((SSD_GROUPS, SSD_STATE, SSD_INNER // SSD_GROUPS), F32),
                        pltpu.VMEM((L, SSD_INNER), F32)],
        compiler_params=_cparams(("parallel", "arbitrary")),
        name="ssd",
    )(z, xbc, dt, conv_w.astype(F32), conv_b.astype(F32).reshape(1, SSD_XBC), _pad_lanes(dt_bias),
      _pad_lanes(a_log), dsx, gate_norm.astype(F32).reshape(1, SSD_INNER), e01)


def _swap_halves(x):
    lane = lax.broadcasted_iota(jnp.int32, x.shape, 1)
    first = (lane % HEAD_DIM) < HEAD_DIM // 2
    return jnp.where(first, pltpu.roll(x, LANES - HEAD_DIM // 2, 1), pltpu.roll(x, HEAD_DIM // 2, 1))


def _rope(x, cos, sin_signed):
    return x * cos + _swap_halves(x) * sin_signed


def _head_lanes(shape, hh):
    lane = lax.broadcasted_iota(jnp.int32, shape, 1)
    return (lane >= hh * HEAD_DIM) & (lane < (hh + 1) * HEAD_DIM)


def _own_block(s_t, vt, tq):
    krow = lax.broadcasted_iota(jnp.int32, (tq, tq), 0)
    qcol = lax.broadcasted_iota(jnp.int32, (tq, tq), 1)
    s_t = jnp.where(krow <= qcol, s_t, NEG_INF)
    m = jnp.max(s_t, axis=0, keepdims=True)
    p = jnp.exp(s_t - m)
    l = jnp.sum(p, axis=0, keepdims=True)
    acc = _dot(vt, p.astype(BF16))
    return m, l, acc


def _past_block(s_t, vt, m, l, acc):
    m_new = jnp.maximum(m, jnp.max(s_t, axis=0, keepdims=True))
    alpha = jnp.exp(m - m_new)
    p = jnp.exp(s_t - m_new)
    l = alpha * l + jnp.sum(p, axis=0, keepdims=True)
    acc = alpha * acc + _dot(vt, p.astype(BF16))
    return m_new, l, acc


def _moba_kernel(q_ref, k_ref, v_ref, cos_ref, sin_ref, o_ref, kr_scr, vt_scr, km_scr, bias_scr):
    tq = ATT_TILE
    nb = kr_scr.shape[0]
    i = pl.program_id(2)
    scale = HEAD_DIM ** -0.5

    @pl.when(i == 0)
    def _():
        for jb in range(nb):
            rows = slice(jb * tq, (jb + 1) * tq)
            kr = _rope(k_ref[rows, :], cos_ref[rows, :], sin_ref[rows, :])
            kr_scr[jb] = kr.astype(BF16)
            km_scr[jb:jb + 1, :] = jnp.mean(kr, axis=0, keepdims=True)
            vt_scr[jb] = v_ref[rows, :].T.astype(BF16)

    qrows = pl.ds(pl.multiple_of(i * tq, tq), tq)
    qr = _rope(q_ref[...], cos_ref[qrows, :], sin_ref[qrows, :])
    jrow = lax.broadcasted_iota(jnp.int32, (nb, tq), 0)
    valid = jrow < i
    nheads = LANES // HEAD_DIM
    qs, stats = [], []
    for hh in range(nheads):
        qm = jnp.where(_head_lanes(qr.shape, hh), qr, 0.0)
        gate = lax.dot_general(km_scr[...], qm, (((1,), (1,)), ((), ())),
                               precision=lax.Precision.HIGHEST, preferred_element_type=F32)
        gate = jnp.where(valid, gate, NEG_INF)
        cnt = jnp.zeros((nb, tq), jnp.int32)
        for jp in range(nb):
            gj = gate[jp:jp + 1, :]
            beats = (gj > gate) | ((gj == gate) & (jp < jrow))
            cnt = cnt + (beats & (jp < i)).astype(jnp.int32)
        chosen = valid & (cnt < MOBA_TOPK)
        bias_scr[hh] = jnp.where(chosen, 0.0, NEG_INF)
        q_h = (qm * scale).astype(BF16)
        qs.append(q_h)
        rows_h = slice(hh * HEAD_DIM, (hh + 1) * HEAD_DIM)
        stats.append(_own_block(_dot_nt(kr_scr[i], q_h), vt_scr[i, rows_h, :], tq))

    def body(j, carry):
        out = []
        for hh in range(nheads):
            m, l, acc = carry[hh]
            s_t = _dot_nt(kr_scr[j], qs[hh]) + bias_scr[hh, pl.ds(j, 1), :]
            out.append(_past_block(s_t, vt_scr[j, hh * HEAD_DIM:(hh + 1) * HEAD_DIM, :], m, l, acc))
        return tuple(out)

    stats = lax.fori_loop(0, i, body, tuple(stats))
    o_t = jnp.concatenate([acc / l for (_, l, acc) in stats], axis=0)
    o_ref[...] = o_t.T.astype(o_ref.dtype)


def _moba(q, k, v, cos, sin_signed, bsz, s):
    tq = ATT_TILE
    nb = s // tq
    npair = MOBA_INNER // LANES
    q3, k3, v3 = (a.reshape(bsz, s, MOBA_INNER) for a in (q, k, v))
    whole = pl.BlockSpec((None, s, LANES), lambda b, p, i: (b, 0, p))
    tab = pl.BlockSpec((s, LANES), lambda b, p, i: (0, 0))
    out = pl.pallas_call(
        _moba_kernel,
        grid=(bsz, npair, nb),
        in_specs=[pl.BlockSpec((None, tq, LANES), lambda b, p, i: (b, i, p)), whole, whole, tab, tab],
        out_specs=pl.BlockSpec((None, tq, LANES), lambda b, p, i: (b, i, p)),
        out_shape=jax.ShapeDtypeStruct((bsz, s, MOBA_INNER), BF16),
        scratch_shapes=[pltpu.VMEM((nb, tq, LANES), BF16),
                        pltpu.VMEM((nb, LANES, tq), BF16),
                        pltpu.VMEM((nb, LANES), F32),
                        pltpu.VMEM((LANES // HEAD_DIM, nb, tq), F32)],
        compiler_params=_cparams(("parallel", "parallel", "arbitrary")),
        name="moba",
    )(q3, k3, v3, cos, sin_signed)
    return out.reshape(bsz * s, MOBA_INNER)


def _fox_gate_kernel(f_ref, b_ref, c_ref, ct_ref):
    tb = ATT_TILE
    s = f_ref.shape[0]
    row = lax.broadcasted_iota(jnp.int32, (tb, tb), 0)
    col = lax.broadcasted_iota(jnp.int32, (tb, tb), 1)
    tril = (row >= col).astype(BF16)
    carry = jnp.zeros((1, LANES), F32)
    for jb in range(s // tb):
        rows = slice(jb * tb, (jb + 1) * tb)
        log_f = -_softplus(-(f_ref[rows, :] + b_ref[...]))
        cblk = _sel_dot_rhs(tril, log_f) + carry
        c_ref[rows, :] = cblk
        ct_ref[:, rows] = cblk.T
        carry = cblk[tb - 1:tb, :]


def _fox_gate(f, bias, bsz, s):
    return pl.pallas_call(
        _fox_gate_kernel,
        grid=(bsz,),
        in_specs=[pl.BlockSpec((None, s, LANES), lambda b: (b, 0, 0)),
                  pl.BlockSpec((1, LANES), lambda b: (0, 0))],
        out_specs=[pl.BlockSpec((None, s, LANES), lambda b: (b, 0, 0)),
                   pl.BlockSpec((None, LANES, s), lambda b: (b, 0, 0))],
        out_shape=[jax.ShapeDtypeStruct((bsz, s, LANES), F32),
                   jax.ShapeDtypeStruct((bsz, LANES, s), F32)],
        compiler_params=_cparams(("parallel",)),
        name="fox_gate",
    )(f.reshape(bsz, s, LANES), _pad_lanes(bias))


def _fox_kernel(q_ref, k_ref, v_ref, c_ref, ct_ref, o_ref, k_scr, vt_scr, ck_scr):
    tq = ATT_TILE
    nb = k_scr.shape[0]
    p = pl.program_id(1)
    i = pl.program_id(2)
    scale = HEAD_DIM ** -0.5
    nheads = LANES // HEAD_DIM

    @pl.when(i == 0)
    def _():
        lane = lax.broadcasted_iota(jnp.int32, (tq, LANES), 1)
        for jb in range(nb):
            rows = slice(jb * tq, (jb + 1) * tq)
            k_scr[jb] = k_ref[rows, :].astype(BF16)
            vt_scr[jb] = v_ref[rows, :].T.astype(BF16)
            cblk = c_ref[rows, :]
            for hh in range(nheads):
                colv = jnp.sum(jnp.where(lane == p * nheads + hh, cblk, 0.0), axis=1, keepdims=True)
                ck_scr[hh, jb] = jnp.broadcast_to(colv, (tq, LANES))

    q = q_ref[...]
    qs, cqs, stats = [], [], []
    for hh in range(nheads):
        q_h = (jnp.where(_head_lanes(q.shape, hh), q, 0.0) * scale).astype(BF16)
        cq = ct_ref[pl.ds(p * nheads + hh, 1), :]
        ck = ck_scr[hh, i]
        s_t = _dot_nt(k_scr[i], q_h) - jnp.concatenate([ck] * (tq // LANES), axis=1) + cq
        qs.append(q_h)
        cqs.append(cq)
        stats.append(_own_block(s_t, vt_scr[i, hh * HEAD_DIM:(hh + 1) * HEAD_DIM, :], tq))

    def body(j, carry):
        out = []
        for hh in range(nheads):
            m, l, acc = carry[hh]
            ck = ck_scr[hh, j]
            s_t = _dot_nt(k_scr[j], qs[hh]) - jnp.concatenate([ck] * (tq // LANES), axis=1) + cqs[hh]
            out.append(_past_block(s_t, vt_scr[j, hh * HEAD_DIM:(hh + 1) * HEAD_DIM, :], m, l, acc))
        return tuple(out)

    stats = lax.fori_loop(0, i, body, tuple(stats))
    o_t = jnp.concatenate([acc / l for (_, l, acc) in stats], axis=0)
    o_ref[...] = o_t.T.astype(o_ref.dtype)


def _fox(q, k, v, c, ct, bsz, s):
    tq = ATT_TILE
    nb = s // tq
    npair = FOX_INNER // LANES
    q3, k3, v3 = (a.reshape(bsz, s, FOX_INNER) for a in (q, k, v))
    whole = pl.BlockSpec((None, s, LANES), lambda b, p, i: (b, 0, p))
    out = pl.pallas_call(
        _fox_kernel,
        grid=(bsz, npair, nb),
        in_specs=[pl.BlockSpec((None, tq, LANES), lambda b, p, i: (b, i, p)), whole, whole,
                  pl.BlockSpec((None, s, LANES), lambda b, p, i: (b, 0, 0)),
                  pl.BlockSpec((None, LANES, tq), lambda b, p, i: (b, 0, i))],
        out_specs=pl.BlockSpec((None, tq, LANES), lambda b, p, i: (b, i, p)),
        out_shape=jax.ShapeDtypeStruct((bsz, s, FOX_INNER), BF16),
        scratch_shapes=[pltpu.VMEM((nb, tq, LANES), BF16),
                        pltpu.VMEM((nb, LANES, tq), BF16),
                        pltpu.VMEM((LANES // HEAD_DIM, nb, tq, LANES), F32)],
        compiler_params=_cparams(("parallel", "parallel", "arbitrary")),
        name="fox",
    )(q3, k3, v3, c, ct)
    return out.reshape(bsz * s, FOX_INNER)


def _rope_tables(s):
    half = HEAD_DIM // 2
    inv = jnp.power(ROPE_THETA, -jnp.arange(half, dtype=F32) / half)
    ang = jnp.arange(s, dtype=F32)[:, None] * inv[None, :]
    cos, sin = jnp.cos(ang), jnp.sin(ang)
    reps = LANES // HEAD_DIM
    return (jnp.tile(jnp.concatenate([cos, cos], axis=1), (1, reps)),
            jnp.tile(jnp.concatenate([-sin, sin], axis=1), (1, reps)))


def _even_layer(x, bsz, s, g, w_in, conv_w, conv_b, dt_bias, a_log, d_skip, gate_norm, w_out, rope_tabs):
    d = x.shape[1]
    o1 = SSD_INNER
    o2 = o1 + SSD_XBC
    o3 = o2 + SSD_HEADS
    w = jnp.concatenate([w_in[:, :o2], w_in[:, o3:], w_in[:, o2:o3],
                         jnp.zeros((d, LANES - SSD_HEADS), w_in.dtype)], axis=1).astype(BF16)
    z, xbc, q, k, v, dt = _norm_matmul(
        x, g, w, (SSD_INNER, SSD_XBC, MOBA_INNER, MOBA_INNER, MOBA_INNER, LANES))
    y_ssd = _ssd(z, xbc, dt, conv_w, conv_b, dt_bias, a_log, d_skip, gate_norm, bsz, s)
    y_att = _moba(q, k, v, *rope_tabs, bsz, s)
    w_o = w_out.astype(BF16)
    return _proj_res([y_ssd, y_att], [w_o[:SSD_INNER], w_o[SSD_INNER:]], x)


def _odd_layer(x, bsz, s, g, w_in, fgate_bias, w_out):
    d = x.shape[1]
    w = jnp.concatenate([w_in, jnp.zeros((d, LANES - FOX_HEADS), w_in.dtype)], axis=1).astype(BF16)
    q, k, v, f = _norm_matmul(x, g, w, (FOX_INNER, FOX_INNER, FOX_INNER, LANES))
    c, ct = _fox_gate(f, fgate_bias, bsz, s)
    y = _fox(q, k, v, c, ct, bsz, s)
    return _proj_res([y], [w_out.astype(BF16)], x)


def kernel(x, norm_mix_even, w_in_even, conv_w, conv_b, dt_bias, a_log, d_skip, ssd_gate_norm, w_out_even,
           norm_mix_odd, w_in_odd, fgate_bias, w_out_odd, norm_mlp, w_up, w_down, final_norm):
    bsz, s, d = x.shape
    depth = norm_mlp.shape[0]
    assert s % ATT_TILE == 0 and s % SSD_CHUNK == 0
    h = x.reshape(bsz * s, d)
    rope_tabs = _rope_tables(s)
    for layer in range(depth):
        i = layer // 2
        if layer % 2 == 0:
            h = _even_layer(h, bsz, s, norm_mix_even[i], w_in_even[i], conv_w[i], conv_b[i], dt_bias[i],
                            a_log[i], d_skip[i], ssd_gate_norm[i], w_out_even[i], rope_tabs)
        else:
            h = _odd_layer(h, bsz, s, norm_mix_odd[i], w_in_odd[i], fgate_bias[i], w_out_odd[i])
        h = _mlp(h, norm_mlp[layer], w_up[layer].astype(BF16), w_down[layer].astype(BF16), final_norm,
                 final_norm=(layer == depth - 1))
    return h.reshape(bsz, s, d)
```

```python
import functools
import math

import jax
import jax.numpy as jnp
from jax import lax
from jax.experimental import pallas as pl
from jax.experimental.pallas import tpu as pltpu

NORM_EPS = 1e-5
ROPE_THETA = 10000.0
LANES = 128
HEAD_DIM = 64
SSD_HEADS = 16
SSD_GROUPS = 4
SSD_STATE = 128
SSD_CONV = 4
SSD_CHUNK = 128
SSD_INNER = SSD_HEADS * HEAD_DIM
SSD_GN = SSD_GROUPS * SSD_STATE
SSD_XBC = SSD_INNER + 2 * SSD_GN
MOBA_HEADS = 8
MOBA_INNER = MOBA_HEADS * HEAD_DIM
MOBA_BLOCK = 256
MOBA_TOPK = 3
FOX_HEADS = 16
FOX_INNER = FOX_HEADS * HEAD_DIM
ATT_TILE = 256
VMEM_LIMIT = 56 * 1024 * 1024

F32 = jnp.float32
BF16 = jnp.bfloat16
NEG_INF = float("-inf")
LOG2E = math.log2(math.e)


def _cparams(sem):
    return pltpu.CompilerParams(dimension_semantics=sem, vmem_limit_bytes=VMEM_LIMIT)


def _rms(x, g):
    return x * lax.rsqrt(jnp.mean(x * x, axis=-1, keepdims=True) + NORM_EPS) * g


def _sigmoid(x):
    return 1.0 / (1.0 + jnp.exp(-x))


def _softplus(x):
    return jnp.maximum(x, 0.0) + jnp.log1p(jnp.exp(-jnp.abs(x)))


def _split3(x):
    hi = x.astype(BF16)
    r = x - hi.astype(F32)
    mid = r.astype(BF16)
    lo = (r - mid.astype(F32)).astype(BF16)
    return hi, mid, lo


def _dot(a, b):
    return jnp.dot(a, b, preferred_element_type=F32)


def _dot_nt(a, b):
    return lax.dot_general(a, b, (((1,), (1,)), ((), ())), preferred_element_type=F32)


def _dot_tn(a, b):
    return lax.dot_general(a, b, (((0,), (0,)), ((), ())), preferred_element_type=F32)


def _sel_dot_rhs(m01, x):
    hi, mid, lo = _split3(x)
    return _dot(m01, hi) + _dot(m01, mid) + _dot(m01, lo)


def _sel_dot_lhs(x, m01):
    hi, mid, lo = _split3(x)
    return _dot(hi, m01) + _dot(mid, m01) + _dot(lo, m01)


def _norm_matmul_kernel(x_ref, g_ref, w_ref, *out_refs, splits):
    xn = _rms(x_ref[...], g_ref[...]).astype(BF16)
    for o_ref, (a, b) in zip(out_refs, splits):
        o_ref[...] = _dot(xn, w_ref[:, a:b]).astype(o_ref.dtype)


def _norm_matmul(x, g, w, widths, tm=256):
    t, d = x.shape
    n = w.shape[1]
    splits, a = [], 0
    for wd in widths:
        splits.append((a, a + wd))
        a += wd
    assert a == n and t % tm == 0
    return pl.pallas_call(
        functools.partial(_norm_matmul_kernel, splits=tuple(splits)),
        grid=(t // tm,),
        in_specs=[pl.BlockSpec((tm, d), lambda i: (i, 0)),
                  pl.BlockSpec((1, d), lambda i: (0, 0)),
                  pl.BlockSpec((d, n), lambda i: (0, 0))],
        out_specs=[pl.BlockSpec((tm, wd), lambda i: (i, 0)) for wd in widths],
        out_shape=[jax.ShapeDtypeStruct((t, wd), F32) for wd in widths],
        compiler_params=_cparams(("parallel",)),
        name="norm_matmul",
    )(x, g.reshape(1, d), w)


def _proj_res_kernel(*refs, n):
    a_refs, w_refs = refs[:n], refs[n:2 * n]
    res_ref, o_ref = refs[2 * n], refs[2 * n + 1]
    acc = res_ref[...]
    for a_ref, w_ref in zip(a_refs, w_refs):
        acc = acc + _dot(a_ref[...], w_ref[...])
    o_ref[...] = acc


def _proj_res(acts, ws, res, tm=512):
    t, d = res.shape
    n = len(acts)
    in_specs = ([pl.BlockSpec((tm, a.shape[1]), lambda i: (i, 0)) for a in acts]
                + [pl.BlockSpec(w.shape, lambda i: (0, 0)) for w in ws]
                + [pl.BlockSpec((tm, d), lambda i: (i, 0))])
    return pl.pallas_call(
        functools.partial(_proj_res_kernel, n=n),
        grid=(t // tm,),
        in_specs=in_specs,
        out_specs=pl.BlockSpec((tm, d), lambda i: (i, 0)),
        out_shape=jax.ShapeDtypeStruct((t, d), F32),
        compiler_params=_cparams(("parallel",)),
        name="proj_res",
    )(*acts, *ws, res)


def _mlp_kernel(x_ref, g_ref, wu_ref, wd_ref, gf_ref, o_ref, xn_scr, acc_scr, *, final_norm):
    j = pl.program_id(1)

    @pl.when(j == 0)
    def _():
        x = x_ref[...]
        xn_scr[...] = _rms(x, g_ref[...]).astype(BF16)
        acc_scr[...] = x

    h = _dot(xn_scr[...], wu_ref[...])
    h = jnp.square(jnp.maximum(h, 0.0)).astype(BF16)
    acc_scr[...] += _dot(h, wd_ref[...])

    @pl.when(j == pl.num_programs(1) - 1)
    def _():
        y = acc_scr[...]
        if final_norm:
            y = _rms(y, gf_ref[...])
        o_ref[...] = y


def _mlp(x, g, w_up, w_down, g_final, final_norm, tm=512, tf=1024):
    t, d = x.shape
    f = w_up.shape[1]
    return pl.pallas_call(
        functools.partial(_mlp_kernel, final_norm=final_norm),
        grid=(t // tm, f // tf),
        in_specs=[pl.BlockSpec((tm, d), lambda i, j: (i, 0)),
                  pl.BlockSpec((1, d), lambda i, j: (0, 0)),
                  pl.BlockSpec((d, tf), lambda i, j: (0, j)),
                  pl.BlockSpec((tf, d), lambda i, j: (j, 0)),
                  pl.BlockSpec((1, d), lambda i, j: (0, 0))],
        out_specs=pl.BlockSpec((tm, d), lambda i, j: (i, 0)),
        out_shape=jax.ShapeDtypeStruct((t, d), F32),
        scratch_shapes=[pltpu.VMEM((tm, d), BF16), pltpu.VMEM((tm, d), F32)],
        compiler_params=_cparams(("parallel", "arbitrary")),
        name="mlp",
    )(x, g.reshape(1, d), w_up, w_down, g_final.reshape(1, d))


def _ssd_kernel(z_ref, xbc_ref, dt_ref, cw_ref, cb_ref, dtb_ref, alog_ref, dskip_ref, gn_ref,
                e_ref, o_ref, ubuf, hst, y_scr):
    L = SSD_CHUNK
    c = pl.program_id(1)

    @pl.when(c == 0)
    def _():
        hst[...] = jnp.zeros_like(hst)
        ubuf[0:8, :] = jnp.zeros((8, SSD_XBC), F32)

    ubuf[8:8 + L, :] = xbc_ref[...]
    conv = cb_ref[...]
    for k in range(SSD_CONV):
        conv = conv + cw_ref[k:k + 1, :] * ubuf[5 + k:5 + k + L, :]
    ubuf[0:8, :] = ubuf[L:L + 8, :]
    xa = conv * _sigmoid(conv)
    xs = xa[:, :SSD_INNER]

    dt = _softplus(dt_ref[...] + dtb_ref[...])
    ad = dt * (-jnp.exp(alog_ref[...]))
    row = lax.broadcasted_iota(jnp.int32, (L, L), 0)
    col = lax.broadcasted_iota(jnp.int32, (L, L), 1)
    causal = row >= col
    acum = _sel_dot_rhs(causal.astype(BF16), ad)
    e01 = e_ref[...]
    dtx = _sel_dot_lhs(dt, e01)
    acx = _sel_dot_lhs(acum, e01)
    alast = acx[L - 1:L, :]
    xc = xs * dtx
    xcd = (xc * jnp.exp(alast - acx)).astype(BF16)
    eac = jnp.exp(acx)
    cdx = jnp.exp(alast)
    act = acum.T
    lane = lax.broadcasted_iota(jnp.int32, (L, LANES), 1)

    gw = SSD_INNER // SSD_GROUPS
    for g in range(SSD_GROUPS):
        bg = xa[:, SSD_INNER + g * SSD_STATE:SSD_INNER + (g + 1) * SSD_STATE].astype(BF16)
        cg = xa[:, SSD_INNER + SSD_GN + g * SSD_STATE:SSD_INNER + SSD_GN + (g + 1) * SSD_STATE].astype(BF16)
        cb = _dot_nt(cg, bg)
        hprev = hst[g]
        y_scr[:, g * gw:(g + 1) * gw] = _dot(cg, hprev.astype(BF16)) * eac[:, g * gw:(g + 1) * gw]
        st = _dot_tn(bg, xcd[:, g * gw:(g + 1) * gw])
        hst[g] = hprev * cdx[:, g * gw:(g + 1) * gw] + st
        for pr in range(gw // LANES):
            lo = g * gw + pr * LANES
            xcp = xc[:, lo:lo + LANES]
            yp = jnp.zeros((L, LANES), F32)
            for hh in range(LANES // HEAD_DIM):
                h = lo // HEAD_DIM + hh
                dec = jnp.exp(jnp.where(causal, acum[:, h:h + 1] - act[h:h + 1, :], NEG_INF))
                m = (cb * dec).astype(BF16)
                in_head = (lane >= hh * HEAD_DIM) & (lane < (hh + 1) * HEAD_DIM)
                yp = yp + _dot(m, jnp.where(in_head, xcp, 0.0).astype(BF16))
            y_scr[:, lo:lo + LANES] += yp

    y = y_scr[...] + xs * dskip_ref[...]
    z = z_ref[...]
    y = y * (z * _sigmoid(z))
    for g in range(SSD_GROUPS):
        yg = y[:, g * gw:(g + 1) * gw]
        ms = jnp.mean(yg * yg, axis=-1, keepdims=True)
        o_ref[:, g * gw:(g + 1) * gw] = (yg * lax.rsqrt(ms + NORM_EPS) * gn_ref[:, g * gw:(g + 1) * gw]).astype(o_ref.dtype)


def _pad_lanes(v):
    return jnp.pad(v.astype(F32), (0, LANES - v.shape[0])).reshape(1, LANES)


def _ssd(z, xbc, dt, conv_w, conv_b, dt_bias, a_log, d_skip, gate_norm, bsz, s):
    L = SSD_CHUNK
    nc = s // L
    e01 = (jnp.arange(LANES)[:, None] == (jnp.arange(SSD_INNER) // HEAD_DIM)[None, :]).astype(BF16)
    dsx = jnp.repeat(d_skip.astype(F32), HEAD_DIM).reshape(1, SSD_INNER)
    tok = lambda b, c: (b * nc + c, 0)
    const = lambda b, c: (0, 0)
    return pl.pallas_call(
        _ssd_kernel,
        grid=(bsz, nc),
        in_specs=[pl.BlockSpec((L, SSD_INNER), tok),
                  pl.BlockSpec((L, SSD_XBC), tok),
                  pl.BlockSpec((L, LANES), tok),
                  pl.BlockSpec((SSD_CONV, SSD_XBC), const),
                  pl.BlockSpec((1, SSD_XBC), const),
                  pl.BlockSpec((1, LANES), const),
                  pl.BlockSpec((1, LANES), const),
                  pl.BlockSpec((1, SSD_INNER), const),
                  pl.BlockSpec((1, SSD_INNER), const),
                  pl.BlockSpec((LANES, SSD_INNER), const)],
        out_specs=pl.BlockSpec((L, SSD_INNER), tok),
        out_shape=jax.ShapeDtypeStruct((bsz * s, SSD_INNER), BF16),
        scratch_shapes=[pltpu.VMEM((8 + L, SSD_XBC), F32),
                        pltpu.VMEM((SSD_GROUPS, SSD_STATE, SSD_INNER // SSD_GROUPS), F32),
                        pltpu.VMEM((L, SSD_INNER), F32)],
        compiler_params=_cparams(("parallel", "arbitrary")),
        name="ssd",
    )(z, xbc, dt, conv_w.astype(F32), conv_b.astype(F32).reshape(1, SSD_XBC), _pad_lanes(dt_bias),
      _pad_lanes(a_log), dsx, gate_norm.astype(F32).reshape(1, SSD_INNER), e01)


def _swap_halves(x):
    lane = lax.broadcasted_iota(jnp.int32, x.shape, 1)
    first = (lane % HEAD_DIM) < HEAD_DIM // 2
    return jnp.where(first, pltpu.roll(x, LANES - HEAD_DIM // 2, 1), pltpu.roll(x, HEAD_DIM // 2, 1))


def _rope(x, cos, sin_signed):
    return x * cos + _swap_halves(x) * sin_signed


def _fold8(x, op):
    parts = [x[r * 8:(r + 1) * 8, :] for r in range(x.shape[0] // 8)]
    return _tree(parts, op)


def _tree(parts, op):
    while len(parts) > 1:
        parts = [op(parts[a], parts[a + 1]) if a + 1 < len(parts) else parts[a]
                 for a in range(0, len(parts), 2)]
    return parts[0]


def _for_blocks(n, body):
    def quad(g, carry):
        body([4 * g + u for u in range(4)])
        return carry

    lax.fori_loop(0, n // 4, quad, 0)
    base = (n // 4) * 4

    @pl.when((n & 2) != 0)
    def _():
        body([base, base + 1])

    @pl.when((n & 1) != 0)
    def _():
        body([n - 1])


def _attend(i, nh, score_fn, row_shift, vt_scr, o_ref, s_scr, m_scr, acc_scr, ot_scr):
    tq = ATT_TILE
    krow = lax.broadcasted_iota(jnp.int32, (tq, tq), 0)
    qcol = lax.broadcasted_iota(jnp.int32, (tq, tq), 1)
    m_scr[...] = jnp.full(m_scr.shape, NEG_INF, F32)

    def scores(js, diagonal):
        for h in range(nh):
            folds = []
            for j in js:
                s_t = score_fn(h, j)
                if diagonal:
                    s_t = jnp.where(krow <= qcol, s_t, NEG_INF)
                s_scr[h, j] = s_t
                folds.append(_fold8(s_t, jnp.maximum))
            m_scr[h] = jnp.maximum(m_scr[h], _tree(folds, jnp.maximum))

    _for_blocks(i, functools.partial(scores, diagonal=False))
    scores([i], True)
    shift = []
    for h in range(nh):
        m = jnp.max(m_scr[h], axis=0, keepdims=True)
        shift.append(m if row_shift[h] is None else (m + row_shift[h]) - row_shift[h])
    acc_scr[...] = jnp.zeros(acc_scr.shape, F32)

    def weigh(js):
        for h in range(nh):
            pv = [_dot(vt_scr[j, h], jnp.exp2(s_scr[h, j] - shift[h]).astype(BF16)) for j in js]
            acc_scr[h] += _tree(pv, jnp.add)

    _for_blocks(i + 1, weigh)
    for h in range(nh):
        ot_scr[h * HEAD_DIM:(h + 1) * HEAD_DIM, :] = acc_scr[h, :HEAD_DIM, :] / acc_scr[h, HEAD_DIM:HEAD_DIM + 1, :]
    for g in range(nh * HEAD_DIM // LANES):
        o_ref[:, g * LANES:(g + 1) * LANES] = ot_scr[g * LANES:(g + 1) * LANES, :].T.astype(o_ref.dtype)


V_ROWS = HEAD_DIM + 16
ATT_GROUPS = 2
ATT_LANES = ATT_GROUPS * LANES
ATT_NH = ATT_LANES // HEAD_DIM


def _store_vt(vt_scr, jb, v_blk):
    vt = v_blk.T
    for h in range(v_blk.shape[1] // HEAD_DIM):
        vt_scr[jb, h, :HEAD_DIM, :] = vt[h * HEAD_DIM:(h + 1) * HEAD_DIM, :].astype(BF16)
        vt_scr[jb, h, HEAD_DIM:, :] = jnp.ones((V_ROWS - HEAD_DIM, v_blk.shape[0]), BF16)


def _attend_scratch(nb):
    tq = ATT_TILE
    return [pltpu.VMEM((nb, ATT_NH, V_ROWS, tq), BF16),
            pltpu.VMEM((ATT_NH, nb, tq, tq), F32),
            pltpu.VMEM((ATT_NH, 8, tq), F32),
            pltpu.VMEM((ATT_NH, V_ROWS, tq), F32),
            pltpu.VMEM((ATT_LANES, tq), F32)]


def _masked_heads(x, scale):
    lane = lax.broadcasted_iota(jnp.int32, (x.shape[0], LANES), 1)
    out = []
    for h in range(x.shape[1] // HEAD_DIM):
        g, hh = divmod(h, LANES // HEAD_DIM)
        xg = x[:, g * LANES:(g + 1) * LANES]
        in_head = (lane >= hh * HEAD_DIM) & (lane < (hh + 1) * HEAD_DIM)
        out.append(jnp.where(in_head, xg * scale, 0.0))
    return out


def _moba_kernel(q_ref, k_ref, v_ref, cos_ref, sin_ref, o_ref, kr_scr, km_scr, bias_scr,
                 vt_scr, s_scr, m_scr, acc_scr, ot_scr):
    tq = ATT_TILE
    nb = kr_scr.shape[1]
    i = pl.program_id(2)
    scale = HEAD_DIM ** -0.5
    hpg = LANES // HEAD_DIM

    @pl.when(i == 0)
    def _():
        for jb in range(nb):
            rows = slice(jb * tq, (jb + 1) * tq)
            for g in range(ATT_GROUPS):
                lanes = slice(g * LANES, (g + 1) * LANES)
                kr = _rope(k_ref[rows, lanes], cos_ref[rows, :], sin_ref[rows, :])
                kr_scr[g, jb] = kr.astype(BF16)
                km_scr[g, jb:jb + 1, :] = jnp.mean(kr, axis=0, keepdims=True)
            _store_vt(vt_scr, jb, v_ref[rows, :])

    qrows = pl.ds(pl.multiple_of(i * tq, tq), tq)
    cos, sin = cos_ref[qrows, :], sin_ref[qrows, :]
    qr = jnp.concatenate([_rope(q_ref[:, g * LANES:(g + 1) * LANES], cos, sin) for g in range(ATT_GROUPS)], axis=1)
    jrow = lax.broadcasted_iota(jnp.int32, (nb, tq), 0)
    valid = jrow < i
    qs = []
    for h, qm in enumerate(_masked_heads(qr, 1.0)):
        gate = lax.dot_general(km_scr[h // hpg], qm, (((1,), (1,)), ((), ())),
                               precision=lax.Precision.HIGHEST, preferred_element_type=F32)
        gate = jnp.where(valid, gate, NEG_INF)
        cnt = jnp.zeros((nb, tq), jnp.int32)
        for jp in range(nb):
            gj = gate[jp:jp + 1, :]
            beats = (gj > gate) | ((gj == gate) & (jp < jrow))
            cnt = cnt + (beats & (jp < i)).astype(jnp.int32)
        chosen = valid & (cnt < MOBA_TOPK)
        bias_scr[h] = jnp.where(chosen | (jrow == i), 0.0, NEG_INF)
        qs.append((qm * (scale * LOG2E)).astype(BF16))

    def score_fn(h, j):
        return _dot_nt(kr_scr[h // hpg, j], qs[h]) + bias_scr[h, pl.ds(j, 1), :]

    _attend(i, ATT_NH, score_fn, [None] * ATT_NH, vt_scr, o_ref, s_scr, m_scr, acc_scr, ot_scr)


def _moba(q, k, v, cos, sin_signed, bsz, s):
    tq = ATT_TILE
    nb = s // tq
    q3, k3, v3 = (a.reshape(bsz, s, MOBA_INNER) for a in (q, k, v))
    whole = pl.BlockSpec((None, s, ATT_LANES), lambda b, p, i: (b, 0, p))
    tile = pl.BlockSpec((None, tq, ATT_LANES), lambda b, p, i: (b, i, p))
    tab = pl.BlockSpec((s, LANES), lambda b, p, i: (0, 0))
    out = pl.pallas_call(
        _moba_kernel,
        grid=(bsz, MOBA_INNER // ATT_LANES, nb),
        in_specs=[tile, whole, whole, tab, tab],
        out_specs=tile,
        out_shape=jax.ShapeDtypeStruct((bsz, s, MOBA_INNER), BF16),
        scratch_shapes=[pltpu.VMEM((ATT_GROUPS, nb, tq, LANES), BF16),
                        pltpu.VMEM((ATT_GROUPS, nb, LANES), F32),
                        pltpu.VMEM((ATT_NH, nb, tq), F32)] + _attend_scratch(nb),
        compiler_params=_cparams(("parallel", "parallel", "arbitrary")),
        name="moba",
    )(q3, k3, v3, cos, sin_signed)
    return out.reshape(bsz * s, MOBA_INNER)


def _fox_gate_kernel(f_ref, b_ref, c_ref, ct_ref):
    tb = ATT_TILE
    s = f_ref.shape[0]
    row = lax.broadcasted_iota(jnp.int32, (tb, tb), 0)
    col = lax.broadcasted_iota(jnp.int32, (tb, tb), 1)
    tril = (row >= col).astype(BF16)
    carry = jnp.zeros((1, LANES), F32)
    for jb in range(s // tb):
        rows = slice(jb * tb, (jb + 1) * tb)
        log_f = -_softplus(-(f_ref[rows, :] + b_ref[...]))
        cblk = _sel_dot_rhs(tril, log_f) + carry
        c_ref[rows, :] = cblk
        ct_ref[:, rows] = cblk.T
        carry = cblk[tb - 1:tb, :]


def _fox_gate(f, bias, bsz, s):
    return pl.pallas_call(
        _fox_gate_kernel,
        grid=(bsz,),
        in_specs=[pl.BlockSpec((None, s, LANES), lambda b: (b, 0, 0)),
                  pl.BlockSpec((1, LANES), lambda b: (0, 0))],
        out_specs=[pl.BlockSpec((None, s, LANES), lambda b: (b, 0, 0)),
                   pl.BlockSpec((None, LANES, s), lambda b: (b, 0, 0))],
        out_shape=[jax.ShapeDtypeStruct((bsz, s, LANES), F32),
                   jax.ShapeDtypeStruct((bsz, LANES, s), F32)],
        compiler_params=_cparams(("parallel",)),
        name="fox_gate",
    )(f.reshape(bsz, s, LANES), _pad_lanes(bias))


def _fox_kernel(q_ref, k_ref, v_ref, c_ref, ct_ref, o_ref, k_scr, ck_scr,
                vt_scr, s_scr, m_scr, acc_scr, ot_scr):
    tq = ATT_TILE
    nb = k_scr.shape[1]
    p = pl.program_id(1)
    i = pl.program_id(2)
    scale = HEAD_DIM ** -0.5
    hpg = LANES // HEAD_DIM

    @pl.when(i == 0)
    def _():
        lane = lax.broadcasted_iota(jnp.int32, (tq, LANES), 1)
        for jb in range(nb):
            rows = slice(jb * tq, (jb + 1) * tq)
            for g in range(ATT_GROUPS):
                k_scr[g, jb] = k_ref[rows, g * LANES:(g + 1) * LANES].astype(BF16)
            _store_vt(vt_scr, jb, v_ref[rows, :])
            cblk = c_ref[rows, :] * LOG2E
            for h in range(ATT_NH):
                colv = jnp.sum(jnp.where(lane == p * ATT_NH + h, cblk, 0.0), axis=1, keepdims=True)
                ck_scr[h, jb] = jnp.broadcast_to(colv, (tq, LANES))

    qs = [qm.astype(BF16) for qm in _masked_heads(q_ref[...], scale * LOG2E)]
    cqs = [ct_ref[pl.ds(p * ATT_NH + h, 1), :] * LOG2E for h in range(ATT_NH)]

    def score_fn(h, j):
        ck = ck_scr[h, j]
        return _dot_nt(k_scr[h // hpg, j], qs[h]) - jnp.concatenate([ck] * (tq // LANES), axis=1)

    _attend(i, ATT_NH, score_fn, cqs, vt_scr, o_ref, s_scr, m_scr, acc_scr, ot_scr)


def _fox(q, k, v, c, ct, bsz, s):
    tq = ATT_TILE
    nb = s // tq
    q3, k3, v3 = (a.reshape(bsz, s, FOX_INNER) for a in (q, k, v))
    whole = pl.BlockSpec((None, s, ATT_LANES), lambda b, p, i: (b, 0, p))
    tile = pl.BlockSpec((None, tq, ATT_LANES), lambda b, p, i: (b, i, p))
    out = pl.pallas_call(
        _fox_kernel,
        grid=(bsz, FOX_INNER // ATT_LANES, nb),
        in_specs=[tile, whole, whole,
                  pl.BlockSpec((None, s, LANES), lambda b, p, i: (b, 0, 0)),
                  pl.BlockSpec((None, LANES, tq), lambda b, p, i: (b, 0, i))],
        out_specs=tile,
        out_shape=jax.ShapeDtypeStruct((bsz, s, FOX_INNER), BF16),
        scratch_shapes=[pltpu.VMEM((ATT_GROUPS, nb, tq, LANES), BF16),
                        pltpu.VMEM((ATT_NH, nb, tq, LANES), F32)] + _attend_scratch(nb),
        compiler_params=_cparams(("parallel", "parallel", "arbitrary")),
        name="fox",
    )(q3, k3, v3, c, ct)
    return out.reshape(bsz * s, FOX_INNER)


def _rope_tables(s):
    half = HEAD_DIM // 2
    inv = jnp.power(ROPE_THETA, -jnp.arange(half, dtype=F32) / half)
    ang = jnp.arange(s, dtype=F32)[:, None] * inv[None, :]
    cos, sin = jnp.cos(ang), jnp.sin(ang)
    reps = LANES // HEAD_DIM
    return (jnp.tile(jnp.concatenate([cos, cos], axis=1), (1, reps)),
            jnp.tile(jnp.concatenate([-sin, sin], axis=1), (1, reps)))


def _even_layer(x, bsz, s, g, w_in, conv_w, conv_b, dt_bias, a_log, d_skip, gate_norm, w_out, rope_tabs):
    d = x.shape[1]
    o1 = SSD_INNER
    o2 = o1 + SSD_XBC
    o3 = o2 + SSD_HEADS
    w = jnp.concatenate([w_in[:, :o2], w_in[:, o3:], w_in[:, o2:o3],
                         jnp.zeros((d, LANES - SSD_HEADS), w_in.dtype)], axis=1).astype(BF16)
    z, xbc, q, k, v, dt = _norm_matmul(
        x, g, w, (SSD_INNER, SSD_XBC, MOBA_INNER, MOBA_INNER, MOBA_INNER, LANES))
    y_ssd = _ssd(z, xbc, dt, conv_w, conv_b, dt_bias, a_log, d_skip, gate_norm, bsz, s)
    y_att = _moba(q, k, v, *rope_tabs, bsz, s)
    w_o = w_out.astype(BF16)
    return _proj_res([y_ssd, y_att], [w_o[:SSD_INNER], w_o[SSD_INNER:]], x)


def _odd_layer(x, bsz, s, g, w_in, fgate_bias, w_out):
    d = x.shape[1]
    w = jnp.concatenate([w_in, jnp.zeros((d, LANES - FOX_HEADS), w_in.dtype)], axis=1).astype(BF16)
    q, k, v, f = _norm_matmul(x, g, w, (FOX_INNER, FOX_INNER, FOX_INNER, LANES))
    c, ct = _fox_gate(f, fgate_bias, bsz, s)
    y = _fox(q, k, v, c, ct, bsz, s)
    return _proj_res([y], [w_out.astype(BF16)], x)


def kernel(x, norm_mix_even, w_in_even, conv_w, conv_b, dt_bias, a_log, d_skip, ssd_gate_norm, w_out_even,
           norm_mix_odd, w_in_odd, fgate_bias, w_out_odd, norm_mlp, w_up, w_down, final_norm):
    bsz, s, d = x.shape
    depth = norm_mlp.shape[0]
    assert s % ATT_TILE == 0 and s % SSD_CHUNK == 0
    h = x.reshape(bsz * s, d)
    rope_tabs = _rope_tables(s)
    for layer in range(depth):
        i = layer // 2
        if layer % 2 == 0:
            h = _even_layer(h, bsz, s, norm_mix_even[i], w_in_even[i], conv_w[i], conv_b[i], dt_bias[i],
                            a_log[i], d_skip[i], ssd_gate_norm[i], w_out_even[i], rope_tabs)
        else:
            h = _odd_layer(h, bsz, s, norm_mix_odd[i], w_in_odd[i], fgate_bias[i], w_out_odd[i])
        h = _mlp(h, norm_mlp[layer], w_up[layer].astype(BF16), w_down[layer].astype(BF16), final_norm,
                 final_norm=(layer == depth - 1))
    return h.reshape(bsz, s, d)
```

```python
import functools
import math

import jax
import jax.numpy as jnp
from jax import lax
from jax.experimental import pallas as pl
from jax.experimental.pallas import tpu as pltpu

NORM_EPS = 1e-5
ROPE_THETA = 10000.0
LANES = 128
HEAD_DIM = 64
SSD_HEADS = 16
SSD_GROUPS = 4
SSD_STATE = 128
SSD_CONV = 4
SSD_CHUNK = 128
SSD_INNER = SSD_HEADS * HEAD_DIM
SSD_GN = SSD_GROUPS * SSD_STATE
SSD_XBC = SSD_INNER + 2 * SSD_GN
MOBA_HEADS = 8
MOBA_INNER = MOBA_HEADS * HEAD_DIM
MOBA_BLOCK = 256
MOBA_TOPK = 3
FOX_HEADS = 16
FOX_INNER = FOX_HEADS * HEAD_DIM
ATT_TILE = 256
VMEM_LIMIT = 56 * 1024 * 1024

F32 = jnp.float32
BF16 = jnp.bfloat16
NEG_INF = float("-inf")
LOG2E = math.log2(math.e)


def _cparams(sem):
    return pltpu.CompilerParams(dimension_semantics=sem, vmem_limit_bytes=VMEM_LIMIT)


def _rms(x, g):
    return x * lax.rsqrt(jnp.mean(x * x, axis=-1, keepdims=True) + NORM_EPS) * g


def _sigmoid(x):
    return 1.0 / (1.0 + jnp.exp(-x))


def _softplus(x):
    return jnp.maximum(x, 0.0) + jnp.log1p(jnp.exp(-jnp.abs(x)))


def _split3(x):
    hi = x.astype(BF16)
    r = x - hi.astype(F32)
    mid = r.astype(BF16)
    lo = (r - mid.astype(F32)).astype(BF16)
    return hi, mid, lo


def _dot(a, b):
    return jnp.dot(a, b, preferred_element_type=F32)


def _dot_nt(a, b):
    return lax.dot_general(a, b, (((1,), (1,)), ((), ())), preferred_element_type=F32)


def _dot_tn(a, b):
    return lax.dot_general(a, b, (((0,), (0,)), ((), ())), preferred_element_type=F32)


def _sel_dot_rhs(m01, x):
    hi, mid, lo = _split3(x)
    return _dot(m01, hi) + _dot(m01, mid) + _dot(m01, lo)


def _sel_dot_lhs(x, m01):
    hi, mid, lo = _split3(x)
    return _dot(hi, m01) + _dot(mid, m01) + _dot(lo, m01)


def _norm_matmul_kernel(x_ref, g_ref, w_ref, *out_refs, splits):
    xn = _rms(x_ref[...], g_ref[...]).astype(BF16)
    for o_ref, (a, b) in zip(out_refs, splits):
        o_ref[...] = _dot(xn, w_ref[:, a:b]).astype(o_ref.dtype)


def _norm_matmul(x, g, w, widths, tm=256):
    t, d = x.shape
    n = w.shape[1]
    splits, a = [], 0
    for wd in widths:
        splits.append((a, a + wd))
        a += wd
    assert a == n and t % tm == 0
    return pl.pallas_call(
        functools.partial(_norm_matmul_kernel, splits=tuple(splits)),
        grid=(t // tm,),
        in_specs=[pl.BlockSpec((tm, d), lambda i: (i, 0)),
                  pl.BlockSpec((1, d), lambda i: (0, 0)),
                  pl.BlockSpec((d, n), lambda i: (0, 0))],
        out_specs=[pl.BlockSpec((tm, wd), lambda i: (i, 0)) for wd in widths],
        out_shape=[jax.ShapeDtypeStruct((t, wd), F32) for wd in widths],
        compiler_params=_cparams(("parallel",)),
        name="norm_matmul",
    )(x, g.reshape(1, d), w)


def _proj_res_kernel(*refs, n):
    a_refs, w_refs = refs[:n], refs[n:2 * n]
    res_ref, o_ref = refs[2 * n], refs[2 * n + 1]
    acc = res_ref[...]
    for a_ref, w_ref in zip(a_refs, w_refs):
        acc = acc + _dot(a_ref[...], w_ref[...])
    o_ref[...] = acc


def _proj_res(acts, ws, res, tm=512):
    t, d = res.shape
    n = len(acts)
    in_specs = ([pl.BlockSpec((tm, a.shape[1]), lambda i: (i, 0)) for a in acts]
                + [pl.BlockSpec(w.shape, lambda i: (0, 0)) for w in ws]
                + [pl.BlockSpec((tm, d), lambda i: (i, 0))])
    return pl.pallas_call(
        functools.partial(_proj_res_kernel, n=n),
        grid=(t // tm,),
        in_specs=in_specs,
        out_specs=pl.BlockSpec((tm, d), lambda i: (i, 0)),
        out_shape=jax.ShapeDtypeStruct((t, d), F32),
        compiler_params=_cparams(("parallel",)),
        name="proj_res",
    )(*acts, *ws, res)


def _mlp_kernel(x_ref, g_ref, wu_ref, wd_ref, gf_ref, o_ref, xn_scr, acc_scr, *, final_norm):
    j = pl.program_id(1)

    @pl.when(j == 0)
    def _():
        x = x_ref[...]
        xn_scr[...] = _rms(x, g_ref[...]).astype(BF16)
        acc_scr[...] = x

    h = _dot(xn_scr[...], wu_ref[...])
    h = jnp.square(jnp.maximum(h, 0.0)).astype(BF16)
    acc_scr[...] += _dot(h, wd_ref[...])

    @pl.when(j == pl.num_programs(1) - 1)
    def _():
        y = acc_scr[...]
        if final_norm:
            y = _rms(y, gf_ref[...])
        o_ref[...] = y


def _mlp(x, g, w_up, w_down, g_final, final_norm, tm=512, tf=1024):
    t, d = x.shape
    f = w_up.shape[1]
    return pl.pallas_call(
        functools.partial(_mlp_kernel, final_norm=final_norm),
        grid=(t // tm, f // tf),
        in_specs=[pl.BlockSpec((tm, d), lambda i, j: (i, 0)),
                  pl.BlockSpec((1, d), lambda i, j: (0, 0)),
                  pl.BlockSpec((d, tf), lambda i, j: (0, j)),
                  pl.BlockSpec((tf, d), lambda i, j: (j, 0)),
                  pl.BlockSpec((1, d), lambda i, j: (0, 0))],
        out_specs=pl.BlockSpec((tm, d), lambda i, j: (i, 0)),
        out_shape=jax.ShapeDtypeStruct((t, d), F32),
        scratch_shapes=[pltpu.VMEM((tm, d), BF16), pltpu.VMEM((tm, d), F32)],
        compiler_params=_cparams(("parallel", "arbitrary")),
        name="mlp",
    )(x, g.reshape(1, d), w_up, w_down, g_final.reshape(1, d))


SSD_STEP = 2 * SSD_CHUNK
CONV_STRIDE = SSD_STEP // 8 + 1
CONV_ROWS = 8 * CONV_STRIDE
SSD_SLABS = SSD_XBC // LANES


def _ssd_kernel(z_ref, xbc_ref, dt_ref, cw_ref, cb_ref, dtb_ref, alog_ref, dskip_ref, gn_ref,
                e_ref, o_ref, ubuf, xa_scr, hst, y_scr):
    L = SSD_CHUNK
    TL = SSD_STEP
    c = pl.program_id(1)

    @pl.when(c == 0)
    def _():
        hst[...] = jnp.zeros_like(hst)
        ubuf[...] = jnp.zeros_like(ubuf)

    for sl in range(SSD_SLABS):
        ubuf[sl, 8:8 + TL, :] = xbc_ref[:, sl * LANES:(sl + 1) * LANES]

    def conv_slab(sl, carry):
        w = cw_ref[sl]
        b = cb_ref[sl]
        for r in range(CONV_STRIDE):
            acc = b
            for k in range(SSD_CONV):
                acc = acc + w[k:k + 1, :] * ubuf[sl, pl.ds(5 + k + r, 8, stride=CONV_STRIDE), :]
            xa_scr[sl, pl.ds(r, 8, stride=CONV_STRIDE), :] = acc * _sigmoid(acc)
        ubuf[sl, 0:8, :] = ubuf[sl, TL:TL + 8, :]
        return carry

    lax.fori_loop(0, SSD_SLABS, conv_slab, 0)
    nxs = SSD_INNER // LANES
    ngs = SSD_GN // LANES
    row = lax.broadcasted_iota(jnp.int32, (L, L), 0)
    col = lax.broadcasted_iota(jnp.int32, (L, L), 1)
    causal = row >= col
    tril = causal.astype(BF16)
    e01 = e_ref[...]
    lane = lax.broadcasted_iota(jnp.int32, (L, LANES), 1)
    neg_a = -jnp.exp(alog_ref[...])
    gw = SSD_INNER // SSD_GROUPS
    hpl = LANES // HEAD_DIM

    for ci in range(TL // L):
        rows = slice(ci * L, (ci + 1) * L)
        xs = jnp.concatenate([xa_scr[sl, rows, :] for sl in range(nxs)], axis=1)
        dt = _softplus(dt_ref[rows, :] + dtb_ref[...])
        acum = _sel_dot_rhs(tril, dt * neg_a)
        alast = acum[L - 1:L, :]
        dtx = _dot(dt.astype(BF16), e01)
        eacx = _dot(jnp.exp(acum).astype(BF16), e01)
        dtex = _dot(jnp.exp(alast - acum).astype(BF16), e01)
        cdx = _sel_dot_lhs(jnp.exp(acum[L - 8:L, :]), e01)[7:8, :]
        xc = xs * dtx
        xcd = (xc * dtex).astype(BF16)
        act = acum.T

        for g in range(SSD_GROUPS):
            b_f32 = xa_scr[nxs + g, rows, :]
            bg = b_f32.astype(BF16)
            bgt = b_f32.T.astype(BF16)
            cg = xa_scr[nxs + ngs + g, rows, :].astype(BF16)
            cb = _dot_nt(cg, bg)
            hprev = hst[g]
            y_off = _dot(cg, hprev.astype(BF16)) * eacx[:, g * gw:(g + 1) * gw]
            hst[g] = hprev * cdx[:, g * gw:(g + 1) * gw] + _dot(bgt, xcd[:, g * gw:(g + 1) * gw])
            for pr in range(gw // LANES):
                lo = g * gw + pr * LANES
                xcp = xc[:, lo:lo + LANES]
                ms, xm = [], []
                for hh in range(hpl):
                    h = lo // HEAD_DIM + hh
                    dec = jnp.exp(jnp.where(causal, acum[:, h:h + 1] - act[h:h + 1, :], NEG_INF))
                    ms.append((cb * dec).astype(BF16))
                    in_head = (lane >= hh * HEAD_DIM) & (lane < (hh + 1) * HEAD_DIM)
                    xm.append(jnp.where(in_head, xcp, 0.0).astype(BF16))
                y_diag = _dot(jnp.concatenate(ms, axis=1), jnp.concatenate(xm, axis=0))
                y_scr[rows, lo:lo + LANES] = y_diag + y_off[:, pr * LANES:(pr + 1) * LANES]

        y = y_scr[rows, :] + xs * dskip_ref[...]
        z = z_ref[rows, :]
        y = y * (z * _sigmoid(z))
        for g in range(SSD_GROUPS):
            yg = y[:, g * gw:(g + 1) * gw]
            ms = jnp.mean(yg * yg, axis=-1, keepdims=True)
            o_ref[rows, g * gw:(g + 1) * gw] = (
                yg * lax.rsqrt(ms + NORM_EPS) * gn_ref[:, g * gw:(g + 1) * gw]).astype(o_ref.dtype)


def _pad_lanes(v):
    return jnp.pad(v.astype(F32), (0, LANES - v.shape[0])).reshape(1, LANES)


def _ssd(z, xbc, dt, conv_w, conv_b, dt_bias, a_log, d_skip, gate_norm, bsz, s):
    L = SSD_STEP
    nc = s // L
    e01 = (jnp.arange(LANES)[:, None] == (jnp.arange(SSD_INNER) // HEAD_DIM)[None, :]).astype(BF16)
    dsx = jnp.repeat(d_skip.astype(F32), HEAD_DIM).reshape(1, SSD_INNER)
    cw = conv_w.astype(F32).reshape(SSD_CONV, SSD_SLABS, LANES).transpose(1, 0, 2)
    cw = jnp.pad(cw, ((0, 0), (0, 8 - SSD_CONV), (0, 0)))
    cb = conv_b.astype(F32).reshape(SSD_SLABS, 1, LANES)
    tok = lambda b, c: (b * nc + c, 0)
    const = lambda b, c: (0, 0)
    const3 = lambda b, c: (0, 0, 0)
    return pl.pallas_call(
        _ssd_kernel,
        grid=(bsz, nc),
        in_specs=[pl.BlockSpec((L, SSD_INNER), tok),
                  pl.BlockSpec((L, SSD_XBC), tok),
                  pl.BlockSpec((L, LANES), tok),
                  pl.BlockSpec((SSD_SLABS, 8, LANES), const3),
                  pl.BlockSpec((SSD_SLABS, 1, LANES), const3),
                  pl.BlockSpec((1, LANES), const),
                  pl.BlockSpec((1, LANES), const),
                  pl.BlockSpec((1, SSD_INNER), const),
                  pl.BlockSpec((1, SSD_INNER), const),
                  pl.BlockSpec((LANES, SSD_INNER), const)],
        out_specs=pl.BlockSpec((L, SSD_INNER), tok),
        out_shape=jax.ShapeDtypeStruct((bsz * s, SSD_INNER), BF16),
        scratch_shapes=[pltpu.VMEM((SSD_SLABS, 8 + CONV_ROWS + 8, LANES), F32),
                        pltpu.VMEM((SSD_SLABS, CONV_ROWS, LANES), F32),
                        pltpu.VMEM((SSD_GROUPS, SSD_STATE, SSD_INNER // SSD_GROUPS), F32),
                        pltpu.VMEM((L, SSD_INNER), F32)],
        compiler_params=_cparams(("parallel", "arbitrary")),
        name="ssd",
    )(z, xbc, dt, cw, cb, _pad_lanes(dt_bias), _pad_lanes(a_log), dsx,
      gate_norm.astype(F32).reshape(1, SSD_INNER), e01)


def _swap_halves(x):
    lane = lax.broadcasted_iota(jnp.int32, x.shape, 1)
    first = (lane % HEAD_DIM) < HEAD_DIM // 2
    return jnp.where(first, pltpu.roll(x, LANES - HEAD_DIM // 2, 1), pltpu.roll(x, HEAD_DIM // 2, 1))


def _rope(x, cos, sin_signed):
    return x * cos + _swap_halves(x) * sin_signed


def _fold8(x, op):
    parts = [x[r * 8:(r + 1) * 8, :] for r in range(x.shape[0] // 8)]
    return _tree(parts, op)


def _tree(parts, op):
    while len(parts) > 1:
        parts = [op(parts[a], parts[a + 1]) if a + 1 < len(parts) else parts[a]
                 for a in range(0, len(parts), 2)]
    return parts[0]


def _for_blocks(n, body):
    def quad(g, carry):
        body([4 * g + u for u in range(4)])
        return carry

    lax.fori_loop(0, n // 4, quad, 0)
    base = (n // 4) * 4

    @pl.when((n & 2) != 0)
    def _():
        body([base, base + 1])

    @pl.when((n & 1) != 0)
    def _():
        body([n - 1])


def _attend(i, nh, score_fn, row_shift, vt_scr, o_ref, s_scr, m_scr, acc_scr, ot_scr):
    tq = ATT_TILE
    krow = lax.broadcasted_iota(jnp.int32, (tq, tq), 0)
    qcol = lax.broadcasted_iota(jnp.int32, (tq, tq), 1)
    m_scr[...] = jnp.full(m_scr.shape, NEG_INF, F32)

    def scores(js, diagonal):
        for h in range(nh):
            folds = []
            for j in js:
                s_t = score_fn(h, j)
                if diagonal:
                    s_t = jnp.where(krow <= qcol, s_t, NEG_INF)
                s_scr[h, j] = s_t
                folds.append(_fold8(s_t, jnp.maximum))
            m_scr[h] = jnp.maximum(m_scr[h], _tree(folds, jnp.maximum))

    _for_blocks(i, functools.partial(scores, diagonal=False))
    scores([i], True)
    shift = []
    for h in range(nh):
        m = jnp.max(m_scr[h], axis=0, keepdims=True)
        shift.append(m if row_shift[h] is None else (m + row_shift[h]) - row_shift[h])
    acc_scr[...] = jnp.zeros(acc_scr.shape, F32)

    def weigh(js):
        for h in range(nh):
            pv = [_dot(vt_scr[j, h], jnp.exp2(s_scr[h, j] - shift[h]).astype(BF16)) for j in js]
            acc_scr[h] += _tree(pv, jnp.add)

    _for_blocks(i + 1, weigh)
    for h in range(nh):
        ot_scr[h * HEAD_DIM:(h + 1) * HEAD_DIM, :] = acc_scr[h, :HEAD_DIM, :] / acc_scr[h, HEAD_DIM:HEAD_DIM + 1, :]
    for g in range(nh * HEAD_DIM // LANES):
        o_ref[:, g * LANES:(g + 1) * LANES] = ot_scr[g * LANES:(g + 1) * LANES, :].T.astype(o_ref.dtype)


V_ROWS = HEAD_DIM + 16
ATT_GROUPS = 2
ATT_LANES = ATT_GROUPS * LANES
ATT_NH = ATT_LANES // HEAD_DIM


def _store_vt(vt_scr, jb, v_blk):
    vt = v_blk.T
    for h in range(v_blk.shape[1] // HEAD_DIM):
        vt_scr[jb, h, :HEAD_DIM, :] = vt[h * HEAD_DIM:(h + 1) * HEAD_DIM, :].astype(BF16)
        vt_scr[jb, h, HEAD_DIM:, :] = jnp.ones((V_ROWS - HEAD_DIM, v_blk.shape[0]), BF16)


def _attend_scratch(nb):
    tq = ATT_TILE
    return [pltpu.VMEM((nb, ATT_NH, V_ROWS, tq), BF16),
            pltpu.VMEM((ATT_NH, nb, tq, tq), F32),
            pltpu.VMEM((ATT_NH, 8, tq), F32),
            pltpu.VMEM((ATT_NH, V_ROWS, tq), F32),
            pltpu.VMEM((ATT_LANES, tq), F32)]


def _masked_heads(x, scale):
    lane = lax.broadcasted_iota(jnp.int32, (x.shape[0], LANES), 1)
    out = []
    for h in range(x.shape[1] // HEAD_DIM):
        g, hh = divmod(h, LANES // HEAD_DIM)
        xg = x[:, g * LANES:(g + 1) * LANES]
        in_head = (lane >= hh * HEAD_DIM) & (lane < (hh + 1) * HEAD_DIM)
        out.append(jnp.where(in_head, xg * scale, 0.0))
    return out


def _moba_kernel(q_ref, k_ref, v_ref, cos_ref, sin_ref, o_ref, kr_scr, km_scr, bias_scr,
                 vt_scr, s_scr, m_scr, acc_scr, ot_scr):
    tq = ATT_TILE
    nb = kr_scr.shape[1]
    i = pl.program_id(2)
    scale = HEAD_DIM ** -0.5
    hpg = LANES // HEAD_DIM

    @pl.when(i == 0)
    def _():
        for jb in range(nb):
            rows = slice(jb * tq, (jb + 1) * tq)
            for g in range(ATT_GROUPS):
                lanes = slice(g * LANES, (g + 1) * LANES)
                kr = _rope(k_ref[rows, lanes], cos_ref[rows, :], sin_ref[rows, :])
                kr_scr[g, jb] = kr.astype(BF16)
                km_scr[g, jb:jb + 1, :] = jnp.mean(kr, axis=0, keepdims=True)
            _store_vt(vt_scr, jb, v_ref[rows, :])

    qrows = pl.ds(pl.multiple_of(i * tq, tq), tq)
    cos, sin = cos_ref[qrows, :], sin_ref[qrows, :]
    qr = jnp.concatenate([_rope(q_ref[:, g * LANES:(g + 1) * LANES], cos, sin) for g in range(ATT_GROUPS)], axis=1)
    jrow = lax.broadcasted_iota(jnp.int32, (nb, tq), 0)
    valid = jrow < i
    qs = []
    for h, qm in enumerate(_masked_heads(qr, 1.0)):
        gate = lax.dot_general(km_scr[h // hpg], qm, (((1,), (1,)), ((), ())),
                               precision=lax.Precision.HIGHEST, preferred_element_type=F32)
        gate = jnp.where(valid, gate, NEG_INF)
        cnt = jnp.zeros((nb, tq), jnp.int32)
        for jp in range(nb):
            gj = gate[jp:jp + 1, :]
            beats = (gj > gate) | ((gj == gate) & (jp < jrow))
            cnt = cnt + (beats & (jp < i)).astype(jnp.int32)
        chosen = valid & (cnt < MOBA_TOPK)
        bias_scr[h] = jnp.where(chosen | (jrow == i), 0.0, NEG_INF)
        qs.append((qm * (scale * LOG2E)).astype(BF16))

    def score_fn(h, j):
        return _dot_nt(kr_scr[h // hpg, j], qs[h]) + bias_scr[h, pl.ds(j, 1), :]

    _attend(i, ATT_NH, score_fn, [None] * ATT_NH, vt_scr, o_ref, s_scr, m_scr, acc_scr, ot_scr)


def _moba(q, k, v, cos, sin_signed, bsz, s):
    tq = ATT_TILE
    nb = s // tq
    q3, k3, v3 = (a.reshape(bsz, s, MOBA_INNER) for a in (q, k, v))
    whole = pl.BlockSpec((None, s, ATT_LANES), lambda b, p, i: (b, 0, p))
    tile = pl.BlockSpec((None, tq, ATT_LANES), lambda b, p, i: (b, i, p))
    tab = pl.BlockSpec((s, LANES), lambda b, p, i: (0, 0))
    out = pl.pallas_call(
        _moba_kernel,
        grid=(bsz, MOBA_INNER // ATT_LANES, nb),
        in_specs=[tile, whole, whole, tab, tab],
        out_specs=tile,
        out_shape=jax.ShapeDtypeStruct((bsz, s, MOBA_INNER), BF16),
        scratch_shapes=[pltpu.VMEM((ATT_GROUPS, nb, tq, LANES), BF16),
                        pltpu.VMEM((ATT_GROUPS, nb, LANES), F32),
                        pltpu.VMEM((ATT_NH, nb, tq), F32)] + _attend_scratch(nb),
        compiler_params=_cparams(("parallel", "parallel", "arbitrary")),
        name="moba",
    )(q3, k3, v3, cos, sin_signed)
    return out.reshape(bsz * s, MOBA_INNER)


def _fox_gate_kernel(f_ref, b_ref, c_ref, ct_ref):
    tb = ATT_TILE
    s = f_ref.shape[0]
    row = lax.broadcasted_iota(jnp.int32, (tb, tb), 0)
    col = lax.broadcasted_iota(jnp.int32, (tb, tb), 1)
    tril = (row >= col).astype(BF16)
    carry = jnp.zeros((1, LANES), F32)
    for jb in range(s // tb):
        rows = slice(jb * tb, (jb + 1) * tb)
        log_f = -_softplus(-(f_ref[rows, :] + b_ref[...]))
        cblk = _sel_dot_rhs(tril, log_f) + carry
        c_ref[rows, :] = cblk
        ct_ref[:, rows] = cblk.T
        carry = cblk[tb - 1:tb, :]


def _fox_gate(f, bias, bsz, s):
    return pl.pallas_call(
        _fox_gate_kernel,
        grid=(bsz,),
        in_specs=[pl.BlockSpec((None, s, LANES), lambda b: (b, 0, 0)),
                  pl.BlockSpec((1, LANES), lambda b: (0, 0))],
        out_specs=[pl.BlockSpec((None, s, LANES), lambda b: (b, 0, 0)),
                   pl.BlockSpec((None, LANES, s), lambda b: (b, 0, 0))],
        out_shape=[jax.ShapeDtypeStruct((bsz, s, LANES), F32),
                   jax.ShapeDtypeStruct((bsz, LANES, s), F32)],
        compiler_params=_cparams(("parallel",)),
        name="fox_gate",
    )(f.reshape(bsz, s, LANES), _pad_lanes(bias))


def _fox_kernel(q_ref, k_ref, v_ref, c_ref, ct_ref, o_ref, k_scr, ck_scr,
                vt_scr, s_scr, m_scr, acc_scr, ot_scr):
    tq = ATT_TILE
    nb = k_scr.shape[1]
    p = pl.program_id(1)
    i = pl.program_id(2)
    scale = HEAD_DIM ** -0.5
    hpg = LANES // HEAD_DIM

    @pl.when(i == 0)
    def _():
        lane = lax.broadcasted_iota(jnp.int32, (tq, LANES), 1)
        for jb in range(nb):
            rows = slice(jb * tq, (jb + 1) * tq)
            for g in range(ATT_GROUPS):
                k_scr[g, jb] = k_ref[rows, g * LANES:(g + 1) * LANES].astype(BF16)
            _store_vt(vt_scr, jb, v_ref[rows, :])
            cblk = c_ref[rows, :] * LOG2E
            for h in range(ATT_NH):
                colv = jnp.sum(jnp.where(lane == p * ATT_NH + h, cblk, 0.0), axis=1, keepdims=True)
                ck_scr[h, jb] = jnp.broadcast_to(colv, (tq, LANES))

    qs = [qm.astype(BF16) for qm in _masked_heads(q_ref[...], scale * LOG2E)]
    cqs = [ct_ref[pl.ds(p * ATT_NH + h, 1), :] * LOG2E for h in range(ATT_NH)]

    def score_fn(h, j):
        ck = ck_scr[h, j]
        return _dot_nt(k_scr[h // hpg, j], qs[h]) - jnp.concatenate([ck] * (tq // LANES), axis=1)

    _attend(i, ATT_NH, score_fn, cqs, vt_scr, o_ref, s_scr, m_scr, acc_scr, ot_scr)


def _fox(q, k, v, c, ct, bsz, s):
    tq = ATT_TILE
    nb = s // tq
    q3, k3, v3 = (a.reshape(bsz, s, FOX_INNER) for a in (q, k, v))
    whole = pl.BlockSpec((None, s, ATT_LANES), lambda b, p, i: (b, 0, p))
    tile = pl.BlockSpec((None, tq, ATT_LANES), lambda b, p, i: (b, i, p))
    out = pl.pallas_call(
        _fox_kernel,
        grid=(bsz, FOX_INNER // ATT_LANES, nb),
        in_specs=[tile, whole, whole,
                  pl.BlockSpec((None, s, LANES), lambda b, p, i: (b, 0, 0)),
                  pl.BlockSpec((None, LANES, tq), lambda b, p, i: (b, 0, i))],
        out_specs=tile,
        out_shape=jax.ShapeDtypeStruct((bsz, s, FOX_INNER), BF16),
        scratch_shapes=[pltpu.VMEM((ATT_GROUPS, nb, tq, LANES), BF16),
                        pltpu.VMEM((ATT_NH, nb, tq, LANES), F32)] + _attend_scratch(nb),
        compiler_params=_cparams(("parallel", "parallel", "arbitrary")),
        name="fox",
    )(q3, k3, v3, c, ct)
    return out.reshape(bsz * s, FOX_INNER)


def _rope_tables(s):
    half = HEAD_DIM // 2
    inv = jnp.power(ROPE_THETA, -jnp.arange(half, dtype=F32) / half)
    ang = jnp.arange(s, dtype=F32)[:, None] * inv[None, :]
    cos, sin = jnp.cos(ang), jnp.sin(ang)
    reps = LANES // HEAD_DIM
    return (jnp.tile(jnp.concatenate([cos, cos], axis=1), (1, reps)),
            jnp.tile(jnp.concatenate([-sin, sin], axis=1), (1, reps)))


def _even_layer(x, bsz, s, g, w_in, conv_w, conv_b, dt_bias, a_log, d_skip, gate_norm, w_out, rope_tabs):
    d = x.shape[1]
    o1 = SSD_INNER
    o2 = o1 + SSD_XBC
    o3 = o2 + SSD_HEADS
    w = jnp.concatenate([w_in[:, :o2], w_in[:, o3:], w_in[:, o2:o3],
                         jnp.zeros((d, LANES - SSD_HEADS), w_in.dtype)], axis=1).astype(BF16)
    z, xbc, q, k, v, dt = _norm_matmul(
        x, g, w, (SSD_INNER, SSD_XBC, MOBA_INNER, MOBA_INNER, MOBA_INNER, LANES))
    y_ssd = _ssd(z, xbc, dt, conv_w, conv_b, dt_bias, a_log, d_skip, gate_norm, bsz, s)
    y_att = _moba(q, k, v, *rope_tabs, bsz, s)
    w_o = w_out.astype(BF16)
    return _proj_res([y_ssd, y_att], [w_o[:SSD_INNER], w_o[SSD_INNER:]], x)


def _odd_layer(x, bsz, s, g, w_in, fgate_bias, w_out):
    d = x.shape[1]
    w = jnp.concatenate([w_in, jnp.zeros((d, LANES - FOX_HEADS), w_in.dtype)], axis=1).astype(BF16)
    q, k, v, f = _norm_matmul(x, g, w, (FOX_INNER, FOX_INNER, FOX_INNER, LANES))
    c, ct = _fox_gate(f, fgate_bias, bsz, s)
    y = _fox(q, k, v, c, ct, bsz, s)
    return _proj_res([y], [w_out.astype(BF16)], x)


def kernel(x, norm_mix_even, w_in_even, conv_w, conv_b, dt_bias, a_log, d_skip, ssd_gate_norm, w_out_even,
           norm_mix_odd, w_in_odd, fgate_bias, w_out_odd, norm_mlp, w_up, w_down, final_norm):
    bsz, s, d = x.shape
    depth = norm_mlp.shape[0]
    assert s % ATT_TILE == 0 and s % SSD_STEP == 0
    h = x.reshape(bsz * s, d)
    rope_tabs = _rope_tables(s)
    for layer in range(depth):
        i = layer // 2
        if layer % 2 == 0:
            h = _even_layer(h, bsz, s, norm_mix_even[i], w_in_even[i], conv_w[i], conv_b[i], dt_bias[i],
                            a_log[i], d_skip[i], ssd_gate_norm[i], w_out_even[i], rope_tabs)
        else:
            h = _odd_layer(h, bsz, s, norm_mix_odd[i], w_in_odd[i], fgate_bias[i], w_out_odd[i])
        h = _mlp(h, norm_mlp[layer], w_up[layer].astype(BF16), w_down[layer].astype(BF16), final_norm,
                 final_norm=(layer == depth - 1))
    return h.reshape(bsz, s, d)
```

```python
import functools
import math

import jax
import jax.numpy as jnp
from jax import lax
from jax.experimental import pallas as pl
from jax.experimental.pallas import tpu as pltpu

NORM_EPS = 1e-5
ROPE_THETA = 10000.0
LANES = 128
HEAD_DIM = 64
SSD_HEADS = 16
SSD_GROUPS = 4
SSD_STATE = 128
SSD_CONV = 4
SSD_CHUNK = 128
SSD_INNER = SSD_HEADS * HEAD_DIM
SSD_GN = SSD_GROUPS * SSD_STATE
SSD_XBC = SSD_INNER + 2 * SSD_GN
MOBA_HEADS = 8
MOBA_INNER = MOBA_HEADS * HEAD_DIM
MOBA_BLOCK = 256
MOBA_TOPK = 3
FOX_HEADS = 16
FOX_INNER = FOX_HEADS * HEAD_DIM
ATT_TILE = 256
VMEM_LIMIT = 56 * 1024 * 1024

F32 = jnp.float32
BF16 = jnp.bfloat16
NEG_INF = float("-inf")
LOG2E = math.log2(math.e)


def _cparams(sem):
    return pltpu.CompilerParams(dimension_semantics=sem, vmem_limit_bytes=VMEM_LIMIT)


def _rms(x, g):
    return x * lax.rsqrt(jnp.mean(x * x, axis=-1, keepdims=True) + NORM_EPS) * g


def _sigmoid(x):
    return 1.0 / (1.0 + jnp.exp(-x))


def _softplus(x):
    return jnp.maximum(x, 0.0) + jnp.log1p(jnp.exp(-jnp.abs(x)))


def _split3(x):
    hi = x.astype(BF16)
    r = x - hi.astype(F32)
    mid = r.astype(BF16)
    lo = (r - mid.astype(F32)).astype(BF16)
    return hi, mid, lo


def _dot(a, b):
    return jnp.dot(a, b, preferred_element_type=F32)


def _dot_nt(a, b):
    return lax.dot_general(a, b, (((1,), (1,)), ((), ())), preferred_element_type=F32)


def _dot_tn(a, b):
    return lax.dot_general(a, b, (((0,), (0,)), ((), ())), preferred_element_type=F32)


def _sel_dot_rhs(m01, x):
    hi, mid, lo = _split3(x)
    return _dot(m01, hi) + _dot(m01, mid) + _dot(m01, lo)


def _sel_dot_lhs(x, m01):
    hi, mid, lo = _split3(x)
    return _dot(hi, m01) + _dot(mid, m01) + _dot(lo, m01)


def _norm_matmul_kernel(x_ref, g_ref, w_ref, *out_refs, splits):
    xn = _rms(x_ref[...], g_ref[...]).astype(BF16)
    for o_ref, (a, b) in zip(out_refs, splits):
        o_ref[...] = _dot(xn, w_ref[:, a:b]).astype(o_ref.dtype)


def _norm_matmul(x, g, w, widths, tm=256):
    t, d = x.shape
    n = w.shape[1]
    splits, a = [], 0
    for wd in widths:
        splits.append((a, a + wd))
        a += wd
    assert a == n and t % tm == 0
    return pl.pallas_call(
        functools.partial(_norm_matmul_kernel, splits=tuple(splits)),
        grid=(t // tm,),
        in_specs=[pl.BlockSpec((tm, d), lambda i: (i, 0)),
                  pl.BlockSpec((1, d), lambda i: (0, 0)),
                  pl.BlockSpec((d, n), lambda i: (0, 0))],
        out_specs=[pl.BlockSpec((tm, wd), lambda i: (i, 0)) for wd in widths],
        out_shape=[jax.ShapeDtypeStruct((t, wd), F32) for wd in widths],
        compiler_params=_cparams(("parallel",)),
        name="norm_matmul",
    )(x, g.reshape(1, d), w)


def _proj_res_kernel(*refs, n):
    a_refs, w_refs = refs[:n], refs[n:2 * n]
    res_ref, o_ref = refs[2 * n], refs[2 * n + 1]
    acc = res_ref[...]
    for a_ref, w_ref in zip(a_refs, w_refs):
        acc = acc + _dot(a_ref[...], w_ref[...])
    o_ref[...] = acc


def _proj_res(acts, ws, res, tm=512):
    t, d = res.shape
    n = len(acts)
    in_specs = ([pl.BlockSpec((tm, a.shape[1]), lambda i: (i, 0)) for a in acts]
                + [pl.BlockSpec(w.shape, lambda i: (0, 0)) for w in ws]
                + [pl.BlockSpec((tm, d), lambda i: (i, 0))])
    return pl.pallas_call(
        functools.partial(_proj_res_kernel, n=n),
        grid=(t // tm,),
        in_specs=in_specs,
        out_specs=pl.BlockSpec((tm, d), lambda i: (i, 0)),
        out_shape=jax.ShapeDtypeStruct((t, d), F32),
        compiler_params=_cparams(("parallel",)),
        name="proj_res",
    )(*acts, *ws, res)


def _mlp_kernel(x_ref, g_ref, wu_ref, wd_ref, gf_ref, o_ref, xn_scr, acc_scr, *, final_norm):
    j = pl.program_id(1)

    @pl.when(j == 0)
    def _():
        x = x_ref[...]
        xn_scr[...] = _rms(x, g_ref[...]).astype(BF16)
        acc_scr[...] = x

    h = _dot(xn_scr[...], wu_ref[...])
    h = jnp.square(jnp.maximum(h, 0.0)).astype(BF16)
    acc_scr[...] += _dot(h, wd_ref[...])

    @pl.when(j == pl.num_programs(1) - 1)
    def _():
        y = acc_scr[...]
        if final_norm:
            y = _rms(y, gf_ref[...])
        o_ref[...] = y


def _mlp(x, g, w_up, w_down, g_final, final_norm, tm=512, tf=1024):
    t, d = x.shape
    f = w_up.shape[1]
    return pl.pallas_call(
        functools.partial(_mlp_kernel, final_norm=final_norm),
        grid=(t // tm, f // tf),
        in_specs=[pl.BlockSpec((tm, d), lambda i, j: (i, 0)),
                  pl.BlockSpec((1, d), lambda i, j: (0, 0)),
                  pl.BlockSpec((d, tf), lambda i, j: (0, j)),
                  pl.BlockSpec((tf, d), lambda i, j: (j, 0)),
                  pl.BlockSpec((1, d), lambda i, j: (0, 0))],
        out_specs=pl.BlockSpec((tm, d), lambda i, j: (i, 0)),
        out_shape=jax.ShapeDtypeStruct((t, d), F32),
        scratch_shapes=[pltpu.VMEM((tm, d), BF16), pltpu.VMEM((tm, d), F32)],
        compiler_params=_cparams(("parallel", "arbitrary")),
        name="mlp",
    )(x, g.reshape(1, d), w_up, w_down, g_final.reshape(1, d))


SSD_STEP = 2 * SSD_CHUNK
CONV_STRIDE = SSD_STEP // 8 + 1
CONV_ROWS = 8 * CONV_STRIDE
SSD_SLABS = SSD_XBC // LANES


def _ssd_kernel(z_ref, xbc_ref, dt_ref, cw_ref, cb_ref, dtb_ref, alog_ref, dskip_ref, gn_ref,
                e_ref, o_ref, ubuf, xa_scr, hst, y_scr):
    L = SSD_CHUNK
    TL = SSD_STEP
    c = pl.program_id(1)

    @pl.when(c == 0)
    def _():
        hst[...] = jnp.zeros_like(hst)
        ubuf[...] = jnp.zeros_like(ubuf)

    for sl in range(SSD_SLABS):
        ubuf[sl, 8:8 + TL, :] = xbc_ref[:, sl * LANES:(sl + 1) * LANES]

    def conv_slab(sl, carry):
        w = cw_ref[sl]
        b = cb_ref[sl]
        for r in range(CONV_STRIDE):
            acc = b
            for k in range(SSD_CONV):
                acc = acc + w[k:k + 1, :] * ubuf[sl, pl.ds(5 + k + r, 8, stride=CONV_STRIDE), :]
            xa_scr[sl, pl.ds(r, 8, stride=CONV_STRIDE), :] = acc * _sigmoid(acc)
        ubuf[sl, 0:8, :] = ubuf[sl, TL:TL + 8, :]
        return carry

    lax.fori_loop(0, SSD_SLABS, conv_slab, 0)
    nxs = SSD_INNER // LANES
    ngs = SSD_GN // LANES
    row = lax.broadcasted_iota(jnp.int32, (L, L), 0)
    col = lax.broadcasted_iota(jnp.int32, (L, L), 1)
    causal = row >= col
    tril = causal.astype(BF16)
    e01 = e_ref[...]
    lane = lax.broadcasted_iota(jnp.int32, (L, LANES), 1)
    neg_a = -jnp.exp(alog_ref[...])
    gw = SSD_INNER // SSD_GROUPS
    hpl = LANES // HEAD_DIM

    for ci in range(TL // L):
        rows = slice(ci * L, (ci + 1) * L)
        xs = jnp.concatenate([xa_scr[sl, rows, :] for sl in range(nxs)], axis=1)
        dt = _softplus(dt_ref[rows, :] + dtb_ref[...])
        acum = _sel_dot_rhs(tril, dt * neg_a)
        alast = acum[L - 1:L, :]
        dtx = _dot(dt.astype(BF16), e01)
        eacx = _dot(jnp.exp(acum).astype(BF16), e01)
        dtex = _dot(jnp.exp(alast - acum).astype(BF16), e01)
        cdx = _sel_dot_lhs(jnp.exp(acum[L - 8:L, :]), e01)[7:8, :]
        xc = xs * dtx
        xcd = (xc * dtex).astype(BF16)
        act = acum.T

        for g in range(SSD_GROUPS):
            b_f32 = xa_scr[nxs + g, rows, :]
            bg = b_f32.astype(BF16)
            bgt = b_f32.T.astype(BF16)
            cg = xa_scr[nxs + ngs + g, rows, :].astype(BF16)
            cb = _dot_nt(cg, bg)
            hprev = hst[g]
            y_off = _dot(cg, hprev.astype(BF16)) * eacx[:, g * gw:(g + 1) * gw]
            hst[g] = hprev * cdx[:, g * gw:(g + 1) * gw] + _dot(bgt, xcd[:, g * gw:(g + 1) * gw])
            for pr in range(gw // LANES):
                lo = g * gw + pr * LANES
                xcp = xc[:, lo:lo + LANES]
                ms, xm = [], []
                for hh in range(hpl):
                    h = lo // HEAD_DIM + hh
                    dec = jnp.exp(jnp.where(causal, acum[:, h:h + 1] - act[h:h + 1, :], NEG_INF))
                    ms.append((cb * dec).astype(BF16))
                    in_head = (lane >= hh * HEAD_DIM) & (lane < (hh + 1) * HEAD_DIM)
                    xm.append(jnp.where(in_head, xcp, 0.0).astype(BF16))
                y_diag = _dot(jnp.concatenate(ms, axis=1), jnp.concatenate(xm, axis=0))
                y_scr[rows, lo:lo + LANES] = y_diag + y_off[:, pr * LANES:(pr + 1) * LANES]

        y = y_scr[rows, :] + xs * dskip_ref[...]
        z = z_ref[rows, :]
        y = y * (z * _sigmoid(z))
        for g in range(SSD_GROUPS):
            yg = y[:, g * gw:(g + 1) * gw]
            ms = jnp.mean(yg * yg, axis=-1, keepdims=True)
            o_ref[rows, g * gw:(g + 1) * gw] = (
                yg * lax.rsqrt(ms + NORM_EPS) * gn_ref[:, g * gw:(g + 1) * gw]).astype(o_ref.dtype)


def _pad_lanes(v):
    return jnp.pad(v.astype(F32), (0, LANES - v.shape[0])).reshape(1, LANES)


def _ssd(z, xbc, dt, conv_w, conv_b, dt_bias, a_log, d_skip, gate_norm, bsz, s):
    L = SSD_STEP
    nc = s // L
    e01 = (jnp.arange(LANES)[:, None] == (jnp.arange(SSD_INNER) // HEAD_DIM)[None, :]).astype(BF16)
    dsx = jnp.repeat(d_skip.astype(F32), HEAD_DIM).reshape(1, SSD_INNER)
    cw = conv_w.astype(F32).reshape(SSD_CONV, SSD_SLABS, LANES).transpose(1, 0, 2)
    cw = jnp.pad(cw, ((0, 0), (0, 8 - SSD_CONV), (0, 0)))
    cb = conv_b.astype(F32).reshape(SSD_SLABS, 1, LANES)
    tok = lambda b, c: (b * nc + c, 0)
    const = lambda b, c: (0, 0)
    const3 = lambda b, c: (0, 0, 0)
    return pl.pallas_call(
        _ssd_kernel,
        grid=(bsz, nc),
        in_specs=[pl.BlockSpec((L, SSD_INNER), tok),
                  pl.BlockSpec((L, SSD_XBC), tok),
                  pl.BlockSpec((L, LANES), tok),
                  pl.BlockSpec((SSD_SLABS, 8, LANES), const3),
                  pl.BlockSpec((SSD_SLABS, 1, LANES), const3),
                  pl.BlockSpec((1, LANES), const),
                  pl.BlockSpec((1, LANES), const),
                  pl.BlockSpec((1, SSD_INNER), const),
                  pl.BlockSpec((1, SSD_INNER), const),
                  pl.BlockSpec((LANES, SSD_INNER), const)],
        out_specs=pl.BlockSpec((L, SSD_INNER), tok),
        out_shape=jax.ShapeDtypeStruct((bsz * s, SSD_INNER), BF16),
        scratch_shapes=[pltpu.VMEM((SSD_SLABS, 8 + CONV_ROWS + 8, LANES), F32),
                        pltpu.VMEM((SSD_SLABS, CONV_ROWS, LANES), F32),
                        pltpu.VMEM((SSD_GROUPS, SSD_STATE, SSD_INNER // SSD_GROUPS), F32),
                        pltpu.VMEM((L, SSD_INNER), F32)],
        compiler_params=_cparams(("parallel", "arbitrary")),
        name="ssd",
    )(z, xbc, dt, cw, cb, _pad_lanes(dt_bias), _pad_lanes(a_log), dsx,
      gate_norm.astype(F32).reshape(1, SSD_INNER), e01)


def _swap_halves(x):
    lane = lax.broadcasted_iota(jnp.int32, x.shape, 1)
    first = (lane % HEAD_DIM) < HEAD_DIM // 2
    return jnp.where(first, pltpu.roll(x, LANES - HEAD_DIM // 2, 1), pltpu.roll(x, HEAD_DIM // 2, 1))


def _rope(x, cos, sin_signed):
    return x * cos + _swap_halves(x) * sin_signed


def _fold8(x, op):
    parts = [x[r * 8:(r + 1) * 8, :] for r in range(x.shape[0] // 8)]
    return _tree(parts, op)


def _tree(parts, op):
    while len(parts) > 1:
        parts = [op(parts[a], parts[a + 1]) if a + 1 < len(parts) else parts[a]
                 for a in range(0, len(parts), 2)]
    return parts[0]


ATT_GROUP = 8


def _attend(i, nh, score_fn, row_shift, vt_scr, o_ref, st_scr, m_scr, acc_scr, ot_scr):
    tq = ATT_TILE
    krow = lax.broadcasted_iota(jnp.int32, (tq, tq), 0)
    qcol = lax.broadcasted_iota(jnp.int32, (tq, tq), 1)
    m_scr[...] = jnp.full(m_scr.shape, NEG_INF, F32)
    acc_scr[...] = jnp.zeros(acc_scr.shape, F32)

    def trip(js, diagonal_last):
        shifts, alphas = [], []
        for h in range(nh):
            folds = []
            for u, j in enumerate(js):
                s_t = score_fn(h, j)
                if diagonal_last and u == len(js) - 1:
                    s_t = jnp.where(krow <= qcol, s_t, NEG_INF)
                st_scr[h, u] = s_t
                folds.append(_fold8(s_t, jnp.maximum))
            m_trip = jnp.max(_tree(folds, jnp.maximum), axis=0, keepdims=True)
            rs = row_shift[h]
            m_old = m_scr[h]
            m_new = jnp.maximum(m_old, m_trip if rs is None else m_trip + rs)
            m_scr[h] = m_new
            shifts.append(m_new if rs is None else m_new - rs)
            alphas.append(jnp.exp2(m_old - m_new))
        for h in range(nh):
            pv = [_dot(vt_scr[j, h], jnp.exp2(st_scr[h, u] - shifts[h]).astype(BF16)) for u, j in enumerate(js)]
            acc_scr[h] = alphas[h] * acc_scr[h] + _tree(pv, jnp.add)

    rem = i % ATT_GROUP
    for r in range(ATT_GROUP):
        @pl.when(rem == r)
        def _(r=r):
            trip([i - r + u for u in range(r + 1)], True)

    def full_trip(g, carry):
        trip([ATT_GROUP * g + u for u in range(ATT_GROUP)], False)
        return carry

    lax.fori_loop(0, i // ATT_GROUP, full_trip, 0)
    for h in range(nh):
        ot_scr[h * HEAD_DIM:(h + 1) * HEAD_DIM, :] = acc_scr[h, :HEAD_DIM, :] / acc_scr[h, HEAD_DIM:HEAD_DIM + 1, :]
    for g in range(nh * HEAD_DIM // LANES):
        o_ref[:, g * LANES:(g + 1) * LANES] = ot_scr[g * LANES:(g + 1) * LANES, :].T.astype(o_ref.dtype)


V_ROWS = HEAD_DIM + 16
ATT_GROUPS = 2
ATT_LANES = ATT_GROUPS * LANES
ATT_NH = ATT_LANES // HEAD_DIM


def _store_vt(vt_scr, jb, v_blk):
    vt = v_blk.T
    for h in range(v_blk.shape[1] // HEAD_DIM):
        vt_scr[jb, h, :HEAD_DIM, :] = vt[h * HEAD_DIM:(h + 1) * HEAD_DIM, :].astype(BF16)
        vt_scr[jb, h, HEAD_DIM:, :] = jnp.ones((V_ROWS - HEAD_DIM, v_blk.shape[0]), BF16)


def _attend_scratch(nb):
    tq = ATT_TILE
    return [pltpu.VMEM((nb, ATT_NH, V_ROWS, tq), BF16),
            pltpu.VMEM((ATT_NH, ATT_GROUP, tq, tq), F32),
            pltpu.VMEM((ATT_NH, 1, tq), F32),
            pltpu.VMEM((ATT_NH, V_ROWS, tq), F32),
            pltpu.VMEM((ATT_LANES, tq), F32)]


def _masked_heads(x, scale):
    lane = lax.broadcasted_iota(jnp.int32, (x.shape[0], LANES), 1)
    out = []
    for h in range(x.shape[1] // HEAD_DIM):
        g, hh = divmod(h, LANES // HEAD_DIM)
        xg = x[:, g * LANES:(g + 1) * LANES]
        in_head = (lane >= hh * HEAD_DIM) & (lane < (hh + 1) * HEAD_DIM)
        out.append(jnp.where(in_head, xg * scale, 0.0))
    return out


def _moba_kernel(q_ref, k_ref, v_ref, cos_ref, sin_ref, o_ref, kr_scr, km_scr, bias_scr,
                 vt_scr, st_scr, m_scr, acc_scr, ot_scr):
    tq = ATT_TILE
    nb = kr_scr.shape[1]
    i = pl.program_id(2)
    scale = HEAD_DIM ** -0.5
    hpg = LANES // HEAD_DIM

    @pl.when(i == 0)
    def _():
        for jb in range(nb):
            rows = slice(jb * tq, (jb + 1) * tq)
            for g in range(ATT_GROUPS):
                lanes = slice(g * LANES, (g + 1) * LANES)
                kr = _rope(k_ref[rows, lanes], cos_ref[rows, :], sin_ref[rows, :])
                kr_scr[g, jb] = kr.astype(BF16)
                km_scr[g, jb:jb + 1, :] = jnp.mean(kr, axis=0, keepdims=True)
            _store_vt(vt_scr, jb, v_ref[rows, :])

    qrows = pl.ds(pl.multiple_of(i * tq, tq), tq)
    cos, sin = cos_ref[qrows, :], sin_ref[qrows, :]
    qr = jnp.concatenate([_rope(q_ref[:, g * LANES:(g + 1) * LANES], cos, sin) for g in range(ATT_GROUPS)], axis=1)
    jrow = lax.broadcasted_iota(jnp.int32, (nb, tq), 0)
    valid = jrow < i
    qs = []
    for h, qm in enumerate(_masked_heads(qr, 1.0)):
        gate = lax.dot_general(km_scr[h // hpg], qm, (((1,), (1,)), ((), ())),
                               precision=lax.Precision.HIGHEST, preferred_element_type=F32)
        gate = jnp.where(valid, gate, NEG_INF)
        cnt = jnp.zeros((nb, tq), jnp.int32)
        for jp in range(nb):
            gj = gate[jp:jp + 1, :]
            beats = (gj > gate) | ((gj == gate) & (jp < jrow))
            cnt = cnt + (beats & (jp < i)).astype(jnp.int32)
        chosen = valid & (cnt < MOBA_TOPK)
        bias_scr[h] = jnp.where(chosen | (jrow == i), 0.0, NEG_INF)
        qs.append((qm * (scale * LOG2E)).astype(BF16))

    def score_fn(h, j):
        return _dot_nt(kr_scr[h // hpg, j], qs[h]) + bias_scr[h, pl.ds(j, 1), :]

    _attend(i, ATT_NH, score_fn, [None] * ATT_NH, vt_scr, o_ref, st_scr, m_scr, acc_scr, ot_scr)


def _moba(q, k, v, cos, sin_signed, bsz, s):
    tq = ATT_TILE
    nb = s // tq
    q3, k3, v3 = (a.reshape(bsz, s, MOBA_INNER) for a in (q, k, v))
    whole = pl.BlockSpec((None, s, ATT_LANES), lambda b, p, i: (b, 0, p))
    tile = pl.BlockSpec((None, tq, ATT_LANES), lambda b, p, i: (b, i, p))
    tab = pl.BlockSpec((s, LANES), lambda b, p, i: (0, 0))
    out = pl.pallas_call(
        _moba_kernel,
        grid=(bsz, MOBA_INNER // ATT_LANES, nb),
        in_specs=[tile, whole, whole, tab, tab],
        out_specs=tile,
        out_shape=jax.ShapeDtypeStruct((bsz, s, MOBA_INNER), BF16),
        scratch_shapes=[pltpu.VMEM((ATT_GROUPS, nb, tq, LANES), BF16),
                        pltpu.VMEM((ATT_GROUPS, nb, LANES), F32),
                        pltpu.VMEM((ATT_NH, nb, tq), F32)] + _attend_scratch(nb),
        compiler_params=_cparams(("parallel", "parallel", "arbitrary")),
        name="moba",
    )(q3, k3, v3, cos, sin_signed)
    return out.reshape(bsz * s, MOBA_INNER)


def _fox_gate_kernel(f_ref, b_ref, c_ref, ct_ref):
    tb = ATT_TILE
    s = f_ref.shape[0]
    row = lax.broadcasted_iota(jnp.int32, (tb, tb), 0)
    col = lax.broadcasted_iota(jnp.int32, (tb, tb), 1)
    tril = (row >= col).astype(BF16)
    carry = jnp.zeros((1, LANES), F32)
    for jb in range(s // tb):
        rows = slice(jb * tb, (jb + 1) * tb)
        log_f = -_softplus(-(f_ref[rows, :] + b_ref[...]))
        cblk = _sel_dot_rhs(tril, log_f) + carry
        c_ref[rows, :] = cblk
        ct_ref[:, rows] = cblk.T
        carry = cblk[tb - 1:tb, :]


def _fox_gate(f, bias, bsz, s):
    return pl.pallas_call(
        _fox_gate_kernel,
        grid=(bsz,),
        in_specs=[pl.BlockSpec((None, s, LANES), lambda b: (b, 0, 0)),
                  pl.BlockSpec((1, LANES), lambda b: (0, 0))],
        out_specs=[pl.BlockSpec((None, s, LANES), lambda b: (b, 0, 0)),
                   pl.BlockSpec((None, LANES, s), lambda b: (b, 0, 0))],
        out_shape=[jax.ShapeDtypeStruct((bsz, s, LANES), F32),
                   jax.ShapeDtypeStruct((bsz, LANES, s), F32)],
        compiler_params=_cparams(("parallel",)),
        name="fox_gate",
    )(f.reshape(bsz, s, LANES), _pad_lanes(bias))


def _fox_kernel(q_ref, k_ref, v_ref, c_ref, ct_ref, o_ref, k_scr, ck_scr,
                vt_scr, st_scr, m_scr, acc_scr, ot_scr):
    tq = ATT_TILE
    nb = k_scr.shape[1]
    p = pl.program_id(1)
    i = pl.program_id(2)
    scale = HEAD_DIM ** -0.5
    hpg = LANES // HEAD_DIM

    @pl.when(i == 0)
    def _():
        lane = lax.broadcasted_iota(jnp.int32, (tq, LANES), 1)
        for jb in range(nb):
            rows = slice(jb * tq, (jb + 1) * tq)
            for g in range(ATT_GROUPS):
                k_scr[g, jb] = k_ref[rows, g * LANES:(g + 1) * LANES].astype(BF16)
            _store_vt(vt_scr, jb, v_ref[rows, :])
            cblk = c_ref[rows, :] * LOG2E
            for h in range(ATT_NH):
                colv = jnp.sum(jnp.where(lane == p * ATT_NH + h, cblk, 0.0), axis=1, keepdims=True)
                ck_scr[h, jb] = jnp.broadcast_to(colv, (tq, LANES))

    qs = [qm.astype(BF16) for qm in _masked_heads(q_ref[...], scale * LOG2E)]
    cqs = [ct_ref[pl.ds(p * ATT_NH + h, 1), :] * LOG2E for h in range(ATT_NH)]

    def score_fn(h, j):
        ck = ck_scr[h, j]
        return _dot_nt(k_scr[h // hpg, j], qs[h]) - jnp.concatenate([ck] * (tq // LANES), axis=1)

    _attend(i, ATT_NH, score_fn, cqs, vt_scr, o_ref, st_scr, m_scr, acc_scr, ot_scr)


def _fox(q, k, v, c, ct, bsz, s):
    tq = ATT_TILE
    nb = s // tq
    q3, k3, v3 = (a.reshape(bsz, s, FOX_INNER) for a in (q, k, v))
    whole = pl.BlockSpec((None, s, ATT_LANES), lambda b, p, i: (b, 0, p))
    tile = pl.BlockSpec((None, tq, ATT_LANES), lambda b, p, i: (b, i, p))
    out = pl.pallas_call(
        _fox_kernel,
        grid=(bsz, FOX_INNER // ATT_LANES, nb),
        in_specs=[tile, whole, whole,
                  pl.BlockSpec((None, s, LANES), lambda b, p, i: (b, 0, 0)),
                  pl.BlockSpec((None, LANES, tq), lambda b, p, i: (b, 0, i))],
        out_specs=tile,
        out_shape=jax.ShapeDtypeStruct((bsz, s, FOX_INNER), BF16),
        scratch_shapes=[pltpu.VMEM((ATT_GROUPS, nb, tq, LANES), BF16),
                        pltpu.VMEM((ATT_NH, nb, tq, LANES), F32)] + _attend_scratch(nb),
        compiler_params=_cparams(("parallel", "parallel", "arbitrary")),
        name="fox",
    )(q3, k3, v3, c, ct)
    return out.reshape(bsz * s, FOX_INNER)


def _rope_tables(s):
    half = HEAD_DIM // 2
    inv = jnp.power(ROPE_THETA, -jnp.arange(half, dtype=F32) / half)
    ang = jnp.arange(s, dtype=F32)[:, None] * inv[None, :]
    cos, sin = jnp.cos(ang), jnp.sin(ang)
    reps = LANES // HEAD_DIM
    return (jnp.tile(jnp.concatenate([cos, cos], axis=1), (1, reps)),
            jnp.tile(jnp.concatenate([-sin, sin], axis=1), (1, reps)))


def _even_layer(x, bsz, s, g, w_in, conv_w, conv_b, dt_bias, a_log, d_skip, gate_norm, w_out, rope_tabs):
    d = x.shape[1]
    o1 = SSD_INNER
    o2 = o1 + SSD_XBC
    o3 = o2 + SSD_HEADS
    w = jnp.concatenate([w_in[:, :o2], w_in[:, o3:], w_in[:, o2:o3],
                         jnp.zeros((d, LANES - SSD_HEADS), w_in.dtype)], axis=1).astype(BF16)
    z, xbc, q, k, v, dt = _norm_matmul(
        x, g, w, (SSD_INNER, SSD_XBC, MOBA_INNER, MOBA_INNER, MOBA_INNER, LANES))
    y_ssd = _ssd(z, xbc, dt, conv_w, conv_b, dt_bias, a_log, d_skip, gate_norm, bsz, s)
    y_att = _moba(q, k, v, *rope_tabs, bsz, s)
    w_o = w_out.astype(BF16)
    return _proj_res([y_ssd, y_att], [w_o[:SSD_INNER], w_o[SSD_INNER:]], x)


def _odd_layer(x, bsz, s, g, w_in, fgate_bias, w_out):
    d = x.shape[1]
    w = jnp.concatenate([w_in, jnp.zeros((d, LANES - FOX_HEADS), w_in.dtype)], axis=1).astype(BF16)
    q, k, v, f = _norm_matmul(x, g, w, (FOX_INNER, FOX_INNER, FOX_INNER, LANES))
    c, ct = _fox_gate(f, fgate_bias, bsz, s)
    y = _fox(q, k, v, c, ct, bsz, s)
    return _proj_res([y], [w_out.astype(BF16)], x)


def kernel(x, norm_mix_even, w_in_even, conv_w, conv_b, dt_bias, a_log, d_skip, ssd_gate_norm, w_out_even,
           norm_mix_odd, w_in_odd, fgate_bias, w_out_odd, norm_mlp, w_up, w_down, final_norm):
    bsz, s, d = x.shape
    depth = norm_mlp.shape[0]
    assert s % ATT_TILE == 0 and s % SSD_STEP == 0
    h = x.reshape(bsz * s, d)
    rope_tabs = _rope_tables(s)
    for layer in range(depth):
        i = layer // 2
        if layer % 2 == 0:
            h = _even_layer(h, bsz, s, norm_mix_even[i], w_in_even[i], conv_w[i], conv_b[i], dt_bias[i],
                            a_log[i], d_skip[i], ssd_gate_norm[i], w_out_even[i], rope_tabs)
        else:
            h = _odd_layer(h, bsz, s, norm_mix_odd[i], w_in_odd[i], fgate_bias[i], w_out_odd[i])
        h = _mlp(h, norm_mlp[layer], w_up[layer].astype(BF16), w_down[layer].astype(BF16), final_norm,
                 final_norm=(layer == depth - 1))
    return h.reshape(bsz, s, d)
```

```python
import functools
import math

import jax
import jax.numpy as jnp
from jax import lax
from jax.experimental import pallas as pl
from jax.experimental.pallas import tpu as pltpu

NORM_EPS = 1e-5
ROPE_THETA = 10000.0
LANES = 128
HEAD_DIM = 64
SSD_HEADS = 16
SSD_GROUPS = 4
SSD_STATE = 128
SSD_CONV = 4
SSD_CHUNK = 128
SSD_INNER = SSD_HEADS * HEAD_DIM
SSD_GN = SSD_GROUPS * SSD_STATE
SSD_XBC = SSD_INNER + 2 * SSD_GN
MOBA_HEADS = 8
MOBA_INNER = MOBA_HEADS * HEAD_DIM
MOBA_BLOCK = 256
MOBA_TOPK = 3
FOX_HEADS = 16
FOX_INNER = FOX_HEADS * HEAD_DIM
ATT_TILE = 256
VMEM_LIMIT = 56 * 1024 * 1024

F32 = jnp.float32
BF16 = jnp.bfloat16
NEG_INF = float("-inf")
LOG2E = math.log2(math.e)


def _cparams(sem):
    return pltpu.CompilerParams(dimension_semantics=sem, vmem_limit_bytes=VMEM_LIMIT)


def _resident(shape):
    return pl.BlockSpec(shape, lambda i: (0,) * len(shape), pipeline_mode=pl.Buffered(1))


def _rms(x, g):
    return x * lax.rsqrt(jnp.mean(x * x, axis=-1, keepdims=True) + NORM_EPS) * g


def _sigmoid(x):
    return 1.0 / (1.0 + jnp.exp(-x))


def _softplus(x):
    return jnp.maximum(x, 0.0) + jnp.log1p(jnp.exp(-jnp.abs(x)))


def _split3(x):
    hi = x.astype(BF16)
    r = x - hi.astype(F32)
    mid = r.astype(BF16)
    lo = (r - mid.astype(F32)).astype(BF16)
    return hi, mid, lo


def _dot(a, b):
    return jnp.dot(a, b, preferred_element_type=F32)


def _dot_nt(a, b):
    return lax.dot_general(a, b, (((1,), (1,)), ((), ())), preferred_element_type=F32)


def _dot_tn(a, b):
    return lax.dot_general(a, b, (((0,), (0,)), ((), ())), preferred_element_type=F32)


def _sel_dot_rhs(m01, x):
    hi, mid, lo = _split3(x)
    return _dot(m01, hi) + _dot(m01, mid) + _dot(m01, lo)


def _sel_dot_lhs(x, m01):
    hi, mid, lo = _split3(x)
    return _dot(hi, m01) + _dot(mid, m01) + _dot(lo, m01)


def _norm_matmul_kernel(x_ref, g_ref, w_ref, *out_refs, splits):
    half = x_ref.shape[0] // 2
    for r in range(2):
        rows = slice(r * half, (r + 1) * half)
        xn = _rms(x_ref[rows, :], g_ref[...]).astype(BF16)
        for o_ref, (a, b) in zip(out_refs, splits):
            o_ref[rows, :] = _dot(xn, w_ref[:, a:b]).astype(o_ref.dtype)


def _norm_matmul(x, g, w, widths, tm=512):
    t, d = x.shape
    n = w.shape[1]
    splits, a = [], 0
    for wd in widths:
        splits.append((a, a + wd))
        a += wd
    assert a == n and t % tm == 0
    return pl.pallas_call(
        functools.partial(_norm_matmul_kernel, splits=tuple(splits)),
        grid=(t // tm,),
        in_specs=[pl.BlockSpec((tm, d), lambda i: (i, 0)),
                  _resident((1, d)), _resident((d, n))],
        out_specs=[pl.BlockSpec((tm, wd), lambda i: (i, 0)) for wd in widths],
        out_shape=[jax.ShapeDtypeStruct((t, wd), F32) for wd in widths],
        compiler_params=_cparams(("parallel",)),
        name="norm_matmul",
    )(x, g.reshape(1, d), w)


MLP_CHUNK = 1024


def _proj_mlp_kernel(*refs, n, final_norm):
    a_refs, w_refs = refs[:n], refs[n:2 * n]
    x_ref, g_ref, wu_ref, wd_ref, gf_ref, o_ref = refs[2 * n:]
    proj = _dot(a_refs[0][...], w_refs[0][...])
    for a_ref, w_ref in zip(a_refs[1:], w_refs[1:]):
        proj = proj + _dot(a_ref[...], w_ref[...])
    x = x_ref[...] + proj
    xn = _rms(x, g_ref[...]).astype(BF16)
    acc = x
    for c in range(wu_ref.shape[1] // MLP_CHUNK):
        cols = slice(c * MLP_CHUNK, (c + 1) * MLP_CHUNK)
        h = _dot(xn, wu_ref[:, cols])
        h = jnp.square(jnp.maximum(h, 0.0)).astype(BF16)
        acc = acc + _dot(h, wd_ref[cols, :])
    if final_norm:
        acc = _rms(acc, gf_ref[...])
    o_ref[...] = acc


def _proj_mlp(acts, ws, x, g, w_up, w_down, g_final, final_norm, tm=512):
    t, d = x.shape
    n = len(acts)
    row = lambda width: pl.BlockSpec((tm, width), lambda i: (i, 0))
    const = lambda shape: pl.BlockSpec(shape, lambda i: (0, 0))
    in_specs = ([row(a.shape[1]) for a in acts] + [const(w.shape) for w in ws]
                + [row(d), const((1, d)), _resident(w_up.shape), _resident(w_down.shape), const((1, d))])
    return pl.pallas_call(
        functools.partial(_proj_mlp_kernel, n=n, final_norm=final_norm),
        grid=(t // tm,),
        in_specs=in_specs,
        out_specs=row(d),
        out_shape=jax.ShapeDtypeStruct((t, d), F32),
        compiler_params=_cparams(("parallel",)),
        name="proj_mlp",
    )(*acts, *ws, x, g.reshape(1, d), w_up, w_down, g_final.reshape(1, d))


SSD_STEP = 2 * SSD_CHUNK
CONV_STRIDE = SSD_STEP // 8 + 1
CONV_ROWS = 8 * CONV_STRIDE
SSD_SLABS = SSD_XBC // LANES


def _ssd_kernel(z_ref, xbc_ref, dt_ref, cw_ref, cb_ref, dtb_ref, alog_ref, dskip_ref, gn_ref,
                e_ref, o_ref, ubuf, xa_scr, hst, y_scr):
    L = SSD_CHUNK
    TL = SSD_STEP
    c = pl.program_id(1)

    @pl.when(c == 0)
    def _():
        hst[...] = jnp.zeros_like(hst)
        ubuf[...] = jnp.zeros_like(ubuf)

    for sl in range(SSD_SLABS):
        ubuf[sl, 8:8 + TL, :] = xbc_ref[:, sl * LANES:(sl + 1) * LANES]

    def conv_slab(sl, carry):
        w = cw_ref[sl]
        b = cb_ref[sl]
        for r in range(CONV_STRIDE):
            acc = b
            for k in range(SSD_CONV):
                acc = acc + w[k:k + 1, :] * ubuf[sl, pl.ds(5 + k + r, 8, stride=CONV_STRIDE), :]
            xa_scr[sl, pl.ds(r, 8, stride=CONV_STRIDE), :] = acc * _sigmoid(acc)
        ubuf[sl, 0:8, :] = ubuf[sl, TL:TL + 8, :]
        return carry

    lax.fori_loop(0, SSD_SLABS, conv_slab, 0)
    nxs = SSD_INNER // LANES
    ngs = SSD_GN // LANES
    row = lax.broadcasted_iota(jnp.int32, (L, L), 0)
    col = lax.broadcasted_iota(jnp.int32, (L, L), 1)
    causal = row >= col
    tril = causal.astype(BF16)
    e01 = e_ref[...]
    lane = lax.broadcasted_iota(jnp.int32, (L, LANES), 1)
    neg_a = -jnp.exp(alog_ref[...])
    gw = SSD_INNER // SSD_GROUPS
    hpl = LANES // HEAD_DIM

    for ci in range(TL // L):
        rows = slice(ci * L, (ci + 1) * L)
        xs = jnp.concatenate([xa_scr[sl, rows, :] for sl in range(nxs)], axis=1)
        dt = _softplus(dt_ref[rows, :] + dtb_ref[...])
        acum = _sel_dot_rhs(tril, dt * neg_a)
        alast = acum[L - 1:L, :]
        dtx = _dot(dt.astype(BF16), e01)
        eacx = _dot(jnp.exp(acum).astype(BF16), e01)
        dtex = _dot(jnp.exp(alast - acum).astype(BF16), e01)
        cdx = _sel_dot_lhs(jnp.exp(acum[L - 8:L, :]), e01)[7:8, :]
        xc = xs * dtx
        xcd = (xc * dtex).astype(BF16)
        act = acum.T

        for g in range(SSD_GROUPS):
            b_f32 = xa_scr[nxs + g, rows, :]
            bg = b_f32.astype(BF16)
            bgt = b_f32.T.astype(BF16)
            cg = xa_scr[nxs + ngs + g, rows, :].astype(BF16)
            cb = _dot_nt(cg, bg)
            hprev = hst[g]
            y_off = _dot(cg, hprev.astype(BF16)) * eacx[:, g * gw:(g + 1) * gw]
            hst[g] = hprev * cdx[:, g * gw:(g + 1) * gw] + _dot(bgt, xcd[:, g * gw:(g + 1) * gw])
            for pr in range(gw // LANES):
                lo = g * gw + pr * LANES
                xcp = xc[:, lo:lo + LANES]
                ms, xm = [], []
                for hh in range(hpl):
                    h = lo // HEAD_DIM + hh
                    dec = jnp.exp(jnp.where(causal, acum[:, h:h + 1] - act[h:h + 1, :], NEG_INF))
                    ms.append((cb * dec).astype(BF16))
                    in_head = (lane >= hh * HEAD_DIM) & (lane < (hh + 1) * HEAD_DIM)
                    xm.append(jnp.where(in_head, xcp, 0.0).astype(BF16))
                y_diag = _dot(jnp.concatenate(ms, axis=1), jnp.concatenate(xm, axis=0))
                y_scr[rows, lo:lo + LANES] = y_diag + y_off[:, pr * LANES:(pr + 1) * LANES]

        y = y_scr[rows, :] + xs * dskip_ref[...]
        z = z_ref[rows, :]
        y = y * (z * _sigmoid(z))
        for g in range(SSD_GROUPS):
            yg = y[:, g * gw:(g + 1) * gw]
            ms = jnp.mean(yg * yg, axis=-1, keepdims=True)
            o_ref[rows, g * gw:(g + 1) * gw] = (
                yg * lax.rsqrt(ms + NORM_EPS) * gn_ref[:, g * gw:(g + 1) * gw]).astype(o_ref.dtype)


def _pad_lanes(v):
    return jnp.pad(v.astype(F32), (0, LANES - v.shape[0])).reshape(1, LANES)


def _ssd(z, xbc, dt, conv_w, conv_b, dt_bias, a_log, d_skip, gate_norm, bsz, s):
    L = SSD_STEP
    nc = s // L
    e01 = (jnp.arange(LANES)[:, None] == (jnp.arange(SSD_INNER) // HEAD_DIM)[None, :]).astype(BF16)
    dsx = jnp.repeat(d_skip.astype(F32), HEAD_DIM).reshape(1, SSD_INNER)
    cw = conv_w.astype(F32).reshape(SSD_CONV, SSD_SLABS, LANES).transpose(1, 0, 2)
    cw = jnp.pad(cw, ((0, 0), (0, 8 - SSD_CONV), (0, 0)))
    cb = conv_b.astype(F32).reshape(SSD_SLABS, 1, LANES)
    tok = lambda b, c: (b * nc + c, 0)
    const = lambda b, c: (0, 0)
    const3 = lambda b, c: (0, 0, 0)
    return pl.pallas_call(
        _ssd_kernel,
        grid=(bsz, nc),
        in_specs=[pl.BlockSpec((L, SSD_INNER), tok),
                  pl.BlockSpec((L, SSD_XBC), tok),
                  pl.BlockSpec((L, LANES), tok),
                  pl.BlockSpec((SSD_SLABS, 8, LANES), const3),
                  pl.BlockSpec((SSD_SLABS, 1, LANES), const3),
                  pl.BlockSpec((1, LANES), const),
                  pl.BlockSpec((1, LANES), const),
                  pl.BlockSpec((1, SSD_INNER), const),
                  pl.BlockSpec((1, SSD_INNER), const),
                  pl.BlockSpec((LANES, SSD_INNER), const)],
        out_specs=pl.BlockSpec((L, SSD_INNER), tok),
        out_shape=jax.ShapeDtypeStruct((bsz * s, SSD_INNER), BF16),
        scratch_shapes=[pltpu.VMEM((SSD_SLABS, 8 + CONV_ROWS + 8, LANES), F32),
                        pltpu.VMEM((SSD_SLABS, CONV_ROWS, LANES), F32),
                        pltpu.VMEM((SSD_GROUPS, SSD_STATE, SSD_INNER // SSD_GROUPS), F32),
                        pltpu.VMEM((L, SSD_INNER), F32)],
        compiler_params=_cparams(("parallel", "arbitrary")),
        name="ssd",
    )(z, xbc, dt, cw, cb, _pad_lanes(dt_bias), _pad_lanes(a_log), dsx,
      gate_norm.astype(F32).reshape(1, SSD_INNER), e01)


def _swap_halves(x):
    lane = lax.broadcasted_iota(jnp.int32, x.shape, 1)
    first = (lane % HEAD_DIM) < HEAD_DIM // 2
    return jnp.where(first, pltpu.roll(x, LANES - HEAD_DIM // 2, 1), pltpu.roll(x, HEAD_DIM // 2, 1))


def _rope(x, cos, sin_signed):
    return x * cos + _swap_halves(x) * sin_signed


def _fold8(x, op):
    parts = [x[r * 8:(r + 1) * 8, :] for r in range(x.shape[0] // 8)]
    return _tree(parts, op)


def _tree(parts, op):
    while len(parts) > 1:
        parts = [op(parts[a], parts[a + 1]) if a + 1 < len(parts) else parts[a]
                 for a in range(0, len(parts), 2)]
    return parts[0]


ATT_GROUP = 8


def _attend(i, nh, score_fn, row_shift, vt_scr, o_ref, st_scr, m_scr, acc_scr, ot_scr):
    tq = ATT_TILE
    krow = lax.broadcasted_iota(jnp.int32, (tq, tq), 0)
    qcol = lax.broadcasted_iota(jnp.int32, (tq, tq), 1)
    m_scr[...] = jnp.full(m_scr.shape, NEG_INF, F32)
    acc_scr[...] = jnp.zeros(acc_scr.shape, F32)

    def trip(js, diagonal_last):
        shifts, alphas = [], []
        for h in range(nh):
            folds = []
            for u, j in enumerate(js):
                s_t = score_fn(h, j)
                if diagonal_last and u == len(js) - 1:
                    s_t = jnp.where(krow <= qcol, s_t, NEG_INF)
                st_scr[h, u] = s_t
                folds.append(_fold8(s_t, jnp.maximum))
            m_trip = jnp.max(_tree(folds, jnp.maximum), axis=0, keepdims=True)
            rs = row_shift[h]
            m_old = m_scr[h]
            m_new = jnp.maximum(m_old, m_trip if rs is None else m_trip + rs)
            m_scr[h] = m_new
            shifts.append(m_new if rs is None else m_new - rs)
            alphas.append(jnp.exp2(m_old - m_new))
        for h in range(nh):
            pv = [_dot(vt_scr[j, h], jnp.exp2(st_scr[h, u] - shifts[h]).astype(BF16)) for u, j in enumerate(js)]
            acc_scr[h] = alphas[h] * acc_scr[h] + _tree(pv, jnp.add)

    rem = i % ATT_GROUP
    for r in range(ATT_GROUP):
        @pl.when(rem == r)
        def _(r=r):
            trip([i - r + u for u in range(r + 1)], True)

    def full_trip(g, carry):
        trip([ATT_GROUP * g + u for u in range(ATT_GROUP)], False)
        return carry

    lax.fori_loop(0, i // ATT_GROUP, full_trip, 0)
    for h in range(nh):
        ot_scr[h * HEAD_DIM:(h + 1) * HEAD_DIM, :] = acc_scr[h, :HEAD_DIM, :] / acc_scr[h, HEAD_DIM:HEAD_DIM + 1, :]
    for g in range(nh * HEAD_DIM // LANES):
        o_ref[:, g * LANES:(g + 1) * LANES] = ot_scr[g * LANES:(g + 1) * LANES, :].T.astype(o_ref.dtype)


V_ROWS = HEAD_DIM + 16
ATT_GROUPS = 2
ATT_LANES = ATT_GROUPS * LANES
ATT_NH = ATT_LANES // HEAD_DIM


def _store_vt(vt_scr, jb, v_blk):
    vt = v_blk.T
    for h in range(v_blk.shape[1] // HEAD_DIM):
        vt_scr[jb, h, :HEAD_DIM, :] = vt[h * HEAD_DIM:(h + 1) * HEAD_DIM, :].astype(BF16)
        vt_scr[jb, h, HEAD_DIM:, :] = jnp.ones((V_ROWS - HEAD_DIM, v_blk.shape[0]), BF16)


def _attend_scratch(nb):
    tq = ATT_TILE
    return [pltpu.VMEM((nb, ATT_NH, V_ROWS, tq), BF16),
            pltpu.VMEM((ATT_NH, ATT_GROUP, tq, tq), F32),
            pltpu.VMEM((ATT_NH, 1, tq), F32),
            pltpu.VMEM((ATT_NH, V_ROWS, tq), F32),
            pltpu.VMEM((ATT_LANES, tq), F32)]


def _masked_heads(x, scale):
    lane = lax.broadcasted_iota(jnp.int32, (x.shape[0], LANES), 1)
    out = []
    for h in range(x.shape[1] // HEAD_DIM):
        g, hh = divmod(h, LANES // HEAD_DIM)
        xg = x[:, g * LANES:(g + 1) * LANES]
        in_head = (lane >= hh * HEAD_DIM) & (lane < (hh + 1) * HEAD_DIM)
        out.append(jnp.where(in_head, xg * scale, 0.0))
    return out


def _moba_kernel(q_ref, k_ref, v_ref, cos_ref, sin_ref, o_ref, kr_scr, km_scr, bias_scr,
                 vt_scr, st_scr, m_scr, acc_scr, ot_scr):
    tq = ATT_TILE
    nb = kr_scr.shape[1]
    i = pl.program_id(2)
    scale = HEAD_DIM ** -0.5
    hpg = LANES // HEAD_DIM

    @pl.when(i == 0)
    def _():
        for jb in range(nb):
            rows = slice(jb * tq, (jb + 1) * tq)
            for g in range(ATT_GROUPS):
                lanes = slice(g * LANES, (g + 1) * LANES)
                kr = _rope(k_ref[rows, lanes], cos_ref[rows, :], sin_ref[rows, :])
                kr_scr[g, jb] = kr.astype(BF16)
                km_scr[g, jb:jb + 1, :] = jnp.mean(kr, axis=0, keepdims=True)
            _store_vt(vt_scr, jb, v_ref[rows, :])

    qrows = pl.ds(pl.multiple_of(i * tq, tq), tq)
    cos, sin = cos_ref[qrows, :], sin_ref[qrows, :]
    qr = jnp.concatenate([_rope(q_ref[:, g * LANES:(g + 1) * LANES], cos, sin) for g in range(ATT_GROUPS)], axis=1)
    jrow = lax.broadcasted_iota(jnp.int32, (nb, tq), 0)
    valid = jrow < i
    qs = []
    for h, qm in enumerate(_masked_heads(qr, 1.0)):
        gate = lax.dot_general(km_scr[h // hpg], qm, (((1,), (1,)), ((), ())),
                               precision=lax.Precision.HIGHEST, preferred_element_type=F32)
        gate = jnp.where(valid, gate, NEG_INF)
        cnt = jnp.zeros((nb, tq), jnp.int32)
        for jp in range(nb):
            gj = gate[jp:jp + 1, :]
            beats = (gj > gate) | ((gj == gate) & (jp < jrow))
            cnt = cnt + (beats & (jp < i)).astype(jnp.int32)
        chosen = valid & (cnt < MOBA_TOPK)
        bias_scr[h] = jnp.where(chosen | (jrow == i), 0.0, NEG_INF)
        qs.append((qm * (scale * LOG2E)).astype(BF16))

    def score_fn(h, j):
        return _dot_nt(kr_scr[h // hpg, j], qs[h]) + bias_scr[h, pl.ds(j, 1), :]

    _attend(i, ATT_NH, score_fn, [None] * ATT_NH, vt_scr, o_ref, st_scr, m_scr, acc_scr, ot_scr)


def _moba(q, k, v, cos, sin_signed, bsz, s):
    tq = ATT_TILE
    nb = s // tq
    q3, k3, v3 = (a.reshape(bsz, s, MOBA_INNER) for a in (q, k, v))
    whole = pl.BlockSpec((None, s, ATT_LANES), lambda b, p, i: (b, 0, p))
    tile = pl.BlockSpec((None, tq, ATT_LANES), lambda b, p, i: (b, i, p))
    tab = pl.BlockSpec((s, LANES), lambda b, p, i: (0, 0))
    out = pl.pallas_call(
        _moba_kernel,
        grid=(bsz, MOBA_INNER // ATT_LANES, nb),
        in_specs=[tile, whole, whole, tab, tab],
        out_specs=tile,
        out_shape=jax.ShapeDtypeStruct((bsz, s, MOBA_INNER), BF16),
        scratch_shapes=[pltpu.VMEM((ATT_GROUPS, nb, tq, LANES), BF16),
                        pltpu.VMEM((ATT_GROUPS, nb, LANES), F32),
                        pltpu.VMEM((ATT_NH, nb, tq), F32)] + _attend_scratch(nb),
        compiler_params=_cparams(("parallel", "parallel", "arbitrary")),
        name="moba",
    )(q3, k3, v3, cos, sin_signed)
    return out.reshape(bsz * s, MOBA_INNER)


def _fox_gate_kernel(f_ref, b_ref, c_ref, ct_ref):
    tb = ATT_TILE
    s = f_ref.shape[0]
    row = lax.broadcasted_iota(jnp.int32, (tb, tb), 0)
    col = lax.broadcasted_iota(jnp.int32, (tb, tb), 1)
    tril = (row >= col).astype(BF16)
    carry = jnp.zeros((1, LANES), F32)
    for jb in range(s // tb):
        rows = slice(jb * tb, (jb + 1) * tb)
        log_f = -_softplus(-(f_ref[rows, :] + b_ref[...]))
        cblk = _sel_dot_rhs(tril, log_f) + carry
        c_ref[rows, :] = cblk
        ct_ref[:, rows] = cblk.T
        carry = cblk[tb - 1:tb, :]


def _fox_gate(f, bias, bsz, s):
    return pl.pallas_call(
        _fox_gate_kernel,
        grid=(bsz,),
        in_specs=[pl.BlockSpec((None, s, LANES), lambda b: (b, 0, 0)),
                  pl.BlockSpec((1, LANES), lambda b: (0, 0))],
        out_specs=[pl.BlockSpec((None, s, LANES), lambda b: (b, 0, 0)),
                   pl.BlockSpec((None, LANES, s), lambda b: (b, 0, 0))],
        out_shape=[jax.ShapeDtypeStruct((bsz, s, LANES), F32),
                   jax.ShapeDtypeStruct((bsz, LANES, s), F32)],
        compiler_params=_cparams(("parallel",)),
        name="fox_gate",
    )(f.reshape(bsz, s, LANES), _pad_lanes(bias))


def _fox_kernel(q_ref, k_ref, v_ref, c_ref, ct_ref, o_ref, k_scr, ck_scr,
                vt_scr, st_scr, m_scr, acc_scr, ot_scr):
    tq = ATT_TILE
    nb = k_scr.shape[1]
    p = pl.program_id(1)
    i = pl.program_id(2)
    scale = HEAD_DIM ** -0.5
    hpg = LANES // HEAD_DIM

    @pl.when(i == 0)
    def _():
        lane = lax.broadcasted_iota(jnp.int32, (tq, LANES), 1)
        for jb in range(nb):
            rows = slice(jb * tq, (jb + 1) * tq)
            for g in range(ATT_GROUPS):
                k_scr[g, jb] = k_ref[rows, g * LANES:(g + 1) * LANES].astype(BF16)
            _store_vt(vt_scr, jb, v_ref[rows, :])
            cblk = c_ref[rows, :] * LOG2E
            for h in range(ATT_NH):
                colv = jnp.sum(jnp.where(lane == p * ATT_NH + h, cblk, 0.0), axis=1, keepdims=True)
                ck_scr[h, jb] = jnp.broadcast_to(colv, (tq, LANES))

    qs = [qm.astype(BF16) for qm in _masked_heads(q_ref[...], scale * LOG2E)]
    cqs = [ct_ref[pl.ds(p * ATT_NH + h, 1), :] * LOG2E for h in range(ATT_NH)]

    def score_fn(h, j):
        ck = ck_scr[h, j]
        return _dot_nt(k_scr[h // hpg, j], qs[h]) - jnp.concatenate([ck] * (tq // LANES), axis=1)

    _attend(i, ATT_NH, score_fn, cqs, vt_scr, o_ref, st_scr, m_scr, acc_scr, ot_scr)


def _fox(q, k, v, c, ct, bsz, s):
    tq = ATT_TILE
    nb = s // tq
    q3, k3, v3 = (a.reshape(bsz, s, FOX_INNER) for a in (q, k, v))
    whole = pl.BlockSpec((None, s, ATT_LANES), lambda b, p, i: (b, 0, p))
    tile = pl.BlockSpec((None, tq, ATT_LANES), lambda b, p, i: (b, i, p))
    out = pl.pallas_call(
        _fox_kernel,
        grid=(bsz, FOX_INNER // ATT_LANES, nb),
        in_specs=[tile, whole, whole,
                  pl.BlockSpec((None, s, LANES), lambda b, p, i: (b, 0, 0)),
                  pl.BlockSpec((None, LANES, tq), lambda b, p, i: (b, 0, i))],
        out_specs=tile,
        out_shape=jax.ShapeDtypeStruct((bsz, s, FOX_INNER), BF16),
        scratch_shapes=[pltpu.VMEM((ATT_GROUPS, nb, tq, LANES), BF16),
                        pltpu.VMEM((ATT_NH, nb, tq, LANES), F32)] + _attend_scratch(nb),
        compiler_params=_cparams(("parallel", "parallel", "arbitrary")),
        name="fox",
    )(q3, k3, v3, c, ct)
    return out.reshape(bsz * s, FOX_INNER)


def _rope_tables(s):
    half = HEAD_DIM // 2
    inv = jnp.power(ROPE_THETA, -jnp.arange(half, dtype=F32) / half)
    ang = jnp.arange(s, dtype=F32)[:, None] * inv[None, :]
    cos, sin = jnp.cos(ang), jnp.sin(ang)
    reps = LANES // HEAD_DIM
    return (jnp.tile(jnp.concatenate([cos, cos], axis=1), (1, reps)),
            jnp.tile(jnp.concatenate([-sin, sin], axis=1), (1, reps)))


def _even_layer(x, bsz, s, g, w_in, conv_w, conv_b, dt_bias, a_log, d_skip, gate_norm, w_out, rope_tabs):
    d = x.shape[1]
    o1 = SSD_INNER
    o2 = o1 + SSD_XBC
    o3 = o2 + SSD_HEADS
    w = jnp.concatenate([w_in[:, :o2], w_in[:, o3:], w_in[:, o2:o3],
                         jnp.zeros((d, LANES - SSD_HEADS), w_in.dtype)], axis=1).astype(BF16)
    z, xbc, q, k, v, dt = _norm_matmul(
        x, g, w, (SSD_INNER, SSD_XBC, MOBA_INNER, MOBA_INNER, MOBA_INNER, LANES))
    y_ssd = _ssd(z, xbc, dt, conv_w, conv_b, dt_bias, a_log, d_skip, gate_norm, bsz, s)
    y_att = _moba(q, k, v, *rope_tabs, bsz, s)
    w_o = w_out.astype(BF16)
    return [y_ssd, y_att], [w_o[:SSD_INNER], w_o[SSD_INNER:]]


def _odd_layer(x, bsz, s, g, w_in, fgate_bias, w_out):
    d = x.shape[1]
    w = jnp.concatenate([w_in, jnp.zeros((d, LANES - FOX_HEADS), w_in.dtype)], axis=1).astype(BF16)
    q, k, v, f = _norm_matmul(x, g, w, (FOX_INNER, FOX_INNER, FOX_INNER, LANES))
    c, ct = _fox_gate(f, fgate_bias, bsz, s)
    y = _fox(q, k, v, c, ct, bsz, s)
    return [y], [w_out.astype(BF16)]


def kernel(x, norm_mix_even, w_in_even, conv_w, conv_b, dt_bias, a_log, d_skip, ssd_gate_norm, w_out_even,
           norm_mix_odd, w_in_odd, fgate_bias, w_out_odd, norm_mlp, w_up, w_down, final_norm):
    bsz, s, d = x.shape
    depth = norm_mlp.shape[0]
    assert s % ATT_TILE == 0 and s % SSD_STEP == 0
    h = x.reshape(bsz * s, d)
    rope_tabs = _rope_tables(s)
    for layer in range(depth):
        i = layer // 2
        if layer % 2 == 0:
            acts, ws = _even_layer(h, bsz, s, norm_mix_even[i], w_in_even[i], conv_w[i], conv_b[i], dt_bias[i],
                            a_log[i], d_skip[i], ssd_gate_norm[i], w_out_even[i], rope_tabs)
        else:
            acts, ws = _odd_layer(h, bsz, s, norm_mix_odd[i], w_in_odd[i], fgate_bias[i], w_out_odd[i])
        h = _proj_mlp(acts, ws, h, norm_mlp[layer], w_up[layer].astype(BF16), w_down[layer].astype(BF16),
                      final_norm, final_norm=(layer == depth - 1))
    return h.reshape(bsz, s, d)
```

```python
import functools
import math

import jax
import jax.numpy as jnp
from jax import lax
from jax.experimental import pallas as pl
from jax.experimental.pallas import tpu as pltpu

NORM_EPS = 1e-5
ROPE_THETA = 10000.0
LANES = 128
HEAD_DIM = 64
SSD_HEADS = 16
SSD_GROUPS = 4
SSD_STATE = 128
SSD_CONV = 4
SSD_CHUNK = 128
SSD_INNER = SSD_HEADS * HEAD_DIM
SSD_GN = SSD_GROUPS * SSD_STATE
SSD_XBC = SSD_INNER + 2 * SSD_GN
MOBA_HEADS = 8
MOBA_INNER = MOBA_HEADS * HEAD_DIM
MOBA_BLOCK = 256
MOBA_TOPK = 3
FOX_HEADS = 16
FOX_INNER = FOX_HEADS * HEAD_DIM
ATT_TILE = 256
VMEM_LIMIT = 56 * 1024 * 1024

F32 = jnp.float32
BF16 = jnp.bfloat16
NEG_INF = float("-inf")
LOG2E = math.log2(math.e)


def _cparams(sem):
    return pltpu.CompilerParams(dimension_semantics=sem, vmem_limit_bytes=VMEM_LIMIT)


def _resident(shape):
    return pl.BlockSpec(shape, lambda i: (0,) * len(shape), pipeline_mode=pl.Buffered(1))


def _rms(x, g):
    return x * lax.rsqrt(jnp.mean(x * x, axis=-1, keepdims=True) + NORM_EPS) * g


def _sigmoid(x):
    return 1.0 / (1.0 + jnp.exp(-x))


def _softplus(x):
    return jnp.maximum(x, 0.0) + jnp.log1p(jnp.exp(-jnp.abs(x)))


def _split3(x):
    hi = x.astype(BF16)
    r = x - hi.astype(F32)
    mid = r.astype(BF16)
    lo = (r - mid.astype(F32)).astype(BF16)
    return hi, mid, lo


def _dot(a, b):
    return jnp.dot(a, b, preferred_element_type=F32)


def _dot_nt(a, b):
    return lax.dot_general(a, b, (((1,), (1,)), ((), ())), preferred_element_type=F32)


def _dot_tn(a, b):
    return lax.dot_general(a, b, (((0,), (0,)), ((), ())), preferred_element_type=F32)


def _sel_dot_rhs(m01, x):
    hi, mid, lo = _split3(x)
    return _dot(m01, hi) + _dot(m01, mid) + _dot(m01, lo)


def _sel_dot_lhs(x, m01):
    hi, mid, lo = _split3(x)
    return _dot(hi, m01) + _dot(mid, m01) + _dot(lo, m01)


def _norm_matmul_kernel(x_ref, g_ref, w_ref, *out_refs, splits):
    half = x_ref.shape[0] // 2
    for r in range(2):
        rows = slice(r * half, (r + 1) * half)
        xn = _rms(x_ref[rows, :], g_ref[...]).astype(BF16)
        for o_ref, (a, b) in zip(out_refs, splits):
            o_ref[rows, :] = _dot(xn, w_ref[:, a:b]).astype(o_ref.dtype)


def _norm_matmul(x, g, w, widths, tm=512):
    t, d = x.shape
    n = w.shape[1]
    splits, a = [], 0
    for wd in widths:
        splits.append((a, a + wd))
        a += wd
    assert a == n and t % tm == 0
    return pl.pallas_call(
        functools.partial(_norm_matmul_kernel, splits=tuple(splits)),
        grid=(t // tm,),
        in_specs=[pl.BlockSpec((tm, d), lambda i: (i, 0)),
                  _resident((1, d)), _resident((d, n))],
        out_specs=[pl.BlockSpec((tm, wd), lambda i: (i, 0)) for wd in widths],
        out_shape=[jax.ShapeDtypeStruct((t, wd), F32) for wd in widths],
        compiler_params=_cparams(("parallel",)),
        name="norm_matmul",
    )(x, g.reshape(1, d), w)


MLP_CHUNK = 1024


def _proj_mlp_kernel(*refs, n, final_norm):
    a_refs, w_refs = refs[:n], refs[n:2 * n]
    x_ref, g_ref, wu_ref, wd_ref, gf_ref, o_ref = refs[2 * n:]
    proj = _dot(a_refs[0][...], w_refs[0][...])
    for a_ref, w_ref in zip(a_refs[1:], w_refs[1:]):
        proj = proj + _dot(a_ref[...], w_ref[...])
    x = x_ref[...] + proj
    xn = _rms(x, g_ref[...]).astype(BF16)
    acc = x
    for c in range(wu_ref.shape[1] // MLP_CHUNK):
        cols = slice(c * MLP_CHUNK, (c + 1) * MLP_CHUNK)
        h = _dot(xn, wu_ref[:, cols])
        h = jnp.square(jnp.maximum(h, 0.0)).astype(BF16)
        acc = acc + _dot(h, wd_ref[cols, :])
    if final_norm:
        acc = _rms(acc, gf_ref[...])
    o_ref[...] = acc


def _proj_mlp(acts, ws, x, g, w_up, w_down, g_final, final_norm, tm=512):
    t, d = x.shape
    n = len(acts)
    row = lambda width: pl.BlockSpec((tm, width), lambda i: (i, 0))
    const = lambda shape: pl.BlockSpec(shape, lambda i: (0, 0))
    in_specs = ([row(a.shape[1]) for a in acts] + [const(w.shape) for w in ws]
                + [row(d), const((1, d)), _resident(w_up.shape), _resident(w_down.shape), const((1, d))])
    return pl.pallas_call(
        functools.partial(_proj_mlp_kernel, n=n, final_norm=final_norm),
        grid=(t // tm,),
        in_specs=in_specs,
        out_specs=row(d),
        out_shape=jax.ShapeDtypeStruct((t, d), F32),
        compiler_params=_cparams(("parallel",)),
        name="proj_mlp",
    )(*acts, *ws, x, g.reshape(1, d), w_up, w_down, g_final.reshape(1, d))


SSD_STEP = 2 * SSD_CHUNK
CONV_STRIDE = SSD_STEP // 8 + 1
CONV_ROWS = 8 * CONV_STRIDE
SSD_SLABS = SSD_XBC // LANES


def _ssd_kernel(z_ref, xbc_ref, dt_ref, cw_ref, cb_ref, dtb_ref, alog_ref, dskip_ref, gn_ref,
                e_ref, o_ref, ubuf, xa_scr, hst, y_scr):
    L = SSD_CHUNK
    TL = SSD_STEP
    c = pl.program_id(1)

    @pl.when(c == 0)
    def _():
        hst[...] = jnp.zeros_like(hst)
        ubuf[...] = jnp.zeros_like(ubuf)

    for sl in range(SSD_SLABS):
        ubuf[sl, 8:8 + TL, :] = xbc_ref[:, sl * LANES:(sl + 1) * LANES]

    def conv_slab(sl, carry):
        w = cw_ref[sl]
        b = cb_ref[sl]
        for r in range(CONV_STRIDE):
            acc = b
            for k in range(SSD_CONV):
                acc = acc + w[k:k + 1, :] * ubuf[sl, pl.ds(5 + k + r, 8, stride=CONV_STRIDE), :]
            xa_scr[sl, pl.ds(r, 8, stride=CONV_STRIDE), :] = acc * _sigmoid(acc)
        ubuf[sl, 0:8, :] = ubuf[sl, TL:TL + 8, :]
        return carry

    lax.fori_loop(0, SSD_SLABS, conv_slab, 0)
    nxs = SSD_INNER // LANES
    ngs = SSD_GN // LANES
    row = lax.broadcasted_iota(jnp.int32, (L, L), 0)
    col = lax.broadcasted_iota(jnp.int32, (L, L), 1)
    causal = row >= col
    tril = causal.astype(BF16)
    e01 = e_ref[...]
    lane = lax.broadcasted_iota(jnp.int32, (L, LANES), 1)
    neg_a = -jnp.exp(alog_ref[...])
    gw = SSD_INNER // SSD_GROUPS
    hpl = LANES // HEAD_DIM

    for ci in range(TL // L):
        rows = slice(ci * L, (ci + 1) * L)
        xs = jnp.concatenate([xa_scr[sl, rows, :] for sl in range(nxs)], axis=1)
        dt = _softplus(dt_ref[rows, :] + dtb_ref[...])
        acum = _sel_dot_rhs(tril, dt * neg_a)
        alast = acum[L - 1:L, :]
        dtx = _dot(dt.astype(BF16), e01)
        eacx = _dot(jnp.exp(acum).astype(BF16), e01)
        dtex = _dot(jnp.exp(alast - acum).astype(BF16), e01)
        cdx = _sel_dot_lhs(jnp.exp(acum[L - 8:L, :]), e01)[7:8, :]
        xc = xs * dtx
        xcd = (xc * dtex).astype(BF16)
        act = acum.T

        for g in range(SSD_GROUPS):
            b_f32 = xa_scr[nxs + g, rows, :]
            bg = b_f32.astype(BF16)
            bgt = b_f32.T.astype(BF16)
            cg = xa_scr[nxs + ngs + g, rows, :].astype(BF16)
            cb = _dot_nt(cg, bg)
            hprev = hst[g]
            y_off = _dot(cg, hprev.astype(BF16)) * eacx[:, g * gw:(g + 1) * gw]
            hst[g] = hprev * cdx[:, g * gw:(g + 1) * gw] + _dot(bgt, xcd[:, g * gw:(g + 1) * gw])
            for pr in range(gw // LANES):
                lo = g * gw + pr * LANES
                xcp = xc[:, lo:lo + LANES]
                ms, xm = [], []
                for hh in range(hpl):
                    h = lo // HEAD_DIM + hh
                    dec = jnp.exp(jnp.where(causal, acum[:, h:h + 1] - act[h:h + 1, :], NEG_INF))
                    ms.append((cb * dec).astype(BF16))
                    in_head = (lane >= hh * HEAD_DIM) & (lane < (hh + 1) * HEAD_DIM)
                    xm.append(jnp.where(in_head, xcp, 0.0).astype(BF16))
                y_diag = _dot(jnp.concatenate(ms, axis=1), jnp.concatenate(xm, axis=0))
                y_scr[rows, lo:lo + LANES] = y_diag + y_off[:, pr * LANES:(pr + 1) * LANES]

        y = y_scr[rows, :] + xs * dskip_ref[...]
        z = z_ref[rows, :]
        y = y * (z * _sigmoid(z))
        for g in range(SSD_GROUPS):
            yg = y[:, g * gw:(g + 1) * gw]
            ms = jnp.mean(yg * yg, axis=-1, keepdims=True)
            o_ref[rows, g * gw:(g + 1) * gw] = (
                yg * lax.rsqrt(ms + NORM_EPS) * gn_ref[:, g * gw:(g + 1) * gw]).astype(o_ref.dtype)


def _pad_lanes(v):
    return jnp.pad(v.astype(F32), (0, LANES - v.shape[0])).reshape(1, LANES)


def _ssd(z, xbc, dt, conv_w, conv_b, dt_bias, a_log, d_skip, gate_norm, bsz, s):
    L = SSD_STEP
    nc = s // L
    e01 = (jnp.arange(LANES)[:, None] == (jnp.arange(SSD_INNER) // HEAD_DIM)[None, :]).astype(BF16)
    dsx = jnp.repeat(d_skip.astype(F32), HEAD_DIM).reshape(1, SSD_INNER)
    cw = conv_w.astype(F32).reshape(SSD_CONV, SSD_SLABS, LANES).transpose(1, 0, 2)
    cw = jnp.pad(cw, ((0, 0), (0, 8 - SSD_CONV), (0, 0)))
    cb = conv_b.astype(F32).reshape(SSD_SLABS, 1, LANES)
    tok = lambda b, c: (b * nc + c, 0)
    const = lambda b, c: (0, 0)
    const3 = lambda b, c: (0, 0, 0)
    return pl.pallas_call(
        _ssd_kernel,
        grid=(bsz, nc),
        in_specs=[pl.BlockSpec((L, SSD_INNER), tok),
                  pl.BlockSpec((L, SSD_XBC), tok),
                  pl.BlockSpec((L, LANES), tok),
                  pl.BlockSpec((SSD_SLABS, 8, LANES), const3),
                  pl.BlockSpec((SSD_SLABS, 1, LANES), const3),
                  pl.BlockSpec((1, LANES), const),
                  pl.BlockSpec((1, LANES), const),
                  pl.BlockSpec((1, SSD_INNER), const),
                  pl.BlockSpec((1, SSD_INNER), const),
                  pl.BlockSpec((LANES, SSD_INNER), const)],
        out_specs=pl.BlockSpec((L, SSD_INNER), tok),
        out_shape=jax.ShapeDtypeStruct((bsz * s, SSD_INNER), BF16),
        scratch_shapes=[pltpu.VMEM((SSD_SLABS, 8 + CONV_ROWS + 8, LANES), F32),
                        pltpu.VMEM((SSD_SLABS, CONV_ROWS, LANES), F32),
                        pltpu.VMEM((SSD_GROUPS, SSD_STATE, SSD_INNER // SSD_GROUPS), F32),
                        pltpu.VMEM((L, SSD_INNER), F32)],
        compiler_params=_cparams(("parallel", "arbitrary")),
        name="ssd",
    )(z, xbc, dt, cw, cb, _pad_lanes(dt_bias), _pad_lanes(a_log), dsx,
      gate_norm.astype(F32).reshape(1, SSD_INNER), e01)


def _swap_halves(x):
    lane = lax.broadcasted_iota(jnp.int32, x.shape, 1)
    first = (lane % HEAD_DIM) < HEAD_DIM // 2
    return jnp.where(first, pltpu.roll(x, LANES - HEAD_DIM // 2, 1), pltpu.roll(x, HEAD_DIM // 2, 1))


def _rope(x, cos, sin_signed):
    return x * cos + _swap_halves(x) * sin_signed


def _fold8(x, op):
    parts = [x[r * 8:(r + 1) * 8, :] for r in range(x.shape[0] // 8)]
    return _tree(parts, op)


def _tree(parts, op):
    while len(parts) > 1:
        parts = [op(parts[a], parts[a + 1]) if a + 1 < len(parts) else parts[a]
                 for a in range(0, len(parts), 2)]
    return parts[0]


ATT_GROUP = 8


def _attend(i, nh, score_fn, row_shift, vt_scr, o_ref, o_rows, st_scr, m_scr, acc_scr, ot_scr):
    tq = ATT_TILE
    krow = lax.broadcasted_iota(jnp.int32, (tq, tq), 0)
    qcol = lax.broadcasted_iota(jnp.int32, (tq, tq), 1)
    m_scr[...] = jnp.full(m_scr.shape, NEG_INF, F32)
    acc_scr[...] = jnp.zeros(acc_scr.shape, F32)

    def trip(js, diagonal_last):
        shifts, alphas = [], []
        for h in range(nh):
            folds = []
            for u, j in enumerate(js):
                s_t = score_fn(h, j)
                if diagonal_last and u == len(js) - 1:
                    s_t = jnp.where(krow <= qcol, s_t, NEG_INF)
                st_scr[h, u] = s_t
                folds.append(_fold8(s_t, jnp.maximum))
            m_trip = jnp.max(_tree(folds, jnp.maximum), axis=0, keepdims=True)
            rs = row_shift[h]
            m_old = m_scr[h]
            m_new = jnp.maximum(m_old, m_trip if rs is None else m_trip + rs)
            m_scr[h] = m_new
            shifts.append(m_new if rs is None else m_new - rs)
            alphas.append(jnp.exp2(m_old - m_new))
        for h in range(nh):
            pv = [_dot(vt_scr[j, h], jnp.exp2(st_scr[h, u] - shifts[h]).astype(BF16)) for u, j in enumerate(js)]
            acc_scr[h] = alphas[h] * acc_scr[h] + _tree(pv, jnp.add)

    rem = i % ATT_GROUP
    for r in range(ATT_GROUP):
        @pl.when(rem == r)
        def _(r=r):
            trip([i - r + u for u in range(r + 1)], True)

    def full_trip(g, carry):
        trip([ATT_GROUP * g + u for u in range(ATT_GROUP)], False)
        return carry

    lax.fori_loop(0, i // ATT_GROUP, full_trip, 0)
    for h in range(nh):
        ot_scr[h * HEAD_DIM:(h + 1) * HEAD_DIM, :] = acc_scr[h, :HEAD_DIM, :] / acc_scr[h, HEAD_DIM:HEAD_DIM + 1, :]
    for g in range(nh * HEAD_DIM // LANES):
        o_ref[o_rows, g * LANES:(g + 1) * LANES] = ot_scr[g * LANES:(g + 1) * LANES, :].T.astype(o_ref.dtype)


V_ROWS = HEAD_DIM + 16
ATT_GROUPS = 2
ATT_LANES = ATT_GROUPS * LANES
ATT_NH = ATT_LANES // HEAD_DIM
ATT_TPS = 4


def _store_vt(vt_scr, jb, v_blk):
    vt = v_blk.T
    for h in range(v_blk.shape[1] // HEAD_DIM):
        vt_scr[jb, h, :HEAD_DIM, :] = vt[h * HEAD_DIM:(h + 1) * HEAD_DIM, :].astype(BF16)
        vt_scr[jb, h, HEAD_DIM:, :] = jnp.ones((V_ROWS - HEAD_DIM, v_blk.shape[0]), BF16)


def _attend_scratch(nb):
    tq = ATT_TILE
    return [pltpu.VMEM((nb, ATT_NH, V_ROWS, tq), BF16),
            pltpu.VMEM((ATT_NH, ATT_GROUP, tq, tq), F32),
            pltpu.VMEM((ATT_NH, 1, tq), F32),
            pltpu.VMEM((ATT_NH, V_ROWS, tq), F32),
            pltpu.VMEM((ATT_LANES, tq), F32)]


def _masked_heads(x, scale):
    lane = lax.broadcasted_iota(jnp.int32, (x.shape[0], LANES), 1)
    out = []
    for h in range(x.shape[1] // HEAD_DIM):
        g, hh = divmod(h, LANES // HEAD_DIM)
        xg = x[:, g * LANES:(g + 1) * LANES]
        in_head = (lane >= hh * HEAD_DIM) & (lane < (hh + 1) * HEAD_DIM)
        out.append(jnp.where(in_head, xg * scale, 0.0))
    return out


def _moba_kernel(q_ref, k_ref, v_ref, cos_ref, sin_ref, o_ref, kr_scr, km_scr, bias_scr,
                 vt_scr, st_scr, m_scr, acc_scr, ot_scr):
    tq = ATT_TILE
    nb = kr_scr.shape[1]
    step = pl.program_id(2)
    scale = HEAD_DIM ** -0.5
    hpg = LANES // HEAD_DIM

    @pl.when(step == 0)
    def _():
        for jb in range(nb):
            rows = slice(jb * tq, (jb + 1) * tq)
            for g in range(ATT_GROUPS):
                lanes = slice(g * LANES, (g + 1) * LANES)
                kr = _rope(k_ref[rows, lanes], cos_ref[rows, :], sin_ref[rows, :])
                kr_scr[g, jb] = kr.astype(BF16)
                km_scr[g, jb:jb + 1, :] = jnp.mean(kr, axis=0, keepdims=True)
            _store_vt(vt_scr, jb, v_ref[rows, :])

    def tile(t, carry):
        i = step * ATT_TPS + t
        trows = pl.ds(pl.multiple_of(t * tq, tq), tq)
        qrows = pl.ds(pl.multiple_of(i * tq, tq), tq)
        cos, sin = cos_ref[qrows, :], sin_ref[qrows, :]
        qr = jnp.concatenate(
            [_rope(q_ref[trows, g * LANES:(g + 1) * LANES], cos, sin) for g in range(ATT_GROUPS)], axis=1)
        jrow = lax.broadcasted_iota(jnp.int32, (nb, tq), 0)
        valid = jrow < i
        qs = []
        for h, qm in enumerate(_masked_heads(qr, 1.0)):
            gate = lax.dot_general(km_scr[h // hpg], qm, (((1,), (1,)), ((), ())),
                                   precision=lax.Precision.HIGHEST, preferred_element_type=F32)
            gate = jnp.where(valid, gate, NEG_INF)
            cnt = jnp.zeros((nb, tq), jnp.int32)
            for jp in range(nb):
                gj = gate[jp:jp + 1, :]
                beats = (gj > gate) | ((gj == gate) & (jp < jrow))
                cnt = cnt + (beats & (jp < i)).astype(jnp.int32)
            chosen = valid & (cnt < MOBA_TOPK)
            bias_scr[h] = jnp.where(chosen | (jrow == i), 0.0, NEG_INF)
            qs.append((qm * (scale * LOG2E)).astype(BF16))

        def score_fn(h, j):
            return _dot_nt(kr_scr[h // hpg, j], qs[h]) + bias_scr[h, pl.ds(j, 1), :]

        _attend(i, ATT_NH, score_fn, [None] * ATT_NH, vt_scr, o_ref, trows, st_scr, m_scr, acc_scr, ot_scr)
        return carry

    lax.fori_loop(0, ATT_TPS, tile, 0)


def _moba(q, k, v, cos, sin_signed, bsz, s):
    tq = ATT_TILE
    nb = s // tq
    q3, k3, v3 = (a.reshape(bsz, s, MOBA_INNER) for a in (q, k, v))
    whole = pl.BlockSpec((None, s, ATT_LANES), lambda b, p, i: (b, 0, p))
    tile = pl.BlockSpec((None, ATT_TPS * tq, ATT_LANES), lambda b, p, i: (b, i, p))
    tab = pl.BlockSpec((s, LANES), lambda b, p, i: (0, 0))
    out = pl.pallas_call(
        _moba_kernel,
        grid=(bsz, MOBA_INNER // ATT_LANES, nb // ATT_TPS),
        in_specs=[tile, whole, whole, tab, tab],
        out_specs=tile,
        out_shape=jax.ShapeDtypeStruct((bsz, s, MOBA_INNER), BF16),
        scratch_shapes=[pltpu.VMEM((ATT_GROUPS, nb, tq, LANES), BF16),
                        pltpu.VMEM((ATT_GROUPS, nb, LANES), F32),
                        pltpu.VMEM((ATT_NH, nb, tq), F32)] + _attend_scratch(nb),
        compiler_params=_cparams(("parallel", "parallel", "arbitrary")),
        name="moba",
    )(q3, k3, v3, cos, sin_signed)
    return out.reshape(bsz * s, MOBA_INNER)


def _fox_gate_kernel(f_ref, b_ref, c_ref, ct_ref):
    tb = ATT_TILE
    s = f_ref.shape[0]
    row = lax.broadcasted_iota(jnp.int32, (tb, tb), 0)
    col = lax.broadcasted_iota(jnp.int32, (tb, tb), 1)
    tril = (row >= col).astype(BF16)
    carry = jnp.zeros((1, LANES), F32)
    for jb in range(s // tb):
        rows = slice(jb * tb, (jb + 1) * tb)
        log_f = -_softplus(-(f_ref[rows, :] + b_ref[...]))
        cblk = _sel_dot_rhs(tril, log_f) + carry
        c_ref[rows, :] = cblk
        ct_ref[jb] = cblk.T
        carry = cblk[tb - 1:tb, :]


def _fox_gate(f, bias, bsz, s):
    return pl.pallas_call(
        _fox_gate_kernel,
        grid=(bsz,),
        in_specs=[pl.BlockSpec((None, s, LANES), lambda b: (b, 0, 0)),
                  pl.BlockSpec((1, LANES), lambda b: (0, 0))],
        out_specs=[pl.BlockSpec((None, s, LANES), lambda b: (b, 0, 0)),
                   pl.BlockSpec((None, s // ATT_TILE, LANES, ATT_TILE), lambda b: (b, 0, 0, 0))],
        out_shape=[jax.ShapeDtypeStruct((bsz, s, LANES), F32),
                   jax.ShapeDtypeStruct((bsz, s // ATT_TILE, LANES, ATT_TILE), F32)],
        compiler_params=_cparams(("parallel",)),
        name="fox_gate",
    )(f.reshape(bsz, s, LANES), _pad_lanes(bias))


def _fox_kernel(q_ref, k_ref, v_ref, c_ref, ct_ref, o_ref, k_scr, ck_scr,
                vt_scr, st_scr, m_scr, acc_scr, ot_scr):
    tq = ATT_TILE
    nb = k_scr.shape[1]
    p = pl.program_id(1)
    step = pl.program_id(2)
    scale = HEAD_DIM ** -0.5
    hpg = LANES // HEAD_DIM

    @pl.when(step == 0)
    def _():
        lane = lax.broadcasted_iota(jnp.int32, (tq, LANES), 1)
        for jb in range(nb):
            rows = slice(jb * tq, (jb + 1) * tq)
            for g in range(ATT_GROUPS):
                k_scr[g, jb] = k_ref[rows, g * LANES:(g + 1) * LANES].astype(BF16)
            _store_vt(vt_scr, jb, v_ref[rows, :])
            cblk = c_ref[rows, :] * LOG2E
            for h in range(ATT_NH):
                colv = jnp.sum(jnp.where(lane == p * ATT_NH + h, cblk, 0.0), axis=1, keepdims=True)
                ck_scr[h, jb] = jnp.broadcast_to(colv, (tq, LANES))

    def tile(t, carry):
        i = step * ATT_TPS + t
        trows = pl.ds(pl.multiple_of(t * tq, tq), tq)
        qs = [qm.astype(BF16) for qm in _masked_heads(q_ref[trows, :], scale * LOG2E)]
        cqs = [ct_ref[t, pl.ds(p * ATT_NH + h, 1), :] * LOG2E for h in range(ATT_NH)]

        def score_fn(h, j):
            ck = ck_scr[h, j]
            return _dot_nt(k_scr[h // hpg, j], qs[h]) - jnp.concatenate([ck] * (tq // LANES), axis=1)

        _attend(i, ATT_NH, score_fn, cqs, vt_scr, o_ref, trows, st_scr, m_scr, acc_scr, ot_scr)
        return carry

    lax.fori_loop(0, ATT_TPS, tile, 0)


def _fox(q, k, v, c, ct, bsz, s):
    tq = ATT_TILE
    nb = s // tq
    q3, k3, v3 = (a.reshape(bsz, s, FOX_INNER) for a in (q, k, v))
    whole = pl.BlockSpec((None, s, ATT_LANES), lambda b, p, i: (b, 0, p))
    tile = pl.BlockSpec((None, ATT_TPS * tq, ATT_LANES), lambda b, p, i: (b, i, p))
    out = pl.pallas_call(
        _fox_kernel,
        grid=(bsz, FOX_INNER // ATT_LANES, nb // ATT_TPS),
        in_specs=[tile, whole, whole,
                  pl.BlockSpec((None, s, LANES), lambda b, p, i: (b, 0, 0)),
                  pl.BlockSpec((None, ATT_TPS, LANES, tq), lambda b, p, i: (b, i, 0, 0))],
        out_specs=tile,
        out_shape=jax.ShapeDtypeStruct((bsz, s, FOX_INNER), BF16),
        scratch_shapes=[pltpu.VMEM((ATT_GROUPS, nb, tq, LANES), BF16),
                        pltpu.VMEM((ATT_NH, nb, tq, LANES), F32)] + _attend_scratch(nb),
        compiler_params=_cparams(("parallel", "parallel", "arbitrary")),
        name="fox",
    )(q3, k3, v3, c, ct)
    return out.reshape(bsz * s, FOX_INNER)


def _rope_tables(s):
    half = HEAD_DIM // 2
    inv = jnp.power(ROPE_THETA, -jnp.arange(half, dtype=F32) / half)
    ang = jnp.arange(s, dtype=F32)[:, None] * inv[None, :]
    cos, sin = jnp.cos(ang), jnp.sin(ang)
    reps = LANES // HEAD_DIM
    return (jnp.tile(jnp.concatenate([cos, cos], axis=1), (1, reps)),
            jnp.tile(jnp.concatenate([-sin, sin], axis=1), (1, reps)))


def _even_layer(x, bsz, s, g, w_in, conv_w, conv_b, dt_bias, a_log, d_skip, gate_norm, w_out, rope_tabs):
    d = x.shape[1]
    o1 = SSD_INNER
    o2 = o1 + SSD_XBC
    o3 = o2 + SSD_HEADS
    w = jnp.concatenate([w_in[:, :o2], w_in[:, o3:], w_in[:, o2:o3],
                         jnp.zeros((d, LANES - SSD_HEADS), w_in.dtype)], axis=1).astype(BF16)
    z, xbc, q, k, v, dt = _norm_matmul(
        x, g, w, (SSD_INNER, SSD_XBC, MOBA_INNER, MOBA_INNER, MOBA_INNER, LANES))
    y_ssd = _ssd(z, xbc, dt, conv_w, conv_b, dt_bias, a_log, d_skip, gate_norm, bsz, s)
    y_att = _moba(q, k, v, *rope_tabs, bsz, s)
    w_o = w_out.astype(BF16)
    return [y_ssd, y_att], [w_o[:SSD_INNER], w_o[SSD_INNER:]]


def _odd_layer(x, bsz, s, g, w_in, fgate_bias, w_out):
    d = x.shape[1]
    w = jnp.concatenate([w_in, jnp.zeros((d, LANES - FOX_HEADS), w_in.dtype)], axis=1).astype(BF16)
    q, k, v, f = _norm_matmul(x, g, w, (FOX_INNER, FOX_INNER, FOX_INNER, LANES))
    c, ct = _fox_gate(f, fgate_bias, bsz, s)
    y = _fox(q, k, v, c, ct, bsz, s)
    return [y], [w_out.astype(BF16)]


def kernel(x, norm_mix_even, w_in_even, conv_w, conv_b, dt_bias, a_log, d_skip, ssd_gate_norm, w_out_even,
           norm_mix_odd, w_in_odd, fgate_bias, w_out_odd, norm_mlp, w_up, w_down, final_norm):
    bsz, s, d = x.shape
    depth = norm_mlp.shape[0]
    assert s % (ATT_TILE * ATT_TPS) == 0 and s % SSD_STEP == 0
    h = x.reshape(bsz * s, d)
    rope_tabs = _rope_tables(s)
    for layer in range(depth):
        i = layer // 2
        if layer % 2 == 0:
            acts, ws = _even_layer(h, bsz, s, norm_mix_even[i], w_in_even[i], conv_w[i], conv_b[i], dt_bias[i],
                            a_log[i], d_skip[i], ssd_gate_norm[i], w_out_even[i], rope_tabs)
        else:
            acts, ws = _odd_layer(h, bsz, s, norm_mix_odd[i], w_in_odd[i], fgate_bias[i], w_out_odd[i])
        h = _proj_mlp(acts, ws, h, norm_mlp[layer], w_up[layer].astype(BF16), w_down[layer].astype(BF16),
                      final_norm, final_norm=(layer == depth - 1))
    return h.reshape(bsz, s, d)
```

```python
import functools
import math

import jax
import jax.numpy as jnp
from jax import lax
from jax.experimental import pallas as pl
from jax.experimental.pallas import tpu as pltpu

NORM_EPS = 1e-5
ROPE_THETA = 10000.0
LANES = 128
HEAD_DIM = 64
SSD_HEADS = 16
SSD_GROUPS = 4
SSD_STATE = 128
SSD_CONV = 4
SSD_CHUNK = 128
SSD_INNER = SSD_HEADS * HEAD_DIM
SSD_GN = SSD_GROUPS * SSD_STATE
SSD_XBC = SSD_INNER + 2 * SSD_GN
MOBA_HEADS = 8
MOBA_INNER = MOBA_HEADS * HEAD_DIM
MOBA_BLOCK = 256
MOBA_TOPK = 3
FOX_HEADS = 16
FOX_INNER = FOX_HEADS * HEAD_DIM
ATT_TILE = 256
VMEM_LIMIT = 56 * 1024 * 1024

F32 = jnp.float32
BF16 = jnp.bfloat16
NEG_INF = float("-inf")
LOG2E = math.log2(math.e)


def _cparams(sem):
    return pltpu.CompilerParams(dimension_semantics=sem, vmem_limit_bytes=VMEM_LIMIT)


def _resident(shape):
    return pl.BlockSpec(shape, lambda i: (0,) * len(shape), pipeline_mode=pl.Buffered(1))


def _rms(x, g):
    return x * lax.rsqrt(jnp.mean(x * x, axis=-1, keepdims=True) + NORM_EPS) * g


def _silu(x):
    h = 0.5 * x
    return h + h * jnp.tanh(h)


def _softplus(x):
    return jnp.maximum(x, 0.0) + jnp.log1p(jnp.exp(-jnp.abs(x)))


def _split3(x):
    hi = x.astype(BF16)
    r = x - hi.astype(F32)
    mid = r.astype(BF16)
    lo = (r - mid.astype(F32)).astype(BF16)
    return hi, mid, lo


def _dot(a, b):
    return jnp.dot(a, b, preferred_element_type=F32)


def _dot_nt(a, b):
    return lax.dot_general(a, b, (((1,), (1,)), ((), ())), preferred_element_type=F32)


def _dot_tn(a, b):
    return lax.dot_general(a, b, (((0,), (0,)), ((), ())), preferred_element_type=F32)


def _sel_dot_rhs(m01, x):
    hi, mid, lo = _split3(x)
    return _dot(m01, hi) + _dot(m01, mid) + _dot(m01, lo)


def _sel_dot_lhs(x, m01):
    hi, mid, lo = _split3(x)
    return _dot(hi, m01) + _dot(mid, m01) + _dot(lo, m01)


def _norm_matmul_kernel(x_ref, g_ref, w_ref, *out_refs, splits):
    half = x_ref.shape[0] // 2
    for r in range(2):
        rows = slice(r * half, (r + 1) * half)
        xn = _rms(x_ref[rows, :], g_ref[...]).astype(BF16)
        for o_ref, (a, b) in zip(out_refs, splits):
            o_ref[rows, :] = _dot(xn, w_ref[:, a:b]).astype(o_ref.dtype)


def _norm_matmul(x, g, w, widths, tm=512):
    t, d = x.shape
    n = w.shape[1]
    splits, a = [], 0
    for wd in widths:
        splits.append((a, a + wd))
        a += wd
    assert a == n and t % tm == 0
    return pl.pallas_call(
        functools.partial(_norm_matmul_kernel, splits=tuple(splits)),
        grid=(t // tm,),
        in_specs=[pl.BlockSpec((tm, d), lambda i: (i, 0)),
                  _resident((1, d)), _resident((d, n))],
        out_specs=[pl.BlockSpec((tm, wd), lambda i: (i, 0)) for wd in widths],
        out_shape=[jax.ShapeDtypeStruct((t, wd), F32) for wd in widths],
        compiler_params=_cparams(("parallel",)),
        name="norm_matmul",
    )(x, g.reshape(1, d), w)


MLP_CHUNK = 1024


def _proj_mlp_kernel(*refs, n, final_norm):
    a_refs, w_refs = refs[:n], refs[n:2 * n]
    x_ref, g_ref, wu_ref, wd_ref, gf_ref, o_ref = refs[2 * n:]
    proj = _dot(a_refs[0][...], w_refs[0][...])
    for a_ref, w_ref in zip(a_refs[1:], w_refs[1:]):
        proj = proj + _dot(a_ref[...], w_ref[...])
    x = x_ref[...] + proj
    xn = _rms(x, g_ref[...]).astype(BF16)
    acc = x
    for c in range(wu_ref.shape[1] // MLP_CHUNK):
        cols = slice(c * MLP_CHUNK, (c + 1) * MLP_CHUNK)
        h = _dot(xn, wu_ref[:, cols])
        h = jnp.square(jnp.maximum(h, 0.0)).astype(BF16)
        acc = acc + _dot(h, wd_ref[cols, :])
    if final_norm:
        acc = _rms(acc, gf_ref[...])
    o_ref[...] = acc


def _proj_mlp(acts, ws, x, g, w_up, w_down, g_final, final_norm, tm=512):
    t, d = x.shape
    n = len(acts)
    row = lambda width: pl.BlockSpec((tm, width), lambda i: (i, 0))
    const = lambda shape: pl.BlockSpec(shape, lambda i: (0, 0))
    in_specs = ([row(a.shape[1]) for a in acts] + [const(w.shape) for w in ws]
                + [row(d), const((1, d)), _resident(w_up.shape), _resident(w_down.shape), const((1, d))])
    return pl.pallas_call(
        functools.partial(_proj_mlp_kernel, n=n, final_norm=final_norm),
        grid=(t // tm,),
        in_specs=in_specs,
        out_specs=row(d),
        out_shape=jax.ShapeDtypeStruct((t, d), F32),
        compiler_params=_cparams(("parallel",)),
        name="proj_mlp",
    )(*acts, *ws, x, g.reshape(1, d), w_up, w_down, g_final.reshape(1, d))


SSD_STEP = 2 * SSD_CHUNK
CONV_STRIDE = SSD_STEP // 8 + 1
CONV_ROWS = 8 * CONV_STRIDE
SSD_SLABS = SSD_XBC // LANES


def _ssd_kernel(z_ref, xbc_ref, dt_ref, cw_ref, cb_ref, dtb_ref, alog_ref, dskip_ref, gn_ref,
                e_ref, o_ref, ubuf, xa_scr, hst, y_scr):
    L = SSD_CHUNK
    TL = SSD_STEP
    c = pl.program_id(1)

    @pl.when(c == 0)
    def _():
        hst[...] = jnp.zeros_like(hst)
        ubuf[...] = jnp.zeros_like(ubuf)

    for sl in range(SSD_SLABS):
        ubuf[sl, 8:8 + TL, :] = xbc_ref[:, sl * LANES:(sl + 1) * LANES]

    def conv_slab(sl, carry):
        w = cw_ref[sl]
        b = cb_ref[sl]
        for r in range(CONV_STRIDE):
            acc = b
            for k in range(SSD_CONV):
                acc = acc + w[k:k + 1, :] * ubuf[sl, pl.ds(5 + k + r, 8, stride=CONV_STRIDE), :]
            xa_scr[sl, pl.ds(r, 8, stride=CONV_STRIDE), :] = acc + acc * jnp.tanh(acc)
        ubuf[sl, 0:8, :] = ubuf[sl, TL:TL + 8, :]
        return carry

    lax.fori_loop(0, SSD_SLABS, conv_slab, 0)
    nxs = SSD_INNER // LANES
    ngs = SSD_GN // LANES
    row = lax.broadcasted_iota(jnp.int32, (L, L), 0)
    col = lax.broadcasted_iota(jnp.int32, (L, L), 1)
    causal = row >= col
    tril = causal.astype(BF16)
    e01 = e_ref[...]
    lane = lax.broadcasted_iota(jnp.int32, (L, LANES), 1)
    neg_a = -jnp.exp(alog_ref[...])
    gw = SSD_INNER // SSD_GROUPS
    hpl = LANES // HEAD_DIM

    for ci in range(TL // L):
        rows = slice(ci * L, (ci + 1) * L)
        xs = jnp.concatenate([xa_scr[sl, rows, :] for sl in range(nxs)], axis=1)
        dt = _softplus(dt_ref[rows, :] + dtb_ref[...])
        acum = _sel_dot_rhs(tril, dt * neg_a)
        alast = acum[L - 1:L, :]
        dtx = _dot(dt.astype(BF16), e01)
        eacx = _dot(jnp.exp(acum).astype(BF16), e01)
        dtex = _dot(jnp.exp(alast - acum).astype(BF16), e01)
        cdx = _sel_dot_lhs(jnp.exp(acum[L - 8:L, :]), e01)[7:8, :]
        xc = xs * dtx
        xcd = (xc * dtex).astype(BF16)
        act = acum.T

        cbs, y_offs = [], []
        for g in range(SSD_GROUPS):
            b_f32 = xa_scr[nxs + g, rows, :]
            bg = b_f32.astype(BF16)
            bgt = b_f32.T.astype(BF16)
            cg = xa_scr[nxs + ngs + g, rows, :].astype(BF16)
            cbs.append(_dot_nt(cg, bg))
            hprev = hst[g]
            y_offs.append(_dot(cg, hprev.astype(BF16)) * eacx[:, g * gw:(g + 1) * gw])
            hst[g] = hprev * cdx[:, g * gw:(g + 1) * gw] + _dot(bgt, xcd[:, g * gw:(g + 1) * gw])
        ms = []
        for h in range(SSD_HEADS):
            dec = jnp.exp(jnp.where(causal, acum[:, h:h + 1] - act[h:h + 1, :], NEG_INF))
            ms.append((cbs[h * HEAD_DIM // gw] * dec).astype(BF16))
        for pr in range(SSD_INNER // LANES):
            lo = pr * LANES
            xcp = xc[:, lo:lo + LANES]
            xm = [jnp.where((lane >= hh * HEAD_DIM) & (lane < (hh + 1) * HEAD_DIM), xcp, 0.0).astype(BF16)
                  for hh in range(hpl)]
            y_diag = _dot(jnp.concatenate(ms[pr * hpl:(pr + 1) * hpl], axis=1), jnp.concatenate(xm, axis=0))
            g, off = divmod(lo, gw)
            y_scr[rows, lo:lo + LANES] = y_diag + y_offs[g][:, off:off + LANES]

        y = y_scr[rows, :] + xs * dskip_ref[...]
        z = z_ref[rows, :]
        y = y * _silu(z)
        for g in range(SSD_GROUPS):
            yg = y[:, g * gw:(g + 1) * gw]
            ms = jnp.mean(yg * yg, axis=-1, keepdims=True)
            o_ref[rows, g * gw:(g + 1) * gw] = (
                yg * lax.rsqrt(ms + NORM_EPS) * gn_ref[:, g * gw:(g + 1) * gw]).astype(o_ref.dtype)


def _pad_lanes(v):
    return jnp.pad(v.astype(F32), (0, LANES - v.shape[0])).reshape(1, LANES)


def _ssd(z, xbc, dt, conv_w, conv_b, dt_bias, a_log, d_skip, gate_norm, bsz, s):
    L = SSD_STEP
    nc = s // L
    e01 = (jnp.arange(LANES)[:, None] == (jnp.arange(SSD_INNER) // HEAD_DIM)[None, :]).astype(BF16)
    dsx = jnp.repeat(d_skip.astype(F32), HEAD_DIM).reshape(1, SSD_INNER)
    cw = (0.5 * conv_w.astype(F32)).reshape(SSD_CONV, SSD_SLABS, LANES).transpose(1, 0, 2)
    cw = jnp.pad(cw, ((0, 0), (0, 8 - SSD_CONV), (0, 0)))
    cb = (0.5 * conv_b.astype(F32)).reshape(SSD_SLABS, 1, LANES)
    tok = lambda b, c: (b * nc + c, 0)
    const = lambda b, c: (0, 0)
    const3 = lambda b, c: (0, 0, 0)
    return pl.pallas_call(
        _ssd_kernel,
        grid=(bsz, nc),
        in_specs=[pl.BlockSpec((L, SSD_INNER), tok),
                  pl.BlockSpec((L, SSD_XBC), tok),
                  pl.BlockSpec((L, LANES), tok),
                  pl.BlockSpec((SSD_SLABS, 8, LANES), const3),
                  pl.BlockSpec((SSD_SLABS, 1, LANES), const3),
                  pl.BlockSpec((1, LANES), const),
                  pl.BlockSpec((1, LANES), const),
                  pl.BlockSpec((1, SSD_INNER), const),
                  pl.BlockSpec((1, SSD_INNER), const),
                  pl.BlockSpec((LANES, SSD_INNER), const)],
        out_specs=pl.BlockSpec((L, SSD_INNER), tok),
        out_shape=jax.ShapeDtypeStruct((bsz * s, SSD_INNER), BF16),
        scratch_shapes=[pltpu.VMEM((SSD_SLABS, 8 + CONV_ROWS + 8, LANES), F32),
                        pltpu.VMEM((SSD_SLABS, CONV_ROWS, LANES), F32),
                        pltpu.VMEM((SSD_GROUPS, SSD_STATE, SSD_INNER // SSD_GROUPS), F32),
                        pltpu.VMEM((L, SSD_INNER), F32)],
        compiler_params=_cparams(("parallel", "arbitrary")),
        name="ssd",
    )(z, xbc, dt, cw, cb, _pad_lanes(dt_bias), _pad_lanes(a_log), dsx,
      gate_norm.astype(F32).reshape(1, SSD_INNER), e01)


def _rope(x, cos, sin_lo, sin_hi):
    half = HEAD_DIM // 2
    return x * cos + pltpu.roll(x, LANES - half, 1) * sin_lo + pltpu.roll(x, half, 1) * sin_hi


def _fold8(x, op):
    parts = [x[r * 8:(r + 1) * 8, :] for r in range(x.shape[0] // 8)]
    return _tree(parts, op)


def _tree(parts, op):
    while len(parts) > 1:
        parts = [op(parts[a], parts[a + 1]) if a + 1 < len(parts) else parts[a]
                 for a in range(0, len(parts), 2)]
    return parts[0]


def _attend(i, nh, score_fn, row_shift, vt_scr, o_ref, o_rows, st_scr, m_scr, acc_scr, ot_scr):
    tq = ATT_TILE
    krow = lax.broadcasted_iota(jnp.int32, (tq, tq), 0)
    qcol = lax.broadcasted_iota(jnp.int32, (tq, tq), 1)
    m_scr[...] = jnp.full(m_scr.shape, NEG_INF, F32)
    acc_scr[...] = jnp.zeros(acc_scr.shape, F32)

    def trip(js, diagonal_last):
        shifts, alphas = [], []
        for h in range(nh):
            folds = []
            for u, j in enumerate(js):
                s_t = score_fn(h, j)
                if diagonal_last and u == len(js) - 1:
                    s_t = jnp.where(krow <= qcol, s_t, NEG_INF)
                st_scr[h, u] = s_t
                folds.append(_fold8(s_t, jnp.maximum))
            m_trip = jnp.max(_tree(folds, jnp.maximum), axis=0, keepdims=True)
            rs = row_shift[h]
            m_old = m_scr[h]
            m_new = jnp.maximum(m_old, m_trip if rs is None else m_trip + rs)
            m_scr[h] = m_new
            shifts.append(m_new if rs is None else m_new - rs)
            alphas.append(jnp.exp2(m_old - m_new))
        for h in range(nh):
            pv = [_dot(vt_scr[j, h], jnp.exp2(st_scr[h, u] - shifts[h]).astype(BF16)) for u, j in enumerate(js)]
            acc_scr[h] = alphas[h] * acc_scr[h] + _tree(pv, jnp.add)

    group = st_scr.shape[1]
    rem = i % group
    for r in range(group):
        @pl.when(rem == r)
        def _(r=r):
            trip([i - r + u for u in range(r + 1)], True)

    def full_trip(g, carry):
        trip([group * g + u for u in range(group)], False)
        return carry

    lax.fori_loop(0, i // group, full_trip, 0)
    for h in range(nh):
        ot_scr[h * HEAD_DIM:(h + 1) * HEAD_DIM, :] = acc_scr[h, :HEAD_DIM, :] / acc_scr[h, HEAD_DIM:HEAD_DIM + 1, :]
    for g in range(nh * HEAD_DIM // LANES):
        o_ref[o_rows, g * LANES:(g + 1) * LANES] = ot_scr[g * LANES:(g + 1) * LANES, :].T.astype(o_ref.dtype)


V_ROWS = HEAD_DIM + 16
ATT_GROUPS = 2
ATT_LANES = ATT_GROUPS * LANES
ATT_NH = ATT_LANES // HEAD_DIM
ATT_TPS = 4
MOBA_TRIP = 8
FOX_TRIP = 8


def _store_vt(vt_scr, jb, v_blk):
    n = v_blk.shape[1]
    eye = (lax.broadcasted_iota(jnp.int32, (n, n), 0) == lax.broadcasted_iota(jnp.int32, (n, n), 1)).astype(BF16)
    vt = _dot_nt(eye, v_blk.astype(BF16))
    for h in range(v_blk.shape[1] // HEAD_DIM):
        vt_scr[jb, h, :HEAD_DIM, :] = vt[h * HEAD_DIM:(h + 1) * HEAD_DIM, :].astype(BF16)
        vt_scr[jb, h, HEAD_DIM:, :] = jnp.ones((V_ROWS - HEAD_DIM, v_blk.shape[0]), BF16)


def _attend_scratch(nb, group):
    tq = ATT_TILE
    return [pltpu.VMEM((nb, ATT_NH, V_ROWS, tq), BF16),
            pltpu.VMEM((ATT_NH, group, tq, tq), F32),
            pltpu.VMEM((ATT_NH, 1, tq), F32),
            pltpu.VMEM((ATT_NH, V_ROWS, tq), F32),
            pltpu.VMEM((ATT_LANES, tq), F32)]


def _masked_heads(x, scale):
    lane = lax.broadcasted_iota(jnp.int32, (x.shape[0], LANES), 1)
    out = []
    for h in range(x.shape[1] // HEAD_DIM):
        g, hh = divmod(h, LANES // HEAD_DIM)
        xg = x[:, g * LANES:(g + 1) * LANES]
        in_head = (lane >= hh * HEAD_DIM) & (lane < (hh + 1) * HEAD_DIM)
        out.append(jnp.where(in_head, xg * scale, 0.0))
    return out


def _moba_kernel(q_ref, k_ref, v_ref, cos_ref, sinl_ref, sinh_ref, o_ref, kr_scr, km_scr, bias_scr,
                 vt_scr, st_scr, m_scr, acc_scr, ot_scr):
    tq = ATT_TILE
    nb = kr_scr.shape[1]
    step = pl.program_id(2)
    scale = HEAD_DIM ** -0.5
    hpg = LANES // HEAD_DIM

    @pl.when(step == 0)
    def _():
        for jb in range(nb):
            rows = slice(jb * tq, (jb + 1) * tq)
            for g in range(ATT_GROUPS):
                lanes = slice(g * LANES, (g + 1) * LANES)
                kr = _rope(k_ref[rows, lanes], cos_ref[rows, :], sinl_ref[rows, :], sinh_ref[rows, :])
                kr_scr[g, jb] = kr.astype(BF16)
                km_scr[g, jb:jb + 1, :] = jnp.mean(kr, axis=0, keepdims=True)
            _store_vt(vt_scr, jb, v_ref[rows, :])

    def tile(t, carry):
        i = step * ATT_TPS + t
        trows = pl.ds(pl.multiple_of(t * tq, tq), tq)
        qrows = pl.ds(pl.multiple_of(i * tq, tq), tq)
        cos, sin_lo, sin_hi = cos_ref[qrows, :], sinl_ref[qrows, :], sinh_ref[qrows, :]
        qgs = [_rope(q_ref[trows, g * LANES:(g + 1) * LANES], cos, sin_lo, sin_hi) for g in range(ATT_GROUPS)]
        klane = lax.broadcasted_iota(jnp.int32, (nb, LANES), 1)
        gates = []
        for g in range(ATT_GROUPS):
            q_hi = qgs[g].astype(BF16)
            q_lo = (qgs[g] - q_hi.astype(F32)).astype(BF16)
            for hh in range(hpg):
                km = jnp.where((klane >= hh * HEAD_DIM) & (klane < (hh + 1) * HEAD_DIM), km_scr[g], 0.0)
                k_hi = km.astype(BF16)
                k_lo = (km - k_hi.astype(F32)).astype(BF16)
                gates.append(_dot_nt(k_hi, q_hi) + _dot_nt(k_hi, q_lo) + _dot_nt(k_lo, q_hi))
        jrow = lax.broadcasted_iota(jnp.int32, (nb, ATT_NH * tq), 0)
        valid = jrow < i
        gate = jnp.where(valid, jnp.concatenate(gates, axis=1), NEG_INF)
        cnt = jnp.zeros(gate.shape, F32)
        for jp in range(nb):
            gj = gate[jp:jp + 1, :]
            cnt = cnt + jnp.where((gj > gate) | ((gj == gate) & (jp < jrow)), 1.0, 0.0)
        chosen = valid & (cnt < MOBA_TOPK)
        bias = jnp.where(chosen | (jrow == i), 0.0, NEG_INF)
        for h in range(ATT_NH):
            bias_scr[h] = bias[:, h * tq:(h + 1) * tq]
        qs = [(qm * (scale * LOG2E)).astype(BF16) for qm in _masked_heads(jnp.concatenate(qgs, axis=1), 1.0)]

        def score_fn(h, j):
            return _dot_nt(kr_scr[h // hpg, j], qs[h]) + bias_scr[h, pl.ds(j, 1), :]

        _attend(i, ATT_NH, score_fn, [None] * ATT_NH, vt_scr, o_ref, trows, st_scr, m_scr, acc_scr, ot_scr)
        return carry

    lax.fori_loop(0, ATT_TPS, tile, 0)


def _moba(q, k, v, cos, sin_lo, sin_hi, bsz, s):
    tq = ATT_TILE
    nb = s // tq
    q3, k3, v3 = (a.reshape(bsz, s, MOBA_INNER) for a in (q, k, v))
    whole = pl.BlockSpec((None, s, ATT_LANES), lambda b, p, i: (b, 0, p))
    tile = pl.BlockSpec((None, ATT_TPS * tq, ATT_LANES), lambda b, p, i: (b, i, p))
    tab = pl.BlockSpec((s, LANES), lambda b, p, i: (0, 0))
    out = pl.pallas_call(
        _moba_kernel,
        grid=(bsz, MOBA_INNER // ATT_LANES, nb // ATT_TPS),
        in_specs=[tile, whole, whole, tab, tab, tab],
        out_specs=tile,
        out_shape=jax.ShapeDtypeStruct((bsz, s, MOBA_INNER), BF16),
        scratch_shapes=[pltpu.VMEM((ATT_GROUPS, nb, tq, LANES), BF16),
                        pltpu.VMEM((ATT_GROUPS, nb, LANES), F32),
                        pltpu.VMEM((ATT_NH, nb, tq), F32)] + _attend_scratch(nb, MOBA_TRIP),
        compiler_params=_cparams(("parallel", "parallel", "arbitrary")),
        name="moba",
    )(q3, k3, v3, cos, sin_lo, sin_hi)
    return out.reshape(bsz * s, MOBA_INNER)


def _fox_gate_kernel(f_ref, b_ref, c_ref, ct_ref):
    tb = ATT_TILE
    s = f_ref.shape[0]
    row = lax.broadcasted_iota(jnp.int32, (tb, tb), 0)
    col = lax.broadcasted_iota(jnp.int32, (tb, tb), 1)
    tril = (row >= col).astype(BF16)
    carry = jnp.zeros((1, LANES), F32)
    for jb in range(s // tb):
        rows = slice(jb * tb, (jb + 1) * tb)
        log_f = -_softplus(-(f_ref[rows, :] + b_ref[...]))
        cblk = _sel_dot_rhs(tril, log_f) + carry
        c_ref[rows, :] = cblk
        ct_ref[jb] = cblk.T
        carry = cblk[tb - 1:tb, :]


def _fox_gate(f, bias, bsz, s):
    return pl.pallas_call(
        _fox_gate_kernel,
        grid=(bsz,),
        in_specs=[pl.BlockSpec((None, s, LANES), lambda b: (b, 0, 0)),
                  pl.BlockSpec((1, LANES), lambda b: (0, 0))],
        out_specs=[pl.BlockSpec((None, s, LANES), lambda b: (b, 0, 0)),
                   pl.BlockSpec((None, s // ATT_TILE, LANES, ATT_TILE), lambda b: (b, 0, 0, 0))],
        out_shape=[jax.ShapeDtypeStruct((bsz, s, LANES), F32),
                   jax.ShapeDtypeStruct((bsz, s // ATT_TILE, LANES, ATT_TILE), F32)],
        compiler_params=_cparams(("parallel",)),
        name="fox_gate",
    )(f.reshape(bsz, s, LANES), _pad_lanes(bias))


def _fox_kernel(q_ref, k_ref, v_ref, c_ref, ct_ref, o_ref, k_scr, ck_scr,
                vt_scr, st_scr, m_scr, acc_scr, ot_scr):
    tq = ATT_TILE
    nb = k_scr.shape[1]
    p = pl.program_id(1)
    step = pl.program_id(2)
    scale = HEAD_DIM ** -0.5
    hpg = LANES // HEAD_DIM

    @pl.when(step == 0)
    def _():
        lane = lax.broadcasted_iota(jnp.int32, (tq, LANES), 1)
        for jb in range(nb):
            rows = slice(jb * tq, (jb + 1) * tq)
            for g in range(ATT_GROUPS):
                k_scr[g, jb] = k_ref[rows, g * LANES:(g + 1) * LANES].astype(BF16)
            _store_vt(vt_scr, jb, v_ref[rows, :])
            cblk = c_ref[rows, :] * LOG2E
            for h in range(ATT_NH):
                colv = jnp.sum(jnp.where(lane == p * ATT_NH + h, cblk, 0.0), axis=1, keepdims=True)
                ck_scr[h, jb] = jnp.broadcast_to(colv, (tq, LANES))

    def tile(t, carry):
        i = step * ATT_TPS + t
        trows = pl.ds(pl.multiple_of(t * tq, tq), tq)
        qs = [qm.astype(BF16) for qm in _masked_heads(q_ref[trows, :], scale * LOG2E)]
        cqs = [ct_ref[t, pl.ds(p * ATT_NH + h, 1), :] * LOG2E for h in range(ATT_NH)]

        def score_fn(h, j):
            ck = ck_scr[h, j]
            return _dot_nt(k_scr[h // hpg, j], qs[h]) - jnp.concatenate([ck] * (tq // LANES), axis=1)

        _attend(i, ATT_NH, score_fn, cqs, vt_scr, o_ref, trows, st_scr, m_scr, acc_scr, ot_scr)
        return carry

    lax.fori_loop(0, ATT_TPS, tile, 0)


def _fox(q, k, v, c, ct, bsz, s):
    tq = ATT_TILE
    nb = s // tq
    q3, k3, v3 = (a.reshape(bsz, s, FOX_INNER) for a in (q, k, v))
    whole = pl.BlockSpec((None, s, ATT_LANES), lambda b, p, i: (b, 0, p))
    tile = pl.BlockSpec((None, ATT_TPS * tq, ATT_LANES), lambda b, p, i: (b, i, p))
    out = pl.pallas_call(
        _fox_kernel,
        grid=(bsz, FOX_INNER // ATT_LANES, nb // ATT_TPS),
        in_specs=[tile, whole, whole,
                  pl.BlockSpec((None, s, LANES), lambda b, p, i: (b, 0, 0)),
                  pl.BlockSpec((None, ATT_TPS, LANES, tq), lambda b, p, i: (b, i, 0, 0))],
        out_specs=tile,
        out_shape=jax.ShapeDtypeStruct((bsz, s, FOX_INNER), BF16),
        scratch_shapes=[pltpu.VMEM((ATT_GROUPS, nb, tq, LANES), BF16),
                        pltpu.VMEM((ATT_NH, nb, tq, LANES), F32)] + _attend_scratch(nb, FOX_TRIP),
        compiler_params=_cparams(("parallel", "parallel", "arbitrary")),
        name="fox",
    )(q3, k3, v3, c, ct)
    return out.reshape(bsz * s, FOX_INNER)


def _rope_tables(s):
    half = HEAD_DIM // 2
    inv = jnp.power(ROPE_THETA, -jnp.arange(half, dtype=F32) / half)
    ang = jnp.arange(s, dtype=F32)[:, None] * inv[None, :]
    cos, sin = jnp.cos(ang), jnp.sin(ang)
    reps = LANES // HEAD_DIM
    zero = jnp.zeros_like(sin)
    return (jnp.tile(jnp.concatenate([cos, cos], axis=1), (1, reps)),
            jnp.tile(jnp.concatenate([-sin, zero], axis=1), (1, reps)),
            jnp.tile(jnp.concatenate([zero, sin], axis=1), (1, reps)))


def _even_layer(x, bsz, s, g, w_in, conv_w, conv_b, dt_bias, a_log, d_skip, gate_norm, w_out, rope_tabs):
    d = x.shape[1]
    o1 = SSD_INNER
    o2 = o1 + SSD_XBC
    o3 = o2 + SSD_HEADS
    w_in = w_in.astype(BF16)
    w = jnp.concatenate([w_in[:, :o2], w_in[:, o3:], w_in[:, o2:o3],
                         jnp.zeros((d, LANES - SSD_HEADS), BF16)], axis=1)
    z, xbc, q, k, v, dt = _norm_matmul(
        x, g, w, (SSD_INNER, SSD_XBC, MOBA_INNER, MOBA_INNER, MOBA_INNER, LANES))
    y_ssd = _ssd(z, xbc, dt, conv_w, conv_b, dt_bias, a_log, d_skip, gate_norm, bsz, s)
    y_att = _moba(q, k, v, *rope_tabs, bsz, s)
    w_o = w_out.astype(BF16)
    return [y_ssd, y_att], [w_o[:SSD_INNER], w_o[SSD_INNER:]]


def _odd_layer(x, bsz, s, g, w_in, fgate_bias, w_out):
    d = x.shape[1]
    w = jnp.concatenate([w_in.astype(BF16), jnp.zeros((d, LANES - FOX_HEADS), BF16)], axis=1)
    q, k, v, f = _norm_matmul(x, g, w, (FOX_INNER, FOX_INNER, FOX_INNER, LANES))
    c, ct = _fox_gate(f, fgate_bias, bsz, s)
    y = _fox(q, k, v, c, ct, bsz, s)
    return [y], [w_out.astype(BF16)]


def kernel(x, norm_mix_even, w_in_even, conv_w, conv_b, dt_bias, a_log, d_skip, ssd_gate_norm, w_out_even,
           norm_mix_odd, w_in_odd, fgate_bias, w_out_odd, norm_mlp, w_up, w_down, final_norm):
    bsz, s, d = x.shape
    depth = norm_mlp.shape[0]
    assert s % (ATT_TILE * ATT_TPS) == 0 and s % SSD_STEP == 0
    h = x.reshape(bsz * s, d)
    rope_tabs = _rope_tables(s)
    for layer in range(depth):
        i = layer // 2
        if layer % 2 == 0:
            acts, ws = _even_layer(h, bsz, s, norm_mix_even[i], w_in_even[i], conv_w[i], conv_b[i], dt_bias[i],
                            a_log[i], d_skip[i], ssd_gate_norm[i], w_out_even[i], rope_tabs)
        else:
            acts, ws = _odd_layer(h, bsz, s, norm_mix_odd[i], w_in_odd[i], fgate_bias[i], w_out_odd[i])
        h = _proj_mlp(acts, ws, h, norm_mlp[layer], w_up[layer].astype(BF16), w_down[layer].astype(BF16),
                      final_norm, final_norm=(layer == depth - 1))
    return h.reshape(bsz, s, d)
```

```python
import functools
import math

import jax
import jax.numpy as jnp
from jax import lax
from jax.experimental import pallas as pl
from jax.experimental.pallas import tpu as pltpu

NORM_EPS = 1e-5
ROPE_THETA = 10000.0
LANES = 128
HEAD_DIM = 64
SSD_HEADS = 16
SSD_GROUPS = 4
SSD_STATE = 128
SSD_CONV = 4
SSD_CHUNK = 128
SSD_INNER = SSD_HEADS * HEAD_DIM
SSD_GN = SSD_GROUPS * SSD_STATE
SSD_XBC = SSD_INNER + 2 * SSD_GN
MOBA_HEADS = 8
MOBA_INNER = MOBA_HEADS * HEAD_DIM
MOBA_BLOCK = 256
MOBA_TOPK = 3
FOX_HEADS = 16
FOX_INNER = FOX_HEADS * HEAD_DIM
ATT_TILE = MOBA_BLOCK
VMEM_LIMIT = 56 * 1024 * 1024

F32 = jnp.float32
BF16 = jnp.bfloat16
NEG_INF = float("-inf")
LOG2E = math.log2(math.e)


def _cparams(sem):
    return pltpu.CompilerParams(dimension_semantics=sem, vmem_limit_bytes=VMEM_LIMIT)


def _resident(shape):
    return pl.BlockSpec(shape, lambda i: (0,) * len(shape), pipeline_mode=pl.Buffered(1))


def _rms(x, g):
    return x * lax.rsqrt(jnp.mean(x * x, axis=-1, keepdims=True) + NORM_EPS) * g


def _silu(x):
    h = 0.5 * x
    return h + h * jnp.tanh(h)


def _softplus(x):
    return jnp.maximum(x, 0.0) + jnp.log1p(jnp.exp(-jnp.abs(x)))


def _split3(x):
    hi = x.astype(BF16)
    r = x - hi.astype(F32)
    mid = r.astype(BF16)
    lo = (r - mid.astype(F32)).astype(BF16)
    return hi, mid, lo


def _dot(a, b):
    return jnp.dot(a, b, preferred_element_type=F32)


def _dot_nt(a, b):
    return lax.dot_general(a, b, (((1,), (1,)), ((), ())), preferred_element_type=F32)


def _sel_dot_rhs(m01, x):
    hi, mid, lo = _split3(x)
    return _dot(m01, hi) + _dot(m01, mid) + _dot(m01, lo)


def _sel_dot_lhs(x, m01):
    hi, mid, lo = _split3(x)
    return _dot(hi, m01) + _dot(mid, m01) + _dot(lo, m01)


def _norm_matmul_kernel(x_ref, g_ref, w_ref, *out_refs, splits):
    half = x_ref.shape[0] // 2
    for r in range(2):
        rows = slice(r * half, (r + 1) * half)
        xn = _rms(x_ref[rows, :], g_ref[...]).astype(BF16)
        for o_ref, (a, b) in zip(out_refs, splits):
            o_ref[rows, :] = _dot(xn, w_ref[:, a:b]).astype(o_ref.dtype)


def _norm_matmul(x, g, w, widths, tm=512):
    t, d = x.shape
    n = w.shape[1]
    splits, a = [], 0
    for wd in widths:
        splits.append((a, a + wd))
        a += wd
    assert a == n and t % tm == 0
    return pl.pallas_call(
        functools.partial(_norm_matmul_kernel, splits=tuple(splits)),
        grid=(t // tm,),
        in_specs=[pl.BlockSpec((tm, d), lambda i: (i, 0)),
                  _resident((1, d)), _resident((d, n))],
        out_specs=[pl.BlockSpec((tm, wd), lambda i: (i, 0)) for wd in widths],
        out_shape=[jax.ShapeDtypeStruct((t, wd), F32) for wd in widths],
        compiler_params=_cparams(("parallel",)),
        name="norm_matmul",
    )(x, g.reshape(1, d), w)


MLP_CHUNK = 1024


def _proj_mlp_kernel(*refs, n, final_norm):
    a_refs, w_refs = refs[:n], refs[n:2 * n]
    x_ref, g_ref, wu_ref, wd_ref, gf_ref, o_ref = refs[2 * n:]
    proj = _dot(a_refs[0][...], w_refs[0][...])
    for a_ref, w_ref in zip(a_refs[1:], w_refs[1:]):
        proj = proj + _dot(a_ref[...], w_ref[...])
    x = x_ref[...] + proj
    xn = _rms(x, g_ref[...]).astype(BF16)
    acc = x
    for c in range(wu_ref.shape[1] // MLP_CHUNK):
        cols = slice(c * MLP_CHUNK, (c + 1) * MLP_CHUNK)
        h = _dot(xn, wu_ref[:, cols])
        h = jnp.square(jnp.maximum(h, 0.0)).astype(BF16)
        acc = acc + _dot(h, wd_ref[cols, :])
    if final_norm:
        acc = _rms(acc, gf_ref[...])
    o_ref[...] = acc


def _proj_mlp(acts, ws, x, g, w_up, w_down, g_final, final_norm, tm=512):
    t, d = x.shape
    n = len(acts)
    row = lambda width: pl.BlockSpec((tm, width), lambda i: (i, 0))
    const = lambda shape: pl.BlockSpec(shape, lambda i: (0, 0))
    in_specs = ([row(a.shape[1]) for a in acts] + [const(w.shape) for w in ws]
                + [row(d), const((1, d)), _resident(w_up.shape), _resident(w_down.shape), const((1, d))])
    return pl.pallas_call(
        functools.partial(_proj_mlp_kernel, n=n, final_norm=final_norm),
        grid=(t // tm,),
        in_specs=in_specs,
        out_specs=row(d),
        out_shape=jax.ShapeDtypeStruct((t, d), F32),
        compiler_params=_cparams(("parallel",)),
        name="proj_mlp",
    )(*acts, *ws, x, g.reshape(1, d), w_up, w_down, g_final.reshape(1, d))


SSD_STEP = 2 * SSD_CHUNK
CONV_STRIDE = SSD_STEP // 8 + 1
CONV_ROWS = 8 * CONV_STRIDE
SSD_SLABS = SSD_XBC // LANES


def _ssd_kernel(z_ref, xbc_ref, dt_ref, cw_ref, cb_ref, dtb_ref, alog_ref, dskip_ref, gn_ref,
                e_ref, o_ref, ubuf, xa_scr, hst):
    L = SSD_CHUNK
    TL = SSD_STEP
    c = pl.program_id(1)

    @pl.when(c == 0)
    def _():
        hst[...] = jnp.zeros_like(hst)
        ubuf[...] = jnp.zeros_like(ubuf)

    for sl in range(SSD_SLABS):
        ubuf[sl, 8:8 + TL, :] = xbc_ref[:, sl * LANES:(sl + 1) * LANES]

    def conv_slab(sl, carry):
        w = cw_ref[sl]
        b = cb_ref[sl]
        u = [ubuf[sl, pl.ds(5 + m, 8, stride=CONV_STRIDE), :] for m in range(CONV_STRIDE + SSD_CONV - 1)]
        for r in range(CONV_STRIDE):
            acc = b
            for k in range(SSD_CONV):
                acc = acc + w[k:k + 1, :] * u[r + k]
            xa_scr[sl, pl.ds(r, 8, stride=CONV_STRIDE), :] = acc + acc * jnp.tanh(acc)
        ubuf[sl, 0:8, :] = ubuf[sl, TL:TL + 8, :]
        return carry

    lax.fori_loop(0, SSD_SLABS, conv_slab, 0)
    nxs = SSD_INNER // LANES
    ngs = SSD_GN // LANES
    row = lax.broadcasted_iota(jnp.int32, (L, L), 0)
    col = lax.broadcasted_iota(jnp.int32, (L, L), 1)
    causal = row >= col
    tril = causal.astype(BF16)
    e01 = e_ref[...]
    lane = lax.broadcasted_iota(jnp.int32, (L, LANES), 1)
    neg_a = -jnp.exp(alog_ref[...])
    gw = SSD_INNER // SSD_GROUPS
    hpl = LANES // HEAD_DIM

    for ci in range(TL // L):
        rows = slice(ci * L, (ci + 1) * L)
        xs = jnp.concatenate([xa_scr[sl, rows, :] for sl in range(nxs)], axis=1)
        dt = _softplus(dt_ref[rows, :] + dtb_ref[...])
        acum = _sel_dot_rhs(tril, dt * neg_a)
        alast = acum[L - 1:L, :]
        dtx = _dot(dt.astype(BF16), e01)
        eacx = _dot(jnp.exp(acum).astype(BF16), e01)
        dtex = _dot(jnp.exp(alast - acum).astype(BF16), e01)
        cdx = _sel_dot_lhs(jnp.exp(acum[L - 8:L, :]), e01)[7:8, :]
        xc = xs * dtx
        xcd = (xc * dtex).astype(BF16)
        act = acum.T

        cbs, y_offs = [], []
        for g in range(SSD_GROUPS):
            b_f32 = xa_scr[nxs + g, rows, :]
            bg = b_f32.astype(BF16)
            bgt = b_f32.T.astype(BF16)
            cg = xa_scr[nxs + ngs + g, rows, :].astype(BF16)
            cbs.append(_dot_nt(cg, bg))
            hprev = hst[g]
            y_offs.append(_dot(cg, hprev.astype(BF16)) * eacx[:, g * gw:(g + 1) * gw])
            hst[g] = hprev * cdx[:, g * gw:(g + 1) * gw] + _dot(bgt, xcd[:, g * gw:(g + 1) * gw])
        ms = []
        for h in range(SSD_HEADS):
            dec = jnp.exp(jnp.where(causal, acum[:, h:h + 1] - act[h:h + 1, :], NEG_INF))
            ms.append((cbs[h * HEAD_DIM // gw] * dec).astype(BF16))
        y_diags = []
        for pr in range(SSD_INNER // LANES):
            xcp = xc[:, pr * LANES:(pr + 1) * LANES]
            xm = [jnp.where((lane >= hh * HEAD_DIM) & (lane < (hh + 1) * HEAD_DIM), xcp, 0.0).astype(BF16)
                  for hh in range(hpl)]
            y_diags.append(_dot(jnp.concatenate(ms[pr * hpl:(pr + 1) * hpl], axis=1), jnp.concatenate(xm, axis=0)))

        ppg = gw // LANES
        for g in range(SSD_GROUPS):
            lanes = slice(g * gw, (g + 1) * gw)
            y = jnp.concatenate(y_diags[g * ppg:(g + 1) * ppg], axis=1) + y_offs[g] + xs[:, lanes] * dskip_ref[:, lanes]
            yg = y * _silu(z_ref[rows, lanes])
            ms_g = jnp.mean(yg * yg, axis=-1, keepdims=True)
            o_ref[rows, lanes] = (yg * lax.rsqrt(ms_g + NORM_EPS) * gn_ref[:, lanes]).astype(o_ref.dtype)


def _pad_lanes(v):
    return jnp.pad(v.astype(F32), (0, LANES - v.shape[0])).reshape(1, LANES)


def _ssd(z, xbc, dt, conv_w, conv_b, dt_bias, a_log, d_skip, gate_norm, bsz, s):
    L = SSD_STEP
    nc = s // L
    e01 = (jnp.arange(LANES)[:, None] == (jnp.arange(SSD_INNER) // HEAD_DIM)[None, :]).astype(BF16)
    dsx = jnp.repeat(d_skip.astype(F32), HEAD_DIM).reshape(1, SSD_INNER)
    cw = (0.5 * conv_w.astype(F32)).reshape(SSD_CONV, SSD_SLABS, LANES).transpose(1, 0, 2)
    cw = jnp.pad(cw, ((0, 0), (0, 8 - SSD_CONV), (0, 0)))
    cb = (0.5 * conv_b.astype(F32)).reshape(SSD_SLABS, 1, LANES)
    tok = lambda b, c: (b * nc + c, 0)
    const = lambda b, c: (0, 0)
    const3 = lambda b, c: (0, 0, 0)
    return pl.pallas_call(
        _ssd_kernel,
        grid=(bsz, nc),
        in_specs=[pl.BlockSpec((L, SSD_INNER), tok),
                  pl.BlockSpec((L, SSD_XBC), tok),
                  pl.BlockSpec((L, LANES), tok),
                  pl.BlockSpec((SSD_SLABS, 8, LANES), const3),
                  pl.BlockSpec((SSD_SLABS, 1, LANES), const3),
                  pl.BlockSpec((1, LANES), const),
                  pl.BlockSpec((1, LANES), const),
                  pl.BlockSpec((1, SSD_INNER), const),
                  pl.BlockSpec((1, SSD_INNER), const),
                  pl.BlockSpec((LANES, SSD_INNER), const)],
        out_specs=pl.BlockSpec((L, SSD_INNER), tok),
        out_shape=jax.ShapeDtypeStruct((bsz * s, SSD_INNER), BF16),
        scratch_shapes=[pltpu.VMEM((SSD_SLABS, 8 + CONV_ROWS + 8, LANES), F32),
                        pltpu.VMEM((SSD_SLABS, CONV_ROWS, LANES), F32),
                        pltpu.VMEM((SSD_GROUPS, SSD_STATE, SSD_INNER // SSD_GROUPS), F32)],
        compiler_params=_cparams(("parallel", "arbitrary")),
        name="ssd",
    )(z, xbc, dt, cw, cb, _pad_lanes(dt_bias), _pad_lanes(a_log), dsx,
      gate_norm.astype(F32).reshape(1, SSD_INNER), e01)


def _rope(x, cos, sin_lo, sin_hi):
    half = HEAD_DIM // 2
    return x * cos + pltpu.roll(x, LANES - half, 1) * sin_lo + pltpu.roll(x, half, 1) * sin_hi


def _fold8(x, op):
    parts = [x[r * 8:(r + 1) * 8, :] for r in range(x.shape[0] // 8)]
    return _tree(parts, op)


def _tree(parts, op):
    while len(parts) > 1:
        parts = [op(parts[a], parts[a + 1]) if a + 1 < len(parts) else parts[a]
                 for a in range(0, len(parts), 2)]
    return parts[0]


def _attend(i, nh, score_fn, row_shift, vt_scr, o_ref, o_rows, st_scr, m_scr, acc_scr, ot_scr):
    tq = ATT_TILE
    krow = lax.broadcasted_iota(jnp.int32, (tq, tq), 0)
    qcol = lax.broadcasted_iota(jnp.int32, (tq, tq), 1)
    m_scr[...] = jnp.full(m_scr.shape, NEG_INF, F32)
    acc_scr[...] = jnp.zeros(acc_scr.shape, F32)

    def trip(js, diagonal_last):
        shifts, alphas = [], []
        for h in range(nh):
            folds = []
            for u, j in enumerate(js):
                s_t = score_fn(h, j)
                if diagonal_last and u == len(js) - 1:
                    s_t = jnp.where(krow <= qcol, s_t, NEG_INF)
                st_scr[h, u] = s_t
                folds.append(_fold8(s_t, jnp.maximum))
            m_trip = jnp.max(_tree(folds, jnp.maximum), axis=0, keepdims=True)
            rs = row_shift[h]
            m_old = m_scr[h]
            m_new = jnp.maximum(m_old, m_trip if rs is None else m_trip + rs)
            m_scr[h] = m_new
            shifts.append(m_new if rs is None else m_new - rs)
            alphas.append(jnp.exp2(m_old - m_new))
        for h in range(nh):
            pv = [_dot(vt_scr[j, h], jnp.exp2(st_scr[h, u] - shifts[h]).astype(BF16)) for u, j in enumerate(js)]
            acc_scr[h] = alphas[h] * acc_scr[h] + _tree(pv, jnp.add)

    group = st_scr.shape[1]
    rem = i % group
    for r in range(group):
        @pl.when(rem == r)
        def _(r=r):
            trip([i - r + u for u in range(r + 1)], True)

    def full_trip(g, carry):
        trip([group * g + u for u in range(group)], False)
        return carry

    lax.fori_loop(0, i // group, full_trip, 0)
    for h in range(nh):
        ot_scr[h * HEAD_DIM:(h + 1) * HEAD_DIM, :] = acc_scr[h, :HEAD_DIM, :] / acc_scr[h, HEAD_DIM:HEAD_DIM + 1, :]
    for g in range(nh * HEAD_DIM // LANES):
        o_ref[o_rows, g * LANES:(g + 1) * LANES] = ot_scr[g * LANES:(g + 1) * LANES, :].T.astype(o_ref.dtype)


V_ROWS = HEAD_DIM + 16
ATT_GROUPS = 2
ATT_LANES = ATT_GROUPS * LANES
ATT_NH = ATT_LANES // HEAD_DIM
ATT_TPS = 4
MOBA_TRIP = 8
FOX_TRIP = 8


def _store_vt(vt_scr, jb, v_blk):
    n = v_blk.shape[1]
    eye = (lax.broadcasted_iota(jnp.int32, (n, n), 0) == lax.broadcasted_iota(jnp.int32, (n, n), 1)).astype(BF16)
    vt = _dot_nt(eye, v_blk.astype(BF16))
    for h in range(v_blk.shape[1] // HEAD_DIM):
        vt_scr[jb, h, :HEAD_DIM, :] = vt[h * HEAD_DIM:(h + 1) * HEAD_DIM, :].astype(BF16)
        vt_scr[jb, h, HEAD_DIM:, :] = jnp.ones((V_ROWS - HEAD_DIM, v_blk.shape[0]), BF16)


def _attend_scratch(nb, group):
    tq = ATT_TILE
    return [pltpu.VMEM((nb, ATT_NH, V_ROWS, tq), BF16),
            pltpu.VMEM((ATT_NH, group, tq, tq), F32),
            pltpu.VMEM((ATT_NH, 1, tq), F32),
            pltpu.VMEM((ATT_NH, V_ROWS, tq), F32),
            pltpu.VMEM((ATT_LANES, tq), F32)]


def _masked_heads(x, scale):
    lane = lax.broadcasted_iota(jnp.int32, (x.shape[0], LANES), 1)
    out = []
    for h in range(x.shape[1] // HEAD_DIM):
        g, hh = divmod(h, LANES // HEAD_DIM)
        xg = x[:, g * LANES:(g + 1) * LANES]
        in_head = (lane >= hh * HEAD_DIM) & (lane < (hh + 1) * HEAD_DIM)
        out.append(jnp.where(in_head, xg * scale, 0.0))
    return out


def _moba_kernel(q_ref, k_ref, v_ref, cos_ref, sinl_ref, sinh_ref, o_ref, kr_scr, km_scr, bias_scr,
                 vt_scr, st_scr, m_scr, acc_scr, ot_scr):
    tq = ATT_TILE
    nb = kr_scr.shape[1]
    step = pl.program_id(2)
    scale = HEAD_DIM ** -0.5
    hpg = LANES // HEAD_DIM

    @pl.when(step == 0)
    def _():
        for jb in range(nb):
            rows = slice(jb * tq, (jb + 1) * tq)
            for g in range(ATT_GROUPS):
                lanes = slice(g * LANES, (g + 1) * LANES)
                kr = _rope(k_ref[rows, lanes], cos_ref[rows, :], sinl_ref[rows, :], sinh_ref[rows, :])
                kr_scr[g, jb] = kr.astype(BF16)
                km_scr[g, jb:jb + 1, :] = jnp.mean(kr, axis=0, keepdims=True)
            _store_vt(vt_scr, jb, v_ref[rows, :])

    def tile(t, carry):
        i = step * ATT_TPS + t
        trows = pl.ds(pl.multiple_of(t * tq, tq), tq)
        qrows = pl.ds(pl.multiple_of(i * tq, tq), tq)
        cos, sin_lo, sin_hi = cos_ref[qrows, :], sinl_ref[qrows, :], sinh_ref[qrows, :]
        qgs = [_rope(q_ref[trows, g * LANES:(g + 1) * LANES], cos, sin_lo, sin_hi) for g in range(ATT_GROUPS)]
        klane = lax.broadcasted_iota(jnp.int32, (nb, LANES), 1)
        gates = []
        for g in range(ATT_GROUPS):
            q_hi = qgs[g].astype(BF16)
            q_lo = (qgs[g] - q_hi.astype(F32)).astype(BF16)
            for hh in range(hpg):
                km = jnp.where((klane >= hh * HEAD_DIM) & (klane < (hh + 1) * HEAD_DIM), km_scr[g], 0.0)
                k_hi = km.astype(BF16)
                k_lo = (km - k_hi.astype(F32)).astype(BF16)
                gates.append(_dot_nt(k_hi, q_hi) + _dot_nt(k_hi, q_lo) + _dot_nt(k_lo, q_hi))
        jrow = lax.broadcasted_iota(jnp.int32, (nb, ATT_NH * tq), 0)
        valid = jrow < i
        gate = jnp.where(valid, jnp.concatenate(gates, axis=1), NEG_INF)
        cnt = jnp.zeros(gate.shape, F32)
        for jp in range(nb):
            gj = gate[jp:jp + 1, :]
            cnt = cnt + jnp.where((gj > gate) | ((gj == gate) & (jp < jrow)), 1.0, 0.0)
        chosen = valid & (cnt < MOBA_TOPK)
        bias = jnp.where(chosen | (jrow == i), 0.0, NEG_INF)
        for h in range(ATT_NH):
            bias_scr[h] = bias[:, h * tq:(h + 1) * tq]
        qs = [(qm * (scale * LOG2E)).astype(BF16) for qm in _masked_heads(jnp.concatenate(qgs, axis=1), 1.0)]

        def score_fn(h, j):
            return _dot_nt(kr_scr[h // hpg, j], qs[h]) + bias_scr[h, pl.ds(j, 1), :]

        _attend(i, ATT_NH, score_fn, [None] * ATT_NH, vt_scr, o_ref, trows, st_scr, m_scr, acc_scr, ot_scr)
        return carry

    lax.fori_loop(0, ATT_TPS, tile, 0)


def _moba(q, k, v, cos, sin_lo, sin_hi, bsz, s):
    tq = ATT_TILE
    nb = s // tq
    q3, k3, v3 = (a.reshape(bsz, s, MOBA_INNER) for a in (q, k, v))
    whole = pl.BlockSpec((None, s, ATT_LANES), lambda b, p, i: (b, 0, p))
    tile = pl.BlockSpec((None, ATT_TPS * tq, ATT_LANES), lambda b, p, i: (b, i, p))
    tab = pl.BlockSpec((s, LANES), lambda b, p, i: (0, 0))
    out = pl.pallas_call(
        _moba_kernel,
        grid=(bsz, MOBA_INNER // ATT_LANES, nb // ATT_TPS),
        in_specs=[tile, whole, whole, tab, tab, tab],
        out_specs=tile,
        out_shape=jax.ShapeDtypeStruct((bsz, s, MOBA_INNER), BF16),
        scratch_shapes=[pltpu.VMEM((ATT_GROUPS, nb, tq, LANES), BF16),
                        pltpu.VMEM((ATT_GROUPS, nb, LANES), F32),
                        pltpu.VMEM((ATT_NH, nb, tq), F32)] + _attend_scratch(nb, MOBA_TRIP),
        compiler_params=_cparams(("parallel", "parallel", "arbitrary")),
        name="moba",
    )(q3, k3, v3, cos, sin_lo, sin_hi)
    return out.reshape(bsz * s, MOBA_INNER)


def _fox_gate_kernel(f_ref, b_ref, c_ref, ct_ref):
    tb = ATT_TILE
    s = f_ref.shape[0]
    row = lax.broadcasted_iota(jnp.int32, (tb, tb), 0)
    col = lax.broadcasted_iota(jnp.int32, (tb, tb), 1)
    tril = (row >= col).astype(BF16)
    carry = jnp.zeros((1, LANES), F32)
    for jb in range(s // tb):
        rows = slice(jb * tb, (jb + 1) * tb)
        log_f = -_softplus(-(f_ref[rows, :] + b_ref[...]))
        cblk = _sel_dot_rhs(tril, log_f) + carry
        c_ref[rows, :] = cblk
        ct_ref[jb] = cblk.T
        carry = cblk[tb - 1:tb, :]


def _fox_gate(f, bias, bsz, s):
    return pl.pallas_call(
        _fox_gate_kernel,
        grid=(bsz,),
        in_specs=[pl.BlockSpec((None, s, LANES), lambda b: (b, 0, 0)),
                  pl.BlockSpec((1, LANES), lambda b: (0, 0))],
        out_specs=[pl.BlockSpec((None, s, LANES), lambda b: (b, 0, 0)),
                   pl.BlockSpec((None, s // ATT_TILE, LANES, ATT_TILE), lambda b: (b, 0, 0, 0))],
        out_shape=[jax.ShapeDtypeStruct((bsz, s, LANES), F32),
                   jax.ShapeDtypeStruct((bsz, s // ATT_TILE, LANES, ATT_TILE), F32)],
        compiler_params=_cparams(("parallel",)),
        name="fox_gate",
    )(f.reshape(bsz, s, LANES), _pad_lanes(bias))


def _fox_kernel(q_ref, k_ref, v_ref, c_ref, ct_ref, o_ref, k_scr, ck_scr,
                vt_scr, st_scr, m_scr, acc_scr, ot_scr):
    tq = ATT_TILE
    nb = k_scr.shape[1]
    p = pl.program_id(1)
    step = pl.program_id(2)
    scale = HEAD_DIM ** -0.5
    hpg = LANES // HEAD_DIM

    @pl.when(step == 0)
    def _():
        lane = lax.broadcasted_iota(jnp.int32, (tq, LANES), 1)
        for jb in range(nb):
            rows = slice(jb * tq, (jb + 1) * tq)
            for g in range(ATT_GROUPS):
                k_scr[g, jb] = k_ref[rows, g * LANES:(g + 1) * LANES].astype(BF16)
            _store_vt(vt_scr, jb, v_ref[rows, :])
            cblk = c_ref[rows, :] * LOG2E
            for h in range(ATT_NH):
                colv = jnp.sum(jnp.where(lane == p * ATT_NH + h, cblk, 0.0), axis=1, keepdims=True)
                ck_scr[h, jb] = jnp.broadcast_to(colv, (tq, LANES))

    def tile(t, carry):
        i = step * ATT_TPS + t
        trows = pl.ds(pl.multiple_of(t * tq, tq), tq)
        qs = [qm.astype(BF16) for qm in _masked_heads(q_ref[trows, :], scale * LOG2E)]
        cqs = [ct_ref[t, pl.ds(p * ATT_NH + h, 1), :] * LOG2E for h in range(ATT_NH)]

        def score_fn(h, j):
            ck = ck_scr[h, j]
            return _dot_nt(k_scr[h // hpg, j], qs[h]) - jnp.concatenate([ck] * (tq // LANES), axis=1)

        _attend(i, ATT_NH, score_fn, cqs, vt_scr, o_ref, trows, st_scr, m_scr, acc_scr, ot_scr)
        return carry

    lax.fori_loop(0, ATT_TPS, tile, 0)


def _fox(q, k, v, c, ct, bsz, s):
    tq = ATT_TILE
    nb = s // tq
    q3, k3, v3 = (a.reshape(bsz, s, FOX_INNER) for a in (q, k, v))
    whole = pl.BlockSpec((None, s, ATT_LANES), lambda b, p, i: (b, 0, p))
    tile = pl.BlockSpec((None, ATT_TPS * tq, ATT_LANES), lambda b, p, i: (b, i, p))
    out = pl.pallas_call(
        _fox_kernel,
        grid=(bsz, FOX_INNER // ATT_LANES, nb // ATT_TPS),
        in_specs=[tile, whole, whole,
                  pl.BlockSpec((None, s, LANES), lambda b, p, i: (b, 0, 0)),
                  pl.BlockSpec((None, ATT_TPS, LANES, tq), lambda b, p, i: (b, i, 0, 0))],
        out_specs=tile,
        out_shape=jax.ShapeDtypeStruct((bsz, s, FOX_INNER), BF16),
        scratch_shapes=[pltpu.VMEM((ATT_GROUPS, nb, tq, LANES), BF16),
                        pltpu.VMEM((ATT_NH, nb, tq, LANES), F32)] + _attend_scratch(nb, FOX_TRIP),
        compiler_params=_cparams(("parallel", "parallel", "arbitrary")),
        name="fox",
    )(q3, k3, v3, c, ct)
    return out.reshape(bsz * s, FOX_INNER)


def _rope_tables(s):
    half = HEAD_DIM // 2
    inv = jnp.power(ROPE_THETA, -jnp.arange(half, dtype=F32) / half)
    ang = jnp.arange(s, dtype=F32)[:, None] * inv[None, :]
    cos, sin = jnp.cos(ang), jnp.sin(ang)
    reps = LANES // HEAD_DIM
    zero = jnp.zeros_like(sin)
    return (jnp.tile(jnp.concatenate([cos, cos], axis=1), (1, reps)),
            jnp.tile(jnp.concatenate([-sin, zero], axis=1), (1, reps)),
            jnp.tile(jnp.concatenate([zero, sin], axis=1), (1, reps)))


def _even_layer(x, bsz, s, g, w_in, conv_w, conv_b, dt_bias, a_log, d_skip, gate_norm, w_out, rope_tabs):
    d = x.shape[1]
    o1 = SSD_INNER
    o2 = o1 + SSD_XBC
    o3 = o2 + SSD_HEADS
    w_in = w_in.astype(BF16)
    w = jnp.concatenate([w_in[:, :o2], w_in[:, o3:], w_in[:, o2:o3],
                         jnp.zeros((d, LANES - SSD_HEADS), BF16)], axis=1)
    z, xbc, q, k, v, dt = _norm_matmul(
        x, g, w, (SSD_INNER, SSD_XBC, MOBA_INNER, MOBA_INNER, MOBA_INNER, LANES))
    y_ssd = _ssd(z, xbc, dt, conv_w, conv_b, dt_bias, a_log, d_skip, gate_norm, bsz, s)
    y_att = _moba(q, k, v, *rope_tabs, bsz, s)
    w_o = w_out.astype(BF16)
    return [y_ssd, y_att], [w_o[:SSD_INNER], w_o[SSD_INNER:]]


def _odd_layer(x, bsz, s, g, w_in, fgate_bias, w_out):
    d = x.shape[1]
    w = jnp.concatenate([w_in.astype(BF16), jnp.zeros((d, LANES - FOX_HEADS), BF16)], axis=1)
    q, k, v, f = _norm_matmul(x, g, w, (FOX_INNER, FOX_INNER, FOX_INNER, LANES))
    c, ct = _fox_gate(f, fgate_bias, bsz, s)
    y = _fox(q, k, v, c, ct, bsz, s)
    return [y], [w_out.astype(BF16)]


def kernel(x, norm_mix_even, w_in_even, conv_w, conv_b, dt_bias, a_log, d_skip, ssd_gate_norm, w_out_even,
           norm_mix_odd, w_in_odd, fgate_bias, w_out_odd, norm_mlp, w_up, w_down, final_norm):
    bsz, s, d = x.shape
    depth = norm_mlp.shape[0]
    assert s % (ATT_TILE * ATT_TPS) == 0 and s % SSD_STEP == 0
    h = x.reshape(bsz * s, d)
    rope_tabs = _rope_tables(s)
    for layer in range(depth):
        i = layer // 2
        if layer % 2 == 0:
            acts, ws = _even_layer(h, bsz, s, norm_mix_even[i], w_in_even[i], conv_w[i], conv_b[i], dt_bias[i],
                            a_log[i], d_skip[i], ssd_gate_norm[i], w_out_even[i], rope_tabs)
        else:
            acts, ws = _odd_layer(h, bsz, s, norm_mix_odd[i], w_in_odd[i], fgate_bias[i], w_out_odd[i])
        h = _proj_mlp(acts, ws, h, norm_mlp[layer], w_up[layer].astype(BF16), w_down[layer].astype(BF16),
                      final_norm, final_norm=(layer == depth - 1))
    return h.reshape(bsz, s, d)
```

```python
import functools
import math

import jax
import jax.numpy as jnp
from jax import lax
from jax.experimental import pallas as pl
from jax.experimental.pallas import tpu as pltpu

NORM_EPS = 1e-5
ROPE_THETA = 10000.0
LANES = 128
HEAD_DIM = 64
SSD_HEADS = 16
SSD_GROUPS = 4
SSD_STATE = 128
SSD_CONV = 4
SSD_CHUNK = 128
SSD_INNER = SSD_HEADS * HEAD_DIM
SSD_GN = SSD_GROUPS * SSD_STATE
SSD_XBC = SSD_INNER + 2 * SSD_GN
MOBA_HEADS = 8
MOBA_INNER = MOBA_HEADS * HEAD_DIM
MOBA_BLOCK = 256
MOBA_TOPK = 3
FOX_HEADS = 16
FOX_INNER = FOX_HEADS * HEAD_DIM
ATT_TILE = MOBA_BLOCK
VMEM_LIMIT = 56 * 1024 * 1024

F32 = jnp.float32
BF16 = jnp.bfloat16
NEG_INF = float("-inf")
LOG2E = math.log2(math.e)


def _cparams(sem):
    return pltpu.CompilerParams(dimension_semantics=sem, vmem_limit_bytes=VMEM_LIMIT)


def _resident(shape):
    return pl.BlockSpec(shape, lambda i: (0,) * len(shape), pipeline_mode=pl.Buffered(1))


def _rms(x, g):
    return x * lax.rsqrt(jnp.mean(x * x, axis=-1, keepdims=True) + NORM_EPS) * g


def _silu(x):
    h = 0.5 * x
    return h + h * jnp.tanh(h)


def _softplus(x):
    return jnp.maximum(x, 0.0) + jnp.log1p(jnp.exp(-jnp.abs(x)))


def _split3(x):
    hi = x.astype(BF16)
    r = x - hi.astype(F32)
    mid = r.astype(BF16)
    lo = (r - mid.astype(F32)).astype(BF16)
    return hi, mid, lo


def _dot(a, b):
    return jnp.dot(a, b, preferred_element_type=F32)


def _dot_nt(a, b):
    return lax.dot_general(a, b, (((1,), (1,)), ((), ())), preferred_element_type=F32)


def _sel_dot_rhs(m01, x):
    hi, mid, lo = _split3(x)
    return _dot(m01, hi) + _dot(m01, mid) + _dot(m01, lo)


def _sel_dot_lhs(x, m01):
    hi, mid, lo = _split3(x)
    return _dot(hi, m01) + _dot(mid, m01) + _dot(lo, m01)


def _norm_matmul_kernel(*refs, splits, nw):
    x_ref, g_ref = refs[:2]
    w_refs, out_refs = refs[2:2 + nw], refs[2 + nw:]
    half = x_ref.shape[0] // 2
    for r in range(2):
        rows = slice(r * half, (r + 1) * half)
        xn = _rms(x_ref[rows, :], g_ref[...]).astype(BF16)
        for o_ref, (wi, a, b) in zip(out_refs, splits):
            o_ref[rows, :] = _dot(xn, w_refs[wi][:, a:b]).astype(o_ref.dtype)


def _norm_matmul(x, g, ws, widths, tm=512):
    t, d = x.shape
    splits = []
    for wi, (w, wds) in enumerate(zip(ws, widths)):
        a = 0
        for wd in wds:
            splits.append((wi, a, a + wd))
            a += wd
        assert a == w.shape[1]
    assert t % tm == 0
    out_w = [b - a for _, a, b in splits]
    return pl.pallas_call(
        functools.partial(_norm_matmul_kernel, splits=tuple(splits), nw=len(ws)),
        grid=(t // tm,),
        in_specs=[pl.BlockSpec((tm, d), lambda i: (i, 0)), _resident((1, d))] + [_resident(w.shape) for w in ws],
        out_specs=[pl.BlockSpec((tm, wd), lambda i: (i, 0)) for wd in out_w],
        out_shape=[jax.ShapeDtypeStruct((t, wd), F32) for wd in out_w],
        compiler_params=_cparams(("parallel",)),
        name="norm_matmul",
    )(x, g.reshape(1, d), *ws)


def _pad_cols(w):
    return jnp.pad(w, ((0, 0), (0, LANES - w.shape[1])))


MLP_CHUNK = 1024


def _proj_mlp_kernel(*refs, n, final_norm):
    a_refs, w_refs = refs[:n], refs[n:2 * n]
    x_ref, g_ref, wu_ref, wd_ref, gf_ref, o_ref = refs[2 * n:]
    proj = _dot(a_refs[0][...], w_refs[0][...])
    for a_ref, w_ref in zip(a_refs[1:], w_refs[1:]):
        proj = proj + _dot(a_ref[...], w_ref[...])
    x = x_ref[...] + proj
    xn = _rms(x, g_ref[...]).astype(BF16)
    acc = x
    for c in range(wu_ref.shape[1] // MLP_CHUNK):
        cols = slice(c * MLP_CHUNK, (c + 1) * MLP_CHUNK)
        h = _dot(xn, wu_ref[:, cols])
        h = jnp.square(jnp.maximum(h, 0.0)).astype(BF16)
        acc = acc + _dot(h, wd_ref[cols, :])
    if final_norm:
        acc = _rms(acc, gf_ref[...])
    o_ref[...] = acc


def _proj_mlp(acts, ws, x, g, w_up, w_down, g_final, final_norm, tm=512):
    t, d = x.shape
    n = len(acts)
    row = lambda width: pl.BlockSpec((tm, width), lambda i: (i, 0))
    const = lambda shape: pl.BlockSpec(shape, lambda i: (0, 0))
    in_specs = ([row(a.shape[1]) for a in acts] + [const(w.shape) for w in ws]
                + [row(d), const((1, d)), _resident(w_up.shape), _resident(w_down.shape), const((1, d))])
    return pl.pallas_call(
        functools.partial(_proj_mlp_kernel, n=n, final_norm=final_norm),
        grid=(t // tm,),
        in_specs=in_specs,
        out_specs=row(d),
        out_shape=jax.ShapeDtypeStruct((t, d), F32),
        compiler_params=_cparams(("parallel",)),
        name="proj_mlp",
    )(*acts, *ws, x, g.reshape(1, d), w_up, w_down, g_final.reshape(1, d))


SSD_STEP = 2 * SSD_CHUNK
CONV_STRIDE = SSD_STEP // 8 + 1
CONV_ROWS = 8 * CONV_STRIDE
SSD_SLABS = SSD_XBC // LANES


def _ssd_kernel(z_ref, xbc_ref, dt_ref, cw_ref, cb_ref, dtb_ref, alog_ref, dskip_ref, gn_ref,
                e_ref, o_ref, ubuf, xa_scr, hst):
    L = SSD_CHUNK
    TL = SSD_STEP
    c = pl.program_id(1)

    @pl.when(c == 0)
    def _():
        hst[...] = jnp.zeros_like(hst)
        ubuf[...] = jnp.zeros_like(ubuf)

    for sl in range(SSD_SLABS):
        ubuf[sl, 8:8 + TL, :] = xbc_ref[:, sl * LANES:(sl + 1) * LANES]

    def conv_slab(sl, carry):
        w = cw_ref[sl]
        b = cb_ref[sl]
        u = [ubuf[sl, pl.ds(5 + m, 8, stride=CONV_STRIDE), :] for m in range(CONV_STRIDE + SSD_CONV - 1)]
        for r in range(CONV_STRIDE):
            acc = b
            for k in range(SSD_CONV):
                acc = acc + w[k:k + 1, :] * u[r + k]
            xa_scr[sl, pl.ds(r, 8, stride=CONV_STRIDE), :] = acc + acc * jnp.tanh(acc)
        ubuf[sl, 0:8, :] = ubuf[sl, TL:TL + 8, :]
        return carry

    lax.fori_loop(0, SSD_SLABS, conv_slab, 0)
    nxs = SSD_INNER // LANES
    ngs = SSD_GN // LANES
    row = lax.broadcasted_iota(jnp.int32, (L, L), 0)
    col = lax.broadcasted_iota(jnp.int32, (L, L), 1)
    causal = row >= col
    tril = causal.astype(BF16)
    e01 = e_ref[...]
    lane = lax.broadcasted_iota(jnp.int32, (L, LANES), 1)
    neg_a = -jnp.exp(alog_ref[...])
    gw = SSD_INNER // SSD_GROUPS
    hpl = LANES // HEAD_DIM

    for ci in range(TL // L):
        rows = slice(ci * L, (ci + 1) * L)
        xs = jnp.concatenate([xa_scr[sl, rows, :] for sl in range(nxs)], axis=1)
        dt = _softplus(dt_ref[rows, :] + dtb_ref[...])
        acum = _sel_dot_rhs(tril, dt * neg_a)
        alast = acum[L - 1:L, :]
        dtx = _dot(dt.astype(BF16), e01)
        eacx = _dot(jnp.exp(acum).astype(BF16), e01)
        dtex = _dot(jnp.exp(alast - acum).astype(BF16), e01)
        cdx = _sel_dot_lhs(jnp.exp(acum[L - 8:L, :]), e01)[7:8, :]
        xc = xs * dtx
        xcd = (xc * dtex).astype(BF16)
        acum2 = acum * LOG2E
        act2 = acum2.T

        cbs, y_offs = [], []
        for g in range(SSD_GROUPS):
            b_f32 = xa_scr[nxs + g, rows, :]
            bg = b_f32.astype(BF16)
            bgt = b_f32.T.astype(BF16)
            cg = xa_scr[nxs + ngs + g, rows, :].astype(BF16)
            cbs.append(_dot_nt(cg, bg))
            hprev = hst[g]
            y_offs.append(_dot(cg, hprev.astype(BF16)) * eacx[:, g * gw:(g + 1) * gw])
            hst[g] = hprev * cdx[:, g * gw:(g + 1) * gw] + _dot(bgt, xcd[:, g * gw:(g + 1) * gw])
        ms = []
        for h in range(SSD_HEADS):
            dec = jnp.exp2(jnp.where(causal, acum2[:, h:h + 1] - act2[h:h + 1, :], NEG_INF))
            ms.append((cbs[h * HEAD_DIM // gw] * dec).astype(BF16))
        y_diags = []
        for pr in range(SSD_INNER // LANES):
            xcp = xc[:, pr * LANES:(pr + 1) * LANES]
            xm = [jnp.where((lane >= hh * HEAD_DIM) & (lane < (hh + 1) * HEAD_DIM), xcp, 0.0).astype(BF16)
                  for hh in range(hpl)]
            y_diags.append(_dot(jnp.concatenate(ms[pr * hpl:(pr + 1) * hpl], axis=1), jnp.concatenate(xm, axis=0)))

        ppg = gw // LANES
        for g in range(SSD_GROUPS):
            lanes = slice(g * gw, (g + 1) * gw)
            y = jnp.concatenate(y_diags[g * ppg:(g + 1) * ppg], axis=1) + y_offs[g] + xs[:, lanes] * dskip_ref[:, lanes]
            yg = y * _silu(z_ref[rows, lanes])
            ms_g = jnp.mean(yg * yg, axis=-1, keepdims=True)
            o_ref[rows, lanes] = (yg * lax.rsqrt(ms_g + NORM_EPS) * gn_ref[:, lanes]).astype(o_ref.dtype)


def _pad_lanes(v):
    return jnp.pad(v.astype(F32), (0, LANES - v.shape[0])).reshape(1, LANES)


def _ssd(z, xbc, dt, conv_w, conv_b, dt_bias, a_log, d_skip, gate_norm, bsz, s):
    L = SSD_STEP
    nc = s // L
    e01 = (jnp.arange(LANES)[:, None] == (jnp.arange(SSD_INNER) // HEAD_DIM)[None, :]).astype(BF16)
    dsx = jnp.repeat(d_skip.astype(F32), HEAD_DIM).reshape(1, SSD_INNER)
    cw = (0.5 * conv_w.astype(F32)).reshape(SSD_CONV, SSD_SLABS, LANES).transpose(1, 0, 2)
    cw = jnp.pad(cw, ((0, 0), (0, 8 - SSD_CONV), (0, 0)))
    cb = (0.5 * conv_b.astype(F32)).reshape(SSD_SLABS, 1, LANES)
    tok = lambda b, c: (b * nc + c, 0)
    const = lambda b, c: (0, 0)
    const3 = lambda b, c: (0, 0, 0)
    return pl.pallas_call(
        _ssd_kernel,
        grid=(bsz, nc),
        in_specs=[pl.BlockSpec((L, SSD_INNER), tok),
                  pl.BlockSpec((L, SSD_XBC), tok),
                  pl.BlockSpec((L, LANES), tok),
                  pl.BlockSpec((SSD_SLABS, 8, LANES), const3),
                  pl.BlockSpec((SSD_SLABS, 1, LANES), const3),
                  pl.BlockSpec((1, LANES), const),
                  pl.BlockSpec((1, LANES), const),
                  pl.BlockSpec((1, SSD_INNER), const),
                  pl.BlockSpec((1, SSD_INNER), const),
                  pl.BlockSpec((LANES, SSD_INNER), const)],
        out_specs=pl.BlockSpec((L, SSD_INNER), tok),
        out_shape=jax.ShapeDtypeStruct((bsz * s, SSD_INNER), BF16),
        scratch_shapes=[pltpu.VMEM((SSD_SLABS, 8 + CONV_ROWS + 8, LANES), F32),
                        pltpu.VMEM((SSD_SLABS, CONV_ROWS, LANES), F32),
                        pltpu.VMEM((SSD_GROUPS, SSD_STATE, SSD_INNER // SSD_GROUPS), F32)],
        compiler_params=_cparams(("parallel", "arbitrary")),
        name="ssd",
    )(z, xbc, dt, cw, cb, _pad_lanes(dt_bias), _pad_lanes(a_log), dsx,
      gate_norm.astype(F32).reshape(1, SSD_INNER), e01)


def _rope(x, cos, sin_lo, sin_hi):
    half = HEAD_DIM // 2
    return x * cos + pltpu.roll(x, LANES - half, 1) * sin_lo + pltpu.roll(x, half, 1) * sin_hi


def _fold8(x, op):
    parts = [x[r * 8:(r + 1) * 8, :] for r in range(x.shape[0] // 8)]
    return _tree(parts, op)


def _tree(parts, op):
    while len(parts) > 1:
        parts = [op(parts[a], parts[a + 1]) if a + 1 < len(parts) else parts[a]
                 for a in range(0, len(parts), 2)]
    return parts[0]


def _attend(i, nh, score_fn, row_shift, vt_scr, o_ref, o_rows, st_scr, m_scr, acc_scr, ot_scr):
    tq = ATT_TILE
    krow = lax.broadcasted_iota(jnp.int32, (tq, tq), 0)
    qcol = lax.broadcasted_iota(jnp.int32, (tq, tq), 1)
    m_scr[...] = jnp.full(m_scr.shape, NEG_INF, F32)
    acc_scr[...] = jnp.zeros(acc_scr.shape, F32)

    def trip(js, diagonal_last):
        shifts, alphas = [], []
        for h in range(nh):
            folds = []
            for u, j in enumerate(js):
                s_t = score_fn(h, j)
                if diagonal_last and u == len(js) - 1:
                    s_t = jnp.where(krow <= qcol, s_t, NEG_INF)
                st_scr[h, u] = s_t
                folds.append(_fold8(s_t, jnp.maximum))
            m_trip = jnp.max(_tree(folds, jnp.maximum), axis=0, keepdims=True)
            rs = row_shift[h]
            m_old = m_scr[h]
            m_new = jnp.maximum(m_old, m_trip if rs is None else m_trip + rs)
            m_scr[h] = m_new
            shifts.append(m_new if rs is None else m_new - rs)
            alphas.append(jnp.exp2(m_old - m_new))
        for h in range(nh):
            pv = [_dot(vt_scr[j, h], jnp.exp2(st_scr[h, u] - shifts[h]).astype(BF16)) for u, j in enumerate(js)]
            acc_scr[h] = alphas[h] * acc_scr[h] + _tree(pv, jnp.add)

    group = st_scr.shape[1]
    rem = i % group
    for r in range(group):
        @pl.when(rem == r)
        def _(r=r):
            trip([i - r + u for u in range(r + 1)], True)

    def full_trip(g, carry):
        trip([group * g + u for u in range(group)], False)
        return carry

    lax.fori_loop(0, i // group, full_trip, 0)
    for h in range(nh):
        ot_scr[h * HEAD_DIM:(h + 1) * HEAD_DIM, :] = acc_scr[h, :HEAD_DIM, :] / acc_scr[h, HEAD_DIM:HEAD_DIM + 1, :]
    for g in range(nh * HEAD_DIM // LANES):
        o_ref[o_rows, g * LANES:(g + 1) * LANES] = ot_scr[g * LANES:(g + 1) * LANES, :].T.astype(o_ref.dtype)


V_ROWS = HEAD_DIM + 16
ATT_GROUPS = 2
ATT_LANES = ATT_GROUPS * LANES
ATT_NH = ATT_LANES // HEAD_DIM
ATT_TPS = 4
MOBA_TRIP = 8
FOX_TRIP = 8


def _store_vt(vt_scr, jb, v_blk):
    n = v_blk.shape[1]
    eye = (lax.broadcasted_iota(jnp.int32, (n, n), 0) == lax.broadcasted_iota(jnp.int32, (n, n), 1)).astype(BF16)
    vt = _dot_nt(eye, v_blk.astype(BF16))
    for h in range(v_blk.shape[1] // HEAD_DIM):
        vt_scr[jb, h, :HEAD_DIM, :] = vt[h * HEAD_DIM:(h + 1) * HEAD_DIM, :].astype(BF16)
        vt_scr[jb, h, HEAD_DIM:, :] = jnp.ones((V_ROWS - HEAD_DIM, v_blk.shape[0]), BF16)


def _attend_scratch(nb, group):
    tq = ATT_TILE
    return [pltpu.VMEM((nb, ATT_NH, V_ROWS, tq), BF16),
            pltpu.VMEM((ATT_NH, group, tq, tq), F32),
            pltpu.VMEM((ATT_NH, 1, tq), F32),
            pltpu.VMEM((ATT_NH, V_ROWS, tq), F32),
            pltpu.VMEM((ATT_LANES, tq), F32)]


def _masked_heads(x, scale):
    lane = lax.broadcasted_iota(jnp.int32, (x.shape[0], LANES), 1)
    out = []
    for h in range(x.shape[1] // HEAD_DIM):
        g, hh = divmod(h, LANES // HEAD_DIM)
        xg = x[:, g * LANES:(g + 1) * LANES]
        in_head = (lane >= hh * HEAD_DIM) & (lane < (hh + 1) * HEAD_DIM)
        out.append(jnp.where(in_head, xg * scale, 0.0))
    return out


def _moba_kernel(q_ref, k_ref, v_ref, cos_ref, sinl_ref, sinh_ref, o_ref, kr_scr, km_scr, bias_scr,
                 vt_scr, st_scr, m_scr, acc_scr, ot_scr):
    tq = ATT_TILE
    nb = kr_scr.shape[1]
    step = pl.program_id(2)
    scale = HEAD_DIM ** -0.5
    hpg = LANES // HEAD_DIM

    @pl.when(step == 0)
    def _():
        for jb in range(nb):
            rows = slice(jb * tq, (jb + 1) * tq)
            for g in range(ATT_GROUPS):
                lanes = slice(g * LANES, (g + 1) * LANES)
                kr = _rope(k_ref[rows, lanes], cos_ref[rows, :], sinl_ref[rows, :], sinh_ref[rows, :])
                kr_scr[g, jb] = kr.astype(BF16)
                km_scr[g, jb:jb + 1, :] = jnp.mean(kr, axis=0, keepdims=True)
            _store_vt(vt_scr, jb, v_ref[rows, :])

    def tile(t, carry):
        i = step * ATT_TPS + t
        trows = pl.ds(pl.multiple_of(t * tq, tq), tq)
        qrows = pl.ds(pl.multiple_of(i * tq, tq), tq)
        cos, sin_lo, sin_hi = cos_ref[qrows, :], sinl_ref[qrows, :], sinh_ref[qrows, :]
        qgs = [_rope(q_ref[trows, g * LANES:(g + 1) * LANES], cos, sin_lo, sin_hi) for g in range(ATT_GROUPS)]
        klane = lax.broadcasted_iota(jnp.int32, (nb, LANES), 1)
        gates = []
        for g in range(ATT_GROUPS):
            q_hi = qgs[g].astype(BF16)
            q_lo = (qgs[g] - q_hi.astype(F32)).astype(BF16)
            for hh in range(hpg):
                km = jnp.where((klane >= hh * HEAD_DIM) & (klane < (hh + 1) * HEAD_DIM), km_scr[g], 0.0)
                k_hi = km.astype(BF16)
                k_lo = (km - k_hi.astype(F32)).astype(BF16)
                gates.append(_dot_nt(k_hi, q_hi) + _dot_nt(k_hi, q_lo) + _dot_nt(k_lo, q_hi))
        jrow = lax.broadcasted_iota(jnp.int32, (nb, ATT_NH * tq), 0)
        valid = jrow < i
        gate = jnp.where(valid, jnp.concatenate(gates, axis=1), NEG_INF)
        cnt = jnp.zeros(gate.shape, F32)
        for jp in range(nb):
            gj = gate[jp:jp + 1, :]
            cnt = cnt + jnp.where((gj > gate) | ((gj == gate) & (jp < jrow)), 1.0, 0.0)
        chosen = valid & (cnt < MOBA_TOPK)
        bias = jnp.where(chosen | (jrow == i), 0.0, NEG_INF)
        for h in range(ATT_NH):
            bias_scr[h] = bias[:, h * tq:(h + 1) * tq]
        qs = [(qm * (scale * LOG2E)).astype(BF16) for qm in _masked_heads(jnp.concatenate(qgs, axis=1), 1.0)]

        def score_fn(h, j):
            return _dot_nt(kr_scr[h // hpg, j], qs[h]) + bias_scr[h, pl.ds(j, 1), :]

        _attend(i, ATT_NH, score_fn, [None] * ATT_NH, vt_scr, o_ref, trows, st_scr, m_scr, acc_scr, ot_scr)
        return carry

    lax.fori_loop(0, ATT_TPS, tile, 0)


def _moba(q, k, v, cos, sin_lo, sin_hi, bsz, s):
    tq = ATT_TILE
    nb = s // tq
    q3, k3, v3 = (a.reshape(bsz, s, MOBA_INNER) for a in (q, k, v))
    whole = pl.BlockSpec((None, s, ATT_LANES), lambda b, p, i: (b, 0, p))
    tile = pl.BlockSpec((None, ATT_TPS * tq, ATT_LANES), lambda b, p, i: (b, i, p))
    tab = pl.BlockSpec((s, LANES), lambda b, p, i: (0, 0))
    out = pl.pallas_call(
        _moba_kernel,
        grid=(bsz, MOBA_INNER // ATT_LANES, nb // ATT_TPS),
        in_specs=[tile, whole, whole, tab, tab, tab],
        out_specs=tile,
        out_shape=jax.ShapeDtypeStruct((bsz, s, MOBA_INNER), BF16),
        scratch_shapes=[pltpu.VMEM((ATT_GROUPS, nb, tq, LANES), BF16),
                        pltpu.VMEM((ATT_GROUPS, nb, LANES), F32),
                        pltpu.VMEM((ATT_NH, nb, tq), F32)] + _attend_scratch(nb, MOBA_TRIP),
        compiler_params=_cparams(("parallel", "parallel", "arbitrary")),
        name="moba",
    )(q3, k3, v3, cos, sin_lo, sin_hi)
    return out.reshape(bsz * s, MOBA_INNER)


def _fox_gate_kernel(f_ref, b_ref, c_ref, ct_ref):
    tb = ATT_TILE
    s = f_ref.shape[0]
    row = lax.broadcasted_iota(jnp.int32, (tb, tb), 0)
    col = lax.broadcasted_iota(jnp.int32, (tb, tb), 1)
    tril = (row >= col).astype(BF16)
    carry = jnp.zeros((1, LANES), F32)
    for jb in range(s // tb):
        rows = slice(jb * tb, (jb + 1) * tb)
        log_f = -_softplus(-(f_ref[rows, :] + b_ref[...]))
        cblk = _sel_dot_rhs(tril, log_f) + carry
        c_ref[rows, :] = cblk
        ct_ref[jb] = cblk.T
        carry = cblk[tb - 1:tb, :]


def _fox_gate(f, bias, bsz, s):
    return pl.pallas_call(
        _fox_gate_kernel,
        grid=(bsz,),
        in_specs=[pl.BlockSpec((None, s, LANES), lambda b: (b, 0, 0)),
                  pl.BlockSpec((1, LANES), lambda b: (0, 0))],
        out_specs=[pl.BlockSpec((None, s, LANES), lambda b: (b, 0, 0)),
                   pl.BlockSpec((None, s // ATT_TILE, LANES, ATT_TILE), lambda b: (b, 0, 0, 0))],
        out_shape=[jax.ShapeDtypeStruct((bsz, s, LANES), F32),
                   jax.ShapeDtypeStruct((bsz, s // ATT_TILE, LANES, ATT_TILE), F32)],
        compiler_params=_cparams(("parallel",)),
        name="fox_gate",
    )(f.reshape(bsz, s, LANES), _pad_lanes(bias))


def _fox_kernel(q_ref, k_ref, v_ref, c_ref, ct_ref, o_ref, k_scr, ck_scr,
                vt_scr, st_scr, m_scr, acc_scr, ot_scr):
    tq = ATT_TILE
    nb = k_scr.shape[1]
    p = pl.program_id(1)
    step = pl.program_id(2)
    scale = HEAD_DIM ** -0.5
    hpg = LANES // HEAD_DIM

    @pl.when(step == 0)
    def _():
        lane = lax.broadcasted_iota(jnp.int32, (tq, LANES), 1)
        for jb in range(nb):
            rows = slice(jb * tq, (jb + 1) * tq)
            for g in range(ATT_GROUPS):
                k_scr[g, jb] = k_ref[rows, g * LANES:(g + 1) * LANES].astype(BF16)
            _store_vt(vt_scr, jb, v_ref[rows, :])
            cblk = c_ref[rows, :] * LOG2E
            for h in range(ATT_NH):
                colv = jnp.sum(jnp.where(lane == p * ATT_NH + h, cblk, 0.0), axis=1, keepdims=True)
                ck_scr[h, jb] = jnp.broadcast_to(colv, (tq, LANES))

    def tile(t, carry):
        i = step * ATT_TPS + t
        trows = pl.ds(pl.multiple_of(t * tq, tq), tq)
        qs = [qm.astype(BF16) for qm in _masked_heads(q_ref[trows, :], scale * LOG2E)]
        cqs = [ct_ref[t, pl.ds(p * ATT_NH + h, 1), :] * LOG2E for h in range(ATT_NH)]

        def score_fn(h, j):
            ck = ck_scr[h, j]
            return _dot_nt(k_scr[h // hpg, j], qs[h]) - jnp.concatenate([ck] * (tq // LANES), axis=1)

        _attend(i, ATT_NH, score_fn, cqs, vt_scr, o_ref, trows, st_scr, m_scr, acc_scr, ot_scr)
        return carry

    lax.fori_loop(0, ATT_TPS, tile, 0)


def _fox(q, k, v, c, ct, bsz, s):
    tq = ATT_TILE
    nb = s // tq
    q3, k3, v3 = (a.reshape(bsz, s, FOX_INNER) for a in (q, k, v))
    whole = pl.BlockSpec((None, s, ATT_LANES), lambda b, p, i: (b, 0, p))
    tile = pl.BlockSpec((None, ATT_TPS * tq, ATT_LANES), lambda b, p, i: (b, i, p))
    out = pl.pallas_call(
        _fox_kernel,
        grid=(bsz, FOX_INNER // ATT_LANES, nb // ATT_TPS),
        in_specs=[tile, whole, whole,
                  pl.BlockSpec((None, s, LANES), lambda b, p, i: (b, 0, 0)),
                  pl.BlockSpec((None, ATT_TPS, LANES, tq), lambda b, p, i: (b, i, 0, 0))],
        out_specs=tile,
        out_shape=jax.ShapeDtypeStruct((bsz, s, FOX_INNER), BF16),
        scratch_shapes=[pltpu.VMEM((ATT_GROUPS, nb, tq, LANES), BF16),
                        pltpu.VMEM((ATT_NH, nb, tq, LANES), F32)] + _attend_scratch(nb, FOX_TRIP),
        compiler_params=_cparams(("parallel", "parallel", "arbitrary")),
        name="fox",
    )(q3, k3, v3, c, ct)
    return out.reshape(bsz * s, FOX_INNER)


def _rope_tables(s):
    half = HEAD_DIM // 2
    inv = jnp.power(ROPE_THETA, -jnp.arange(half, dtype=F32) / half)
    ang = jnp.arange(s, dtype=F32)[:, None] * inv[None, :]
    cos, sin = jnp.cos(ang), jnp.sin(ang)
    reps = LANES // HEAD_DIM
    zero = jnp.zeros_like(sin)
    return (jnp.tile(jnp.concatenate([cos, cos], axis=1), (1, reps)),
            jnp.tile(jnp.concatenate([-sin, zero], axis=1), (1, reps)),
            jnp.tile(jnp.concatenate([zero, sin], axis=1), (1, reps)))


def _even_layer(x, bsz, s, g, w_in, conv_w, conv_b, dt_bias, a_log, d_skip, gate_norm, w_out, rope_tabs):
    o1 = SSD_INNER
    o2 = o1 + SSD_XBC
    o3 = o2 + SSD_HEADS
    ws = [w_in[:, :o2].astype(BF16), w_in[:, o3:].astype(BF16), _pad_cols(w_in[:, o2:o3]).astype(BF16)]
    z, xbc, q, k, v, dt = _norm_matmul(
        x, g, ws, ((SSD_INNER, SSD_XBC), (MOBA_INNER, MOBA_INNER, MOBA_INNER), (LANES,)))
    y_ssd = _ssd(z, xbc, dt, conv_w, conv_b, dt_bias, a_log, d_skip, gate_norm, bsz, s)
    y_att = _moba(q, k, v, *rope_tabs, bsz, s)
    w_o = w_out.astype(BF16)
    return [y_ssd, y_att], [w_o[:SSD_INNER], w_o[SSD_INNER:]]


def _odd_layer(x, bsz, s, g, w_in, fgate_bias, w_out):
    qkv = 3 * FOX_INNER
    ws = [w_in[:, :qkv].astype(BF16), _pad_cols(w_in[:, qkv:]).astype(BF16)]
    q, k, v, f = _norm_matmul(x, g, ws, ((FOX_INNER, FOX_INNER, FOX_INNER), (LANES,)))
    c, ct = _fox_gate(f, fgate_bias, bsz, s)
    y = _fox(q, k, v, c, ct, bsz, s)
    return [y], [w_out.astype(BF16)]


def kernel(x, norm_mix_even, w_in_even, conv_w, conv_b, dt_bias, a_log, d_skip, ssd_gate_norm, w_out_even,
           norm_mix_odd, w_in_odd, fgate_bias, w_out_odd, norm_mlp, w_up, w_down, final_norm):
    bsz, s, d = x.shape
    depth = norm_mlp.shape[0]
    assert s % (ATT_TILE * ATT_TPS) == 0 and s % SSD_STEP == 0
    h = x.reshape(bsz * s, d)
    rope_tabs = _rope_tables(s)
    for layer in range(depth):
        i = layer // 2
        if layer % 2 == 0:
            acts, ws = _even_layer(h, bsz, s, norm_mix_even[i], w_in_even[i], conv_w[i], conv_b[i], dt_bias[i],
                            a_log[i], d_skip[i], ssd_gate_norm[i], w_out_even[i], rope_tabs)
        else:
            acts, ws = _odd_layer(h, bsz, s, norm_mix_odd[i], w_in_odd[i], fgate_bias[i], w_out_odd[i])
        h = _proj_mlp(acts, ws, h, norm_mlp[layer], w_up[layer].astype(BF16), w_down[layer].astype(BF16),
                      final_norm, final_norm=(layer == depth - 1))
    return h.reshape(bsz, s, d)
```

```python
import functools
import math

import jax
import jax.numpy as jnp
from jax import lax
from jax.experimental import pallas as pl
from jax.experimental.pallas import tpu as pltpu

NORM_EPS = 1e-5
ROPE_THETA = 10000.0
LANES = 128
HEAD_DIM = 64
SSD_HEADS = 16
SSD_GROUPS = 4
SSD_STATE = 128
SSD_CONV = 4
SSD_CHUNK = 128
SSD_INNER = SSD_HEADS * HEAD_DIM
SSD_GN = SSD_GROUPS * SSD_STATE
SSD_XBC = SSD_INNER + 2 * SSD_GN
MOBA_HEADS = 8
MOBA_INNER = MOBA_HEADS * HEAD_DIM
MOBA_BLOCK = 256
MOBA_TOPK = 3
FOX_HEADS = 16
FOX_INNER = FOX_HEADS * HEAD_DIM
ATT_TILE = MOBA_BLOCK
VMEM_LIMIT = 56 * 1024 * 1024

F32 = jnp.float32
BF16 = jnp.bfloat16
NEG_INF = float("-inf")
LOG2E = math.log2(math.e)


def _cparams(sem):
    return pltpu.CompilerParams(dimension_semantics=sem, vmem_limit_bytes=VMEM_LIMIT)


def _resident(shape):
    return pl.BlockSpec(shape, lambda i: (0,) * len(shape), pipeline_mode=pl.Buffered(1))


def _rms(x, g):
    return x * lax.rsqrt(jnp.mean(x * x, axis=-1, keepdims=True) + NORM_EPS) * g


def _silu(x):
    h = 0.5 * x
    return h + h * jnp.tanh(h)


def _softplus(x):
    return jnp.maximum(x, 0.0) + jnp.log1p(jnp.exp(-jnp.abs(x)))


def _split3(x):
    hi = x.astype(BF16)
    r = x - hi.astype(F32)
    mid = r.astype(BF16)
    lo = (r - mid.astype(F32)).astype(BF16)
    return hi, mid, lo


def _dot(a, b):
    return jnp.dot(a, b, preferred_element_type=F32)


def _dot_nt(a, b):
    return lax.dot_general(a, b, (((1,), (1,)), ((), ())), preferred_element_type=F32)


def _sel_dot_rhs(m01, x):
    hi, mid, lo = _split3(x)
    return _dot(m01, hi) + _dot(m01, mid) + _dot(m01, lo)


def _sel_dot_lhs(x, m01):
    hi, mid, lo = _split3(x)
    return _dot(hi, m01) + _dot(mid, m01) + _dot(lo, m01)


def _norm_matmul_kernel(*refs, splits, nw):
    x_ref, g_ref = refs[:2]
    w_refs, out_refs = refs[2:2 + nw], refs[2 + nw:]
    half = x_ref.shape[0] // 2
    for r in range(2):
        rows = slice(r * half, (r + 1) * half)
        xn = _rms(x_ref[rows, :], g_ref[...]).astype(BF16)
        for o_ref, (wi, a, b) in zip(out_refs, splits):
            o_ref[rows, :] = _dot(xn, w_refs[wi][:, a:b]).astype(o_ref.dtype)


def _norm_matmul(x, g, ws, widths, tm=512):
    t, d = x.shape
    splits = []
    for wi, (w, wds) in enumerate(zip(ws, widths)):
        a = 0
        for wd in wds:
            splits.append((wi, a, a + wd))
            a += wd
        assert a == w.shape[1]
    assert t % tm == 0
    out_w = [b - a for _, a, b in splits]
    return pl.pallas_call(
        functools.partial(_norm_matmul_kernel, splits=tuple(splits), nw=len(ws)),
        grid=(t // tm,),
        in_specs=[pl.BlockSpec((tm, d), lambda i: (i, 0)), _resident((1, d))] + [_resident(w.shape) for w in ws],
        out_specs=[pl.BlockSpec((tm, wd), lambda i: (i, 0)) for wd in out_w],
        out_shape=[jax.ShapeDtypeStruct((t, wd), F32) for wd in out_w],
        compiler_params=_cparams(("parallel",)),
        name="norm_matmul",
    )(x, g.reshape(1, d), *ws)


def _pad_cols(w):
    return jnp.pad(w, ((0, 0), (0, LANES - w.shape[1])))


MLP_CHUNK = 1024


def _proj_mlp_kernel(*refs, n, final_norm):
    a_refs, w_refs = refs[:n], refs[n:2 * n]
    x_ref, g_ref, wu_ref, wd_ref, gf_ref, o_ref = refs[2 * n:]
    proj = _dot(a_refs[0][...], w_refs[0][...])
    for a_ref, w_ref in zip(a_refs[1:], w_refs[1:]):
        proj = proj + _dot(a_ref[...], w_ref[...])
    x = x_ref[...] + proj
    xn = _rms(x, g_ref[...]).astype(BF16)
    acc = x
    for c in range(wu_ref.shape[1] // MLP_CHUNK):
        cols = slice(c * MLP_CHUNK, (c + 1) * MLP_CHUNK)
        h = _dot(xn, wu_ref[:, cols])
        h = jnp.square(jnp.maximum(h, 0.0)).astype(BF16)
        acc = acc + _dot(h, wd_ref[cols, :])
    if final_norm:
        acc = _rms(acc, gf_ref[...])
    o_ref[...] = acc


def _proj_mlp(acts, ws, x, g, w_up, w_down, g_final, final_norm, tm=512):
    t, d = x.shape
    n = len(acts)
    row = lambda width: pl.BlockSpec((tm, width), lambda i: (i, 0))
    const = lambda shape: pl.BlockSpec(shape, lambda i: (0, 0))
    in_specs = ([row(a.shape[1]) for a in acts] + [const(w.shape) for w in ws]
                + [row(d), const((1, d)), _resident(w_up.shape), _resident(w_down.shape), const((1, d))])
    return pl.pallas_call(
        functools.partial(_proj_mlp_kernel, n=n, final_norm=final_norm),
        grid=(t // tm,),
        in_specs=in_specs,
        out_specs=row(d),
        out_shape=jax.ShapeDtypeStruct((t, d), F32),
        compiler_params=_cparams(("parallel",)),
        name="proj_mlp",
    )(*acts, *ws, x, g.reshape(1, d), w_up, w_down, g_final.reshape(1, d))


SSD_STEP = 8 * SSD_CHUNK
CONV_STRIDE = SSD_STEP // 8 + 1
CONV_ROWS = 8 * CONV_STRIDE
SSD_SLABS = SSD_XBC // LANES


def _ssd_kernel(z_ref, xbc_ref, dt_ref, cw_ref, cb_ref, dtb_ref, alog_ref, dskip_ref, gn_ref,
                e_ref, o_ref, ubuf, xa_scr, hst):
    L = SSD_CHUNK
    TL = SSD_STEP
    c = pl.program_id(1)

    @pl.when(c == 0)
    def _():
        hst[...] = jnp.zeros_like(hst)
        ubuf[...] = jnp.zeros_like(ubuf)

    for sl in range(SSD_SLABS):
        ubuf[sl, 8:8 + TL, :] = xbc_ref[:, sl * LANES:(sl + 1) * LANES]

    def conv_slab(sl, carry):
        w = cw_ref[sl]
        b = cb_ref[sl]
        u = [ubuf[sl, pl.ds(5 + m, 8, stride=CONV_STRIDE), :] for m in range(CONV_STRIDE + SSD_CONV - 1)]
        for r in range(CONV_STRIDE):
            acc = b
            for k in range(SSD_CONV):
                acc = acc + w[k:k + 1, :] * u[r + k]
            xa_scr[sl, pl.ds(r, 8, stride=CONV_STRIDE), :] = acc + acc * jnp.tanh(acc)
        ubuf[sl, 0:8, :] = ubuf[sl, TL:TL + 8, :]
        return carry

    lax.fori_loop(0, SSD_SLABS, conv_slab, 0)
    nxs = SSD_INNER // LANES
    ngs = SSD_GN // LANES
    row = lax.broadcasted_iota(jnp.int32, (L, L), 0)
    col = lax.broadcasted_iota(jnp.int32, (L, L), 1)
    causal = row >= col
    tril = causal.astype(BF16)
    e01 = e_ref[...]
    lane = lax.broadcasted_iota(jnp.int32, (L, LANES), 1)
    neg_a = -jnp.exp(alog_ref[...])
    gw = SSD_INNER // SSD_GROUPS
    hpl = LANES // HEAD_DIM

    for ci in range(TL // L):
        rows = slice(ci * L, (ci + 1) * L)
        xs = jnp.concatenate([xa_scr[sl, rows, :] for sl in range(nxs)], axis=1)
        dt = _softplus(dt_ref[rows, :] + dtb_ref[...])
        acum = _sel_dot_rhs(tril, dt * neg_a)
        alast = acum[L - 1:L, :]
        dtx = _dot(dt.astype(BF16), e01)
        eacx = _dot(jnp.exp(acum).astype(BF16), e01)
        dtex = _dot(jnp.exp(alast - acum).astype(BF16), e01)
        cdx = _sel_dot_lhs(jnp.exp(acum[L - 8:L, :]), e01)[7:8, :]
        xc = xs * dtx
        xcd = (xc * dtex).astype(BF16)
        acum2 = acum * LOG2E
        act2 = acum2.T

        cbs, y_offs = [], []
        for g in range(SSD_GROUPS):
            b_f32 = xa_scr[nxs + g, rows, :]
            bg = b_f32.astype(BF16)
            bgt = b_f32.T.astype(BF16)
            cg = xa_scr[nxs + ngs + g, rows, :].astype(BF16)
            cbs.append(_dot_nt(cg, bg))
            hprev = hst[g]
            y_offs.append(_dot(cg, hprev.astype(BF16)) * eacx[:, g * gw:(g + 1) * gw])
            hst[g] = hprev * cdx[:, g * gw:(g + 1) * gw] + _dot(bgt, xcd[:, g * gw:(g + 1) * gw])
        ms = []
        for h in range(SSD_HEADS):
            dec = jnp.exp2(jnp.where(causal, acum2[:, h:h + 1] - act2[h:h + 1, :], NEG_INF))
            ms.append((cbs[h * HEAD_DIM // gw] * dec).astype(BF16))
        y_diags = []
        for pr in range(SSD_INNER // LANES):
            xcp = xc[:, pr * LANES:(pr + 1) * LANES]
            xm = [jnp.where((lane >= hh * HEAD_DIM) & (lane < (hh + 1) * HEAD_DIM), xcp, 0.0).astype(BF16)
                  for hh in range(hpl)]
            y_diags.append(_dot(jnp.concatenate(ms[pr * hpl:(pr + 1) * hpl], axis=1), jnp.concatenate(xm, axis=0)))

        ppg = gw // LANES
        for g in range(SSD_GROUPS):
            lanes = slice(g * gw, (g + 1) * gw)
            y = jnp.concatenate(y_diags[g * ppg:(g + 1) * ppg], axis=1) + y_offs[g] + xs[:, lanes] * dskip_ref[:, lanes]
            yg = y * _silu(z_ref[rows, lanes])
            ms_g = jnp.mean(yg * yg, axis=-1, keepdims=True)
            o_ref[rows, lanes] = (yg * lax.rsqrt(ms_g + NORM_EPS) * gn_ref[:, lanes]).astype(o_ref.dtype)


def _pad_lanes(v):
    return jnp.pad(v.astype(F32), (0, LANES - v.shape[0])).reshape(1, LANES)


def _ssd(z, xbc, dt, conv_w, conv_b, dt_bias, a_log, d_skip, gate_norm, bsz, s):
    L = SSD_STEP
    nc = s // L
    e01 = (jnp.arange(LANES)[:, None] == (jnp.arange(SSD_INNER) // HEAD_DIM)[None, :]).astype(BF16)
    dsx = jnp.repeat(d_skip.astype(F32), HEAD_DIM).reshape(1, SSD_INNER)
    cw = (0.5 * conv_w.astype(F32)).reshape(SSD_CONV, SSD_SLABS, LANES).transpose(1, 0, 2)
    cw = jnp.pad(cw, ((0, 0), (0, 8 - SSD_CONV), (0, 0)))
    cb = (0.5 * conv_b.astype(F32)).reshape(SSD_SLABS, 1, LANES)
    tok = lambda b, c: (b * nc + c, 0)
    const = lambda b, c: (0, 0)
    const3 = lambda b, c: (0, 0, 0)
    return pl.pallas_call(
        _ssd_kernel,
        grid=(bsz, nc),
        in_specs=[pl.BlockSpec((L, SSD_INNER), tok),
                  pl.BlockSpec((L, SSD_XBC), tok),
                  pl.BlockSpec((L, LANES), tok),
                  pl.BlockSpec((SSD_SLABS, 8, LANES), const3),
                  pl.BlockSpec((SSD_SLABS, 1, LANES), const3),
                  pl.BlockSpec((1, LANES), const),
                  pl.BlockSpec((1, LANES), const),
                  pl.BlockSpec((1, SSD_INNER), const),
                  pl.BlockSpec((1, SSD_INNER), const),
                  pl.BlockSpec((LANES, SSD_INNER), const)],
        out_specs=pl.BlockSpec((L, SSD_INNER), tok),
        out_shape=jax.ShapeDtypeStruct((bsz * s, SSD_INNER), BF16),
        scratch_shapes=[pltpu.VMEM((SSD_SLABS, 8 + CONV_ROWS + 8, LANES), F32),
                        pltpu.VMEM((SSD_SLABS, CONV_ROWS, LANES), F32),
                        pltpu.VMEM((SSD_GROUPS, SSD_STATE, SSD_INNER // SSD_GROUPS), F32)],
        compiler_params=_cparams(("parallel", "arbitrary")),
        name="ssd",
    )(z, xbc, dt, cw, cb, _pad_lanes(dt_bias), _pad_lanes(a_log), dsx,
      gate_norm.astype(F32).reshape(1, SSD_INNER), e01)


def _rope(x, cos, sin_lo, sin_hi):
    half = HEAD_DIM // 2
    return x * cos + pltpu.roll(x, LANES - half, 1) * sin_lo + pltpu.roll(x, half, 1) * sin_hi


def _fold8(x, op):
    parts = [x[r * 8:(r + 1) * 8, :] for r in range(x.shape[0] // 8)]
    return _tree(parts, op)


def _tree(parts, op):
    while len(parts) > 1:
        parts = [op(parts[a], parts[a + 1]) if a + 1 < len(parts) else parts[a]
                 for a in range(0, len(parts), 2)]
    return parts[0]


def _attend(i, nh, score_fn, row_shift, vt_scr, o_ref, o_rows, st_scr, m_scr, acc_scr, ot_scr):
    tq = ATT_TILE
    krow = lax.broadcasted_iota(jnp.int32, (tq, tq), 0)
    qcol = lax.broadcasted_iota(jnp.int32, (tq, tq), 1)
    m_scr[...] = jnp.full(m_scr.shape, NEG_INF, F32)
    acc_scr[...] = jnp.zeros(acc_scr.shape, F32)

    def trip(js, diagonal_last):
        shifts, alphas = [], []
        for h in range(nh):
            folds = []
            for u, j in enumerate(js):
                s_t = score_fn(h, j)
                if diagonal_last and u == len(js) - 1:
                    s_t = jnp.where(krow <= qcol, s_t, NEG_INF)
                st_scr[h, u] = s_t
                folds.append(_fold8(s_t, jnp.maximum))
            m_trip = jnp.max(_tree(folds, jnp.maximum), axis=0, keepdims=True)
            rs = row_shift[h]
            m_old = m_scr[h]
            m_new = jnp.maximum(m_old, m_trip if rs is None else m_trip + rs)
            m_scr[h] = m_new
            shifts.append(m_new if rs is None else m_new - rs)
            alphas.append(jnp.exp2(m_old - m_new))
        for h in range(nh):
            pv = [_dot(vt_scr[j, h], jnp.exp2(st_scr[h, u] - shifts[h]).astype(BF16)) for u, j in enumerate(js)]
            acc_scr[h] = alphas[h] * acc_scr[h] + _tree(pv, jnp.add)

    group = st_scr.shape[1]
    rem = i % group
    for r in range(group):
        @pl.when(rem == r)
        def _(r=r):
            trip([i - r + u for u in range(r + 1)], True)

    def full_trip(g, carry):
        trip([group * g + u for u in range(group)], False)
        return carry

    lax.fori_loop(0, i // group, full_trip, 0)
    for h in range(nh):
        ot_scr[h * HEAD_DIM:(h + 1) * HEAD_DIM, :] = acc_scr[h, :HEAD_DIM, :] / acc_scr[h, HEAD_DIM:HEAD_DIM + 1, :]
    for g in range(nh * HEAD_DIM // LANES):
        o_ref[o_rows, g * LANES:(g + 1) * LANES] = ot_scr[g * LANES:(g + 1) * LANES, :].T.astype(o_ref.dtype)


V_ROWS = HEAD_DIM + 16
ATT_GROUPS = 2
ATT_LANES = ATT_GROUPS * LANES
ATT_NH = ATT_LANES // HEAD_DIM
ATT_TPS = 4
MOBA_TRIP = 8
FOX_TRIP = 8


def _store_vt(vt_scr, jb, v_blk):
    n = v_blk.shape[1]
    eye = (lax.broadcasted_iota(jnp.int32, (n, n), 0) == lax.broadcasted_iota(jnp.int32, (n, n), 1)).astype(BF16)
    vt = _dot_nt(eye, v_blk.astype(BF16))
    for h in range(v_blk.shape[1] // HEAD_DIM):
        vt_scr[jb, h, :HEAD_DIM, :] = vt[h * HEAD_DIM:(h + 1) * HEAD_DIM, :].astype(BF16)
        vt_scr[jb, h, HEAD_DIM:, :] = jnp.ones((V_ROWS - HEAD_DIM, v_blk.shape[0]), BF16)


def _attend_scratch(nb, group):
    tq = ATT_TILE
    return [pltpu.VMEM((nb, ATT_NH, V_ROWS, tq), BF16),
            pltpu.VMEM((ATT_NH, group, tq, tq), F32),
            pltpu.VMEM((ATT_NH, 1, tq), F32),
            pltpu.VMEM((ATT_NH, V_ROWS, tq), F32),
            pltpu.VMEM((ATT_LANES, tq), F32)]


def _masked_heads(x, scale):
    lane = lax.broadcasted_iota(jnp.int32, (x.shape[0], LANES), 1)
    out = []
    for h in range(x.shape[1] // HEAD_DIM):
        g, hh = divmod(h, LANES // HEAD_DIM)
        xg = x[:, g * LANES:(g + 1) * LANES]
        in_head = (lane >= hh * HEAD_DIM) & (lane < (hh + 1) * HEAD_DIM)
        out.append(jnp.where(in_head, xg * scale, 0.0))
    return out


def _moba_kernel(q_ref, k_ref, v_ref, cos_ref, sinl_ref, sinh_ref, o_ref, kr_scr, km_scr, bias_scr,
                 vt_scr, st_scr, m_scr, acc_scr, ot_scr):
    tq = ATT_TILE
    nb = kr_scr.shape[1]
    step = pl.program_id(2)
    scale = HEAD_DIM ** -0.5
    hpg = LANES // HEAD_DIM

    @pl.when(step == 0)
    def _():
        for jb in range(nb):
            rows = slice(jb * tq, (jb + 1) * tq)
            for g in range(ATT_GROUPS):
                lanes = slice(g * LANES, (g + 1) * LANES)
                kr = _rope(k_ref[rows, lanes], cos_ref[rows, :], sinl_ref[rows, :], sinh_ref[rows, :])
                kr_scr[g, jb] = kr.astype(BF16)
                km_scr[g, jb:jb + 1, :] = jnp.mean(kr, axis=0, keepdims=True)
            _store_vt(vt_scr, jb, v_ref[rows, :])

    def tile(t, carry):
        i = step * ATT_TPS + t
        trows = pl.ds(pl.multiple_of(t * tq, tq), tq)
        qrows = pl.ds(pl.multiple_of(i * tq, tq), tq)
        cos, sin_lo, sin_hi = cos_ref[qrows, :], sinl_ref[qrows, :], sinh_ref[qrows, :]
        qgs = [_rope(q_ref[trows, g * LANES:(g + 1) * LANES], cos, sin_lo, sin_hi) for g in range(ATT_GROUPS)]
        klane = lax.broadcasted_iota(jnp.int32, (nb, LANES), 1)
        gates = []
        for g in range(ATT_GROUPS):
            q_hi = qgs[g].astype(BF16)
            q_lo = (qgs[g] - q_hi.astype(F32)).astype(BF16)
            for hh in range(hpg):
                km = jnp.where((klane >= hh * HEAD_DIM) & (klane < (hh + 1) * HEAD_DIM), km_scr[g], 0.0)
                k_hi = km.astype(BF16)
                k_lo = (km - k_hi.astype(F32)).astype(BF16)
                gates.append(_dot_nt(k_hi, q_hi) + _dot_nt(k_hi, q_lo) + _dot_nt(k_lo, q_hi))
        jrow = lax.broadcasted_iota(jnp.int32, (nb, ATT_NH * tq), 0)
        valid = jrow < i
        gate = jnp.where(valid, jnp.concatenate(gates, axis=1), NEG_INF)
        cnt = jnp.zeros(gate.shape, F32)
        for jp in range(nb):
            gj = gate[jp:jp + 1, :]
            cnt = cnt + jnp.where((gj > gate) | ((gj == gate) & (jp < jrow)), 1.0, 0.0)
        chosen = valid & (cnt < MOBA_TOPK)
        bias = jnp.where(chosen | (jrow == i), 0.0, NEG_INF)
        for h in range(ATT_NH):
            bias_scr[h] = bias[:, h * tq:(h + 1) * tq]
        qs = [(qm * (scale * LOG2E)).astype(BF16) for qm in _masked_heads(jnp.concatenate(qgs, axis=1), 1.0)]

        def score_fn(h, j):
            return _dot_nt(kr_scr[h // hpg, j], qs[h]) + bias_scr[h, pl.ds(j, 1), :]

        _attend(i, ATT_NH, score_fn, [None] * ATT_NH, vt_scr, o_ref, trows, st_scr, m_scr, acc_scr, ot_scr)
        return carry

    lax.fori_loop(0, ATT_TPS, tile, 0)


def _moba(q, k, v, cos, sin_lo, sin_hi, bsz, s):
    tq = ATT_TILE
    nb = s // tq
    q3, k3, v3 = (a.reshape(bsz, s, MOBA_INNER) for a in (q, k, v))
    whole = pl.BlockSpec((None, s, ATT_LANES), lambda b, p, i: (b, 0, p))
    tile = pl.BlockSpec((None, ATT_TPS * tq, ATT_LANES), lambda b, p, i: (b, i, p))
    tab = pl.BlockSpec((s, LANES), lambda b, p, i: (0, 0))
    out = pl.pallas_call(
        _moba_kernel,
        grid=(bsz, MOBA_INNER // ATT_LANES, nb // ATT_TPS),
        in_specs=[tile, whole, whole, tab, tab, tab],
        out_specs=tile,
        out_shape=jax.ShapeDtypeStruct((bsz, s, MOBA_INNER), BF16),
        scratch_shapes=[pltpu.VMEM((ATT_GROUPS, nb, tq, LANES), BF16),
                        pltpu.VMEM((ATT_GROUPS, nb, LANES), F32),
                        pltpu.VMEM((ATT_NH, nb, tq), F32)] + _attend_scratch(nb, MOBA_TRIP),
        compiler_params=_cparams(("parallel", "parallel", "arbitrary")),
        name="moba",
    )(q3, k3, v3, cos, sin_lo, sin_hi)
    return out.reshape(bsz * s, MOBA_INNER)


def _fox_gate_kernel(f_ref, b_ref, c_ref, ct_ref):
    tb = ATT_TILE
    s = f_ref.shape[0]
    row = lax.broadcasted_iota(jnp.int32, (tb, tb), 0)
    col = lax.broadcasted_iota(jnp.int32, (tb, tb), 1)
    tril = (row >= col).astype(BF16)
    carry = jnp.zeros((1, LANES), F32)
    for jb in range(s // tb):
        rows = slice(jb * tb, (jb + 1) * tb)
        log_f = -_softplus(-(f_ref[rows, :] + b_ref[...]))
        cblk = _sel_dot_rhs(tril, log_f) + carry
        c_ref[rows, :] = cblk
        ct_ref[jb] = cblk.T
        carry = cblk[tb - 1:tb, :]


def _fox_gate(f, bias, bsz, s):
    return pl.pallas_call(
        _fox_gate_kernel,
        grid=(bsz,),
        in_specs=[pl.BlockSpec((None, s, LANES), lambda b: (b, 0, 0)),
                  pl.BlockSpec((1, LANES), lambda b: (0, 0))],
        out_specs=[pl.BlockSpec((None, s, LANES), lambda b: (b, 0, 0)),
                   pl.BlockSpec((None, s // ATT_TILE, LANES, ATT_TILE), lambda b: (b, 0, 0, 0))],
        out_shape=[jax.ShapeDtypeStruct((bsz, s, LANES), F32),
                   jax.ShapeDtypeStruct((bsz, s // ATT_TILE, LANES, ATT_TILE), F32)],
        compiler_params=_cparams(("parallel",)),
        name="fox_gate",
    )(f.reshape(bsz, s, LANES), _pad_lanes(bias))


def _fox_kernel(q_ref, k_ref, v_ref, c_ref, ct_ref, o_ref, k_scr, ck_scr,
                vt_scr, st_scr, m_scr, acc_scr, ot_scr):
    tq = ATT_TILE
    nb = k_scr.shape[1]
    p = pl.program_id(1)
    step = pl.program_id(2)
    scale = HEAD_DIM ** -0.5
    hpg = LANES // HEAD_DIM

    @pl.when(step == 0)
    def _():
        lane = lax.broadcasted_iota(jnp.int32, (tq, LANES), 1)
        for jb in range(nb):
            rows = slice(jb * tq, (jb + 1) * tq)
            for g in range(ATT_GROUPS):
                k_scr[g, jb] = k_ref[rows, g * LANES:(g + 1) * LANES].astype(BF16)
            _store_vt(vt_scr, jb, v_ref[rows, :])
            cblk = c_ref[rows, :] * LOG2E
            for h in range(ATT_NH):
                colv = jnp.sum(jnp.where(lane == p * ATT_NH + h, cblk, 0.0), axis=1, keepdims=True)
                ck_scr[h, jb] = jnp.broadcast_to(colv, (tq, LANES))

    def tile(t, carry):
        i = step * ATT_TPS + t
        trows = pl.ds(pl.multiple_of(t * tq, tq), tq)
        qs = [qm.astype(BF16) for qm in _masked_heads(q_ref[trows, :], scale * LOG2E)]
        cqs = [ct_ref[t, pl.ds(p * ATT_NH + h, 1), :] * LOG2E for h in range(ATT_NH)]

        def score_fn(h, j):
            ck = ck_scr[h, j]
            return _dot_nt(k_scr[h // hpg, j], qs[h]) - jnp.concatenate([ck] * (tq // LANES), axis=1)

        _attend(i, ATT_NH, score_fn, cqs, vt_scr, o_ref, trows, st_scr, m_scr, acc_scr, ot_scr)
        return carry

    lax.fori_loop(0, ATT_TPS, tile, 0)


def _fox(q, k, v, c, ct, bsz, s):
    tq = ATT_TILE
    nb = s // tq
    q3, k3, v3 = (a.reshape(bsz, s, FOX_INNER) for a in (q, k, v))
    whole = pl.BlockSpec((None, s, ATT_LANES), lambda b, p, i: (b, 0, p))
    tile = pl.BlockSpec((None, ATT_TPS * tq, ATT_LANES), lambda b, p, i: (b, i, p))
    out = pl.pallas_call(
        _fox_kernel,
        grid=(bsz, FOX_INNER // ATT_LANES, nb // ATT_TPS),
        in_specs=[tile, whole, whole,
                  pl.BlockSpec((None, s, LANES), lambda b, p, i: (b, 0, 0)),
                  pl.BlockSpec((None, ATT_TPS, LANES, tq), lambda b, p, i: (b, i, 0, 0))],
        out_specs=tile,
        out_shape=jax.ShapeDtypeStruct((bsz, s, FOX_INNER), BF16),
        scratch_shapes=[pltpu.VMEM((ATT_GROUPS, nb, tq, LANES), BF16),
                        pltpu.VMEM((ATT_NH, nb, tq, LANES), F32)] + _attend_scratch(nb, FOX_TRIP),
        compiler_params=_cparams(("parallel", "parallel", "arbitrary")),
        name="fox",
    )(q3, k3, v3, c, ct)
    return out.reshape(bsz * s, FOX_INNER)


def _rope_tables(s):
    half = HEAD_DIM // 2
    inv = jnp.power(ROPE_THETA, -jnp.arange(half, dtype=F32) / half)
    ang = jnp.arange(s, dtype=F32)[:, None] * inv[None, :]
    cos, sin = jnp.cos(ang), jnp.sin(ang)
    reps = LANES // HEAD_DIM
    zero = jnp.zeros_like(sin)
    return (jnp.tile(jnp.concatenate([cos, cos], axis=1), (1, reps)),
            jnp.tile(jnp.concatenate([-sin, zero], axis=1), (1, reps)),
            jnp.tile(jnp.concatenate([zero, sin], axis=1), (1, reps)))


def _even_layer(x, bsz, s, g, w_in, conv_w, conv_b, dt_bias, a_log, d_skip, gate_norm, w_out, rope_tabs):
    o1 = SSD_INNER
    o2 = o1 + SSD_XBC
    o3 = o2 + SSD_HEADS
    ws = [w_in[:, :o2].astype(BF16), w_in[:, o3:].astype(BF16), _pad_cols(w_in[:, o2:o3]).astype(BF16)]
    z, xbc, q, k, v, dt = _norm_matmul(
        x, g, ws, ((SSD_INNER, SSD_XBC), (MOBA_INNER, MOBA_INNER, MOBA_INNER), (LANES,)))
    y_ssd = _ssd(z, xbc, dt, conv_w, conv_b, dt_bias, a_log, d_skip, gate_norm, bsz, s)
    y_att = _moba(q, k, v, *rope_tabs, bsz, s)
    w_o = w_out.astype(BF16)
    return [y_ssd, y_att], [w_o[:SSD_INNER], w_o[SSD_INNER:]]


def _odd_layer(x, bsz, s, g, w_in, fgate_bias, w_out):
    qkv = 3 * FOX_INNER
    ws = [w_in[:, :qkv].astype(BF16), _pad_cols(w_in[:, qkv:]).astype(BF16)]
    q, k, v, f = _norm_matmul(x, g, ws, ((FOX_INNER, FOX_INNER, FOX_INNER), (LANES,)))
    c, ct = _fox_gate(f, fgate_bias, bsz, s)
    y = _fox(q, k, v, c, ct, bsz, s)
    return [y], [w_out.astype(BF16)]


def kernel(x, norm_mix_even, w_in_even, conv_w, conv_b, dt_bias, a_log, d_skip, ssd_gate_norm, w_out_even,
           norm_mix_odd, w_in_odd, fgate_bias, w_out_odd, norm_mlp, w_up, w_down, final_norm):
    bsz, s, d = x.shape
    depth = norm_mlp.shape[0]
    assert s % (ATT_TILE * ATT_TPS) == 0 and s % SSD_STEP == 0
    h = x.reshape(bsz * s, d)
    rope_tabs = _rope_tables(s)
    for layer in range(depth):
        i = layer // 2
        if layer % 2 == 0:
            acts, ws = _even_layer(h, bsz, s, norm_mix_even[i], w_in_even[i], conv_w[i], conv_b[i], dt_bias[i],
                            a_log[i], d_skip[i], ssd_gate_norm[i], w_out_even[i], rope_tabs)
        else:
            acts, ws = _odd_layer(h, bsz, s, norm_mix_odd[i], w_in_odd[i], fgate_bias[i], w_out_odd[i])
        h = _proj_mlp(acts, ws, h, norm_mlp[layer], w_up[layer].astype(BF16), w_down[layer].astype(BF16),
                      final_norm, final_norm=(layer == depth - 1))
    return h.reshape(bsz, s, d)
```

```python
import functools
import math

import jax
import jax.numpy as jnp
from jax import lax
from jax.experimental import pallas as pl
from jax.experimental.pallas import tpu as pltpu

NORM_EPS = 1e-5
ROPE_THETA = 10000.0
LANES = 128
HEAD_DIM = 64
SSD_HEADS = 16
SSD_GROUPS = 4
SSD_STATE = 128
SSD_CONV = 4
SSD_CHUNK = 128
SSD_INNER = SSD_HEADS * HEAD_DIM
SSD_GN = SSD_GROUPS * SSD_STATE
SSD_XBC = SSD_INNER + 2 * SSD_GN
MOBA_HEADS = 8
MOBA_INNER = MOBA_HEADS * HEAD_DIM
MOBA_BLOCK = 256
MOBA_TOPK = 3
FOX_HEADS = 16
FOX_INNER = FOX_HEADS * HEAD_DIM
ATT_TILE = MOBA_BLOCK
VMEM_LIMIT = 56 * 1024 * 1024

F32 = jnp.float32
BF16 = jnp.bfloat16
NEG_INF = float("-inf")
LOG2E = math.log2(math.e)


def _cparams(sem):
    return pltpu.CompilerParams(dimension_semantics=sem, vmem_limit_bytes=VMEM_LIMIT)


def _resident(shape):
    return pl.BlockSpec(shape, lambda i: (0,) * len(shape), pipeline_mode=pl.Buffered(1))


def _rms(x, g):
    return x * lax.rsqrt(jnp.mean(x * x, axis=-1, keepdims=True) + NORM_EPS) * g


def _silu(x):
    h = 0.5 * x
    return h + h * jnp.tanh(h)


def _softplus(x):
    return jnp.maximum(x, 0.0) + jnp.log1p(jnp.exp(-jnp.abs(x)))


def _split3(x):
    hi = x.astype(BF16)
    r = x - hi.astype(F32)
    mid = r.astype(BF16)
    lo = (r - mid.astype(F32)).astype(BF16)
    return hi, mid, lo


def _dot(a, b):
    return jnp.dot(a, b, preferred_element_type=F32)


def _dot_nt(a, b):
    return lax.dot_general(a, b, (((1,), (1,)), ((), ())), preferred_element_type=F32)


def _sel_dot_rhs(m01, x):
    hi, mid, lo = _split3(x)
    return _dot(m01, hi) + _dot(m01, mid) + _dot(m01, lo)


def _sel_dot_lhs(x, m01):
    hi, mid, lo = _split3(x)
    return _dot(hi, m01) + _dot(mid, m01) + _dot(lo, m01)


def _norm_matmul_kernel(*refs, splits, nw):
    x_ref, g_ref = refs[:2]
    w_refs, out_refs = refs[2:2 + nw], refs[2 + nw:]
    half = x_ref.shape[0] // 2
    for r in range(2):
        rows = slice(r * half, (r + 1) * half)
        xn = _rms(x_ref[rows, :], g_ref[...]).astype(BF16)
        for o_ref, (wi, a, b) in zip(out_refs, splits):
            o_ref[rows, :] = _dot(xn, w_refs[wi][:, a:b]).astype(o_ref.dtype)


def _norm_matmul(x, g, ws, widths, tm=512):
    t, d = x.shape
    splits = []
    for wi, (w, wds) in enumerate(zip(ws, widths)):
        a = 0
        for wd in wds:
            splits.append((wi, a, a + wd))
            a += wd
        assert a == w.shape[1]
    assert t % tm == 0
    out_w = [b - a for _, a, b in splits]
    return pl.pallas_call(
        functools.partial(_norm_matmul_kernel, splits=tuple(splits), nw=len(ws)),
        grid=(t // tm,),
        in_specs=[pl.BlockSpec((tm, d), lambda i: (i, 0)), _resident((1, d))] + [_resident(w.shape) for w in ws],
        out_specs=[pl.BlockSpec((tm, wd), lambda i: (i, 0)) for wd in out_w],
        out_shape=[jax.ShapeDtypeStruct((t, wd), F32) for wd in out_w],
        compiler_params=_cparams(("parallel",)),
        name="norm_matmul",
    )(x, g.reshape(1, d), *ws)


def _pad_cols(w):
    return jnp.pad(w, ((0, 0), (0, LANES - w.shape[1])))


MLP_CHUNK = 1024


def _proj_mlp_kernel(*refs, n, final_norm):
    a_refs, w_refs = refs[:n], refs[n:2 * n]
    x_ref, g_ref, wu_ref, wd_ref, gf_ref, o_ref = refs[2 * n:]
    proj = _dot(a_refs[0][...], w_refs[0][...])
    for a_ref, w_ref in zip(a_refs[1:], w_refs[1:]):
        proj = proj + _dot(a_ref[...], w_ref[...])
    x = x_ref[...] + proj
    xn = _rms(x, g_ref[...]).astype(BF16)
    acc = x
    for c in range(wu_ref.shape[1] // MLP_CHUNK):
        cols = slice(c * MLP_CHUNK, (c + 1) * MLP_CHUNK)
        h = _dot(xn, wu_ref[:, cols])
        h = jnp.square(jnp.maximum(h, 0.0)).astype(BF16)
        acc = acc + _dot(h, wd_ref[cols, :])
    if final_norm:
        acc = _rms(acc, gf_ref[...])
    o_ref[...] = acc


def _proj_mlp(acts, ws, x, g, w_up, w_down, g_final, final_norm, tm=512):
    t, d = x.shape
    n = len(acts)
    row = lambda width: pl.BlockSpec((tm, width), lambda i: (i, 0))
    const = lambda shape: pl.BlockSpec(shape, lambda i: (0, 0))
    in_specs = ([row(a.shape[1]) for a in acts] + [const(w.shape) for w in ws]
                + [row(d), const((1, d)), _resident(w_up.shape), _resident(w_down.shape), const((1, d))])
    return pl.pallas_call(
        functools.partial(_proj_mlp_kernel, n=n, final_norm=final_norm),
        grid=(t // tm,),
        in_specs=in_specs,
        out_specs=row(d),
        out_shape=jax.ShapeDtypeStruct((t, d), F32),
        compiler_params=_cparams(("parallel",)),
        name="proj_mlp",
    )(*acts, *ws, x, g.reshape(1, d), w_up, w_down, g_final.reshape(1, d))


SSD_STEP = 8 * SSD_CHUNK
CONV_STRIDE = SSD_STEP // 8 + 1
CONV_ROWS = 8 * CONV_STRIDE
SSD_SLABS = SSD_XBC // LANES


def _ssd_kernel(z_ref, xbc_ref, dt_ref, cw_ref, cb_ref, dtb_ref, alog_ref, dskip_ref, gn_ref,
                e_ref, o_ref, ubuf, xa_scr, hst):
    L = SSD_CHUNK
    TL = SSD_STEP
    c = pl.program_id(1)

    @pl.when(c == 0)
    def _():
        hst[...] = jnp.zeros_like(hst)
        ubuf[...] = jnp.zeros_like(ubuf)

    for sl in range(SSD_SLABS):
        ubuf[sl, 8:8 + TL, :] = xbc_ref[:, sl * LANES:(sl + 1) * LANES]

    def conv_slab(sl, carry):
        w = cw_ref[sl]
        b = cb_ref[sl]
        u = [ubuf[sl, pl.ds(5 + m, 8, stride=CONV_STRIDE), :] for m in range(CONV_STRIDE + SSD_CONV - 1)]
        for r in range(CONV_STRIDE):
            acc = b
            for k in range(SSD_CONV):
                acc = acc + w[k:k + 1, :] * u[r + k]
            xa_scr[sl, pl.ds(r, 8, stride=CONV_STRIDE), :] = acc + acc * jnp.tanh(acc)
        ubuf[sl, 0:8, :] = ubuf[sl, TL:TL + 8, :]
        return carry

    lax.fori_loop(0, SSD_SLABS, conv_slab, 0)
    nxs = SSD_INNER // LANES
    ngs = SSD_GN // LANES
    row = lax.broadcasted_iota(jnp.int32, (L, L), 0)
    col = lax.broadcasted_iota(jnp.int32, (L, L), 1)
    causal = row >= col
    tril = causal.astype(BF16)
    e01 = e_ref[...]
    lane = lax.broadcasted_iota(jnp.int32, (L, LANES), 1)
    neg_a = -jnp.exp(alog_ref[...])
    gw = SSD_INNER // SSD_GROUPS
    hpl = LANES // HEAD_DIM

    for ci in range(TL // L):
        rows = slice(ci * L, (ci + 1) * L)
        xs = jnp.concatenate([xa_scr[sl, rows, :] for sl in range(nxs)], axis=1)
        dt = _softplus(dt_ref[rows, :] + dtb_ref[...])
        acum = _sel_dot_rhs(tril, dt * neg_a)
        alast = acum[L - 1:L, :]
        dtx = _dot(dt.astype(BF16), e01)
        eacx = _dot(jnp.exp(acum).astype(BF16), e01)
        dtex = _dot(jnp.exp(alast - acum).astype(BF16), e01)
        cdx = _sel_dot_lhs(jnp.exp(acum[L - 8:L, :]), e01)[7:8, :]
        xc = xs * dtx
        xcd = (xc * dtex).astype(BF16)
        acum2 = acum * LOG2E
        act2 = acum2.T

        cbs, y_offs = [], []
        for g in range(SSD_GROUPS):
            b_f32 = xa_scr[nxs + g, rows, :]
            bg = b_f32.astype(BF16)
            bgt = b_f32.T.astype(BF16)
            cg = xa_scr[nxs + ngs + g, rows, :].astype(BF16)
            cbs.append(_dot_nt(cg, bg))
            hprev = hst[g]
            y_offs.append(_dot(cg, hprev.astype(BF16)) * eacx[:, g * gw:(g + 1) * gw])
            hst[g] = hprev * cdx[:, g * gw:(g + 1) * gw] + _dot(bgt, xcd[:, g * gw:(g + 1) * gw])
        ms = []
        for h in range(SSD_HEADS):
            dec = jnp.exp2(jnp.where(causal, acum2[:, h:h + 1] - act2[h:h + 1, :], NEG_INF))
            ms.append((cbs[h * HEAD_DIM // gw] * dec).astype(BF16))
        y_diags = []
        for pr in range(SSD_INNER // LANES):
            xcp = xc[:, pr * LANES:(pr + 1) * LANES]
            xm = [jnp.where((lane >= hh * HEAD_DIM) & (lane < (hh + 1) * HEAD_DIM), xcp, 0.0).astype(BF16)
                  for hh in range(hpl)]
            y_diags.append(_dot(jnp.concatenate(ms[pr * hpl:(pr + 1) * hpl], axis=1), jnp.concatenate(xm, axis=0)))

        ppg = gw // LANES
        for g in range(SSD_GROUPS):
            lanes = slice(g * gw, (g + 1) * gw)
            y = jnp.concatenate(y_diags[g * ppg:(g + 1) * ppg], axis=1) + y_offs[g] + xs[:, lanes] * dskip_ref[:, lanes]
            yg = y * _silu(z_ref[rows, lanes])
            ms_g = jnp.mean(yg * yg, axis=-1, keepdims=True)
            o_ref[rows, lanes] = (yg * lax.rsqrt(ms_g + NORM_EPS) * gn_ref[:, lanes]).astype(o_ref.dtype)


def _pad_lanes(v):
    return jnp.pad(v.astype(F32), (0, LANES - v.shape[0])).reshape(1, LANES)


def _ssd(z, xbc, dt, conv_w, conv_b, dt_bias, a_log, d_skip, gate_norm, bsz, s):
    L = SSD_STEP
    nc = s // L
    e01 = (jnp.arange(LANES)[:, None] == (jnp.arange(SSD_INNER) // HEAD_DIM)[None, :]).astype(BF16)
    dsx = jnp.repeat(d_skip.astype(F32), HEAD_DIM).reshape(1, SSD_INNER)
    cw = (0.5 * conv_w.astype(F32)).reshape(SSD_CONV, SSD_SLABS, LANES).transpose(1, 0, 2)
    cw = jnp.pad(cw, ((0, 0), (0, 8 - SSD_CONV), (0, 0)))
    cb = (0.5 * conv_b.astype(F32)).reshape(SSD_SLABS, 1, LANES)
    tok = lambda b, c: (b * nc + c, 0)
    const = lambda b, c: (0, 0)
    const3 = lambda b, c: (0, 0, 0)
    return pl.pallas_call(
        _ssd_kernel,
        grid=(bsz, nc),
        in_specs=[pl.BlockSpec((L, SSD_INNER), tok),
                  pl.BlockSpec((L, SSD_XBC), tok),
                  pl.BlockSpec((L, LANES), tok),
                  pl.BlockSpec((SSD_SLABS, 8, LANES), const3),
                  pl.BlockSpec((SSD_SLABS, 1, LANES), const3),
                  pl.BlockSpec((1, LANES), const),
                  pl.BlockSpec((1, LANES), const),
                  pl.BlockSpec((1, SSD_INNER), const),
                  pl.BlockSpec((1, SSD_INNER), const),
                  pl.BlockSpec((LANES, SSD_INNER), const)],
        out_specs=pl.BlockSpec((L, SSD_INNER), tok),
        out_shape=jax.ShapeDtypeStruct((bsz * s, SSD_INNER), BF16),
        scratch_shapes=[pltpu.VMEM((SSD_SLABS, 8 + CONV_ROWS + 8, LANES), F32),
                        pltpu.VMEM((SSD_SLABS, CONV_ROWS, LANES), F32),
                        pltpu.VMEM((SSD_GROUPS, SSD_STATE, SSD_INNER // SSD_GROUPS), F32)],
        compiler_params=_cparams(("parallel", "arbitrary")),
        name="ssd",
    )(z, xbc, dt, cw, cb, _pad_lanes(dt_bias), _pad_lanes(a_log), dsx,
      gate_norm.astype(F32).reshape(1, SSD_INNER), e01)


def _rope(x, cos, sin_lo, sin_hi):
    half = HEAD_DIM // 2
    return x * cos + pltpu.roll(x, LANES - half, 1) * sin_lo + pltpu.roll(x, half, 1) * sin_hi


def _fold8(x, op):
    parts = [x[r * 8:(r + 1) * 8, :] for r in range(x.shape[0] // 8)]
    return _tree(parts, op)


def _tree(parts, op):
    while len(parts) > 1:
        parts = [op(parts[a], parts[a + 1]) if a + 1 < len(parts) else parts[a]
                 for a in range(0, len(parts), 2)]
    return parts[0]


def _attend(i, nh, score_fn, row_shift, vt_scr, o_ref, o_rows, st_scr, m_scr, acc_scr, ot_scr):
    tq = ATT_TILE
    krow = lax.broadcasted_iota(jnp.int32, (tq, tq), 0)
    qcol = lax.broadcasted_iota(jnp.int32, (tq, tq), 1)
    m_scr[...] = jnp.full(m_scr.shape, NEG_INF, F32)
    acc_scr[...] = jnp.zeros(acc_scr.shape, F32)

    def trip(js, diagonal_last):
        shifts, alphas = [], []
        for h in range(nh):
            folds = []
            for u, j in enumerate(js):
                s_t = score_fn(h, j)
                if diagonal_last and u == len(js) - 1:
                    s_t = jnp.where(krow <= qcol, s_t, NEG_INF)
                st_scr[h, u] = s_t
                folds.append(_fold8(s_t, jnp.maximum))
            m_trip = jnp.max(_tree(folds, jnp.maximum), axis=0, keepdims=True)
            rs = row_shift[h]
            m_old = m_scr[h]
            m_new = jnp.maximum(m_old, m_trip if rs is None else m_trip + rs)
            m_scr[h] = m_new
            shifts.append(m_new if rs is None else m_new - rs)
            alphas.append(jnp.exp2(m_old - m_new))
        for h in range(nh):
            pv = [_dot(vt_scr[j, h], jnp.exp2(st_scr[h, u] - shifts[h]).astype(BF16)) for u, j in enumerate(js)]
            acc_scr[h] = alphas[h] * acc_scr[h] + _tree(pv, jnp.add)

    group = st_scr.shape[1]
    rem = i % group
    for r in range(group):
        @pl.when(rem == r)
        def _(r=r):
            trip([i - r + u for u in range(r + 1)], True)

    def full_trip(g, carry):
        trip([group * g + u for u in range(group)], False)
        return carry

    lax.fori_loop(0, i // group, full_trip, 0)
    for h in range(nh):
        ot_scr[h * HEAD_DIM:(h + 1) * HEAD_DIM, :] = acc_scr[h, :HEAD_DIM, :] / acc_scr[h, HEAD_DIM:HEAD_DIM + 1, :]
    for g in range(nh * HEAD_DIM // LANES):
        o_ref[o_rows, g * LANES:(g + 1) * LANES] = ot_scr[g * LANES:(g + 1) * LANES, :].T.astype(o_ref.dtype)


V_ROWS = HEAD_DIM + 16
ATT_GROUPS = 2
ATT_LANES = ATT_GROUPS * LANES
ATT_NH = ATT_LANES // HEAD_DIM
ATT_TPS = 4
MOBA_TRIP = 8
FOX_TRIP = 8


def _store_vt(vt_scr, jb, v_blk):
    n = v_blk.shape[1]
    eye = (lax.broadcasted_iota(jnp.int32, (n, n), 0) == lax.broadcasted_iota(jnp.int32, (n, n), 1)).astype(BF16)
    vt = _dot_nt(eye, v_blk.astype(BF16))
    for h in range(v_blk.shape[1] // HEAD_DIM):
        vt_scr[jb, h, :HEAD_DIM, :] = vt[h * HEAD_DIM:(h + 1) * HEAD_DIM, :].astype(BF16)
        vt_scr[jb, h, HEAD_DIM:, :] = jnp.ones((V_ROWS - HEAD_DIM, v_blk.shape[0]), BF16)


def _attend_scratch(nb, group):
    tq = ATT_TILE
    return [pltpu.VMEM((nb, ATT_NH, V_ROWS, tq), BF16),
            pltpu.VMEM((ATT_NH, group, tq, tq), F32),
            pltpu.VMEM((ATT_NH, 1, tq), F32),
            pltpu.VMEM((ATT_NH, V_ROWS, tq), F32),
            pltpu.VMEM((ATT_LANES, tq), F32)]


def _masked_heads(x, scale):
    lane = lax.broadcasted_iota(jnp.int32, (x.shape[0], LANES), 1)
    out = []
    for h in range(x.shape[1] // HEAD_DIM):
        g, hh = divmod(h, LANES // HEAD_DIM)
        xg = x[:, g * LANES:(g + 1) * LANES]
        in_head = (lane >= hh * HEAD_DIM) & (lane < (hh + 1) * HEAD_DIM)
        out.append(jnp.where(in_head, xg * scale, 0.0))
    return out


def _moba_kernel(q_ref, k_ref, v_ref, cos_ref, sinl_ref, sinh_ref, o_ref, kr_scr, km_scr, bias_scr, qs_scr,
                 vt_scr, st_scr, m_scr, acc_scr, ot_scr):
    tq = ATT_TILE
    nb = kr_scr.shape[1]
    step = pl.program_id(2)
    scale = HEAD_DIM ** -0.5
    hpg = LANES // HEAD_DIM

    @pl.when(step == 0)
    def _():
        for jb in range(nb):
            rows = slice(jb * tq, (jb + 1) * tq)
            for g in range(ATT_GROUPS):
                lanes = slice(g * LANES, (g + 1) * LANES)
                kr = _rope(k_ref[rows, lanes], cos_ref[rows, :], sinl_ref[rows, :], sinh_ref[rows, :])
                kr_scr[g, jb] = kr.astype(BF16)
                km_scr[g, jb:jb + 1, :] = jnp.mean(kr, axis=0, keepdims=True)
            _store_vt(vt_scr, jb, v_ref[rows, :])

    klane = lax.broadcasted_iota(jnp.int32, (nb, LANES), 1)
    jrow = lax.broadcasted_iota(jnp.int32, (nb, ATT_NH * tq), 0)
    kms = []
    for g in range(ATT_GROUPS):
        for hh in range(hpg):
            km = jnp.where((klane >= hh * HEAD_DIM) & (klane < (hh + 1) * HEAD_DIM), km_scr[g], 0.0)
            k_hi = km.astype(BF16)
            kms.append((k_hi, (km - k_hi.astype(F32)).astype(BF16)))
    for t in range(ATT_TPS):
        i = step * ATT_TPS + t
        trows = slice(t * tq, (t + 1) * tq)
        qrows = pl.ds(pl.multiple_of(i * tq, tq), tq)
        cos, sin_lo, sin_hi = cos_ref[qrows, :], sinl_ref[qrows, :], sinh_ref[qrows, :]
        qgs = [_rope(q_ref[trows, g * LANES:(g + 1) * LANES], cos, sin_lo, sin_hi) for g in range(ATT_GROUPS)]
        gates = []
        for g in range(ATT_GROUPS):
            q_hi = qgs[g].astype(BF16)
            q_lo = (qgs[g] - q_hi.astype(F32)).astype(BF16)
            for hh in range(hpg):
                k_hi, k_lo = kms[g * hpg + hh]
                gates.append(_dot_nt(k_hi, q_hi) + _dot_nt(k_hi, q_lo) + _dot_nt(k_lo, q_hi))
        valid = jrow < i
        gate = jnp.where(valid, jnp.concatenate(gates, axis=1), NEG_INF)
        cnt = jnp.zeros(gate.shape, F32)
        for jp in range(nb):
            gj = gate[jp:jp + 1, :]
            cnt = cnt + jnp.where((gj > gate) | ((gj == gate) & (jp < jrow)), 1.0, 0.0)
        chosen = valid & (cnt < MOBA_TOPK)
        bias = jnp.where(chosen | (jrow == i), 0.0, NEG_INF)
        for h, qm in enumerate(_masked_heads(jnp.concatenate(qgs, axis=1), 1.0)):
            bias_scr[t, h] = bias[:, h * tq:(h + 1) * tq]
            qs_scr[t, h] = (qm * (scale * LOG2E)).astype(BF16)

    def tile(t, carry):
        i = step * ATT_TPS + t
        trows = pl.ds(pl.multiple_of(t * tq, tq), tq)

        def score_fn(h, j):
            return _dot_nt(kr_scr[h // hpg, j], qs_scr[t, h]) + bias_scr[t, h, pl.ds(j, 1), :]

        _attend(i, ATT_NH, score_fn, [None] * ATT_NH, vt_scr, o_ref, trows, st_scr, m_scr, acc_scr, ot_scr)
        return carry

    lax.fori_loop(0, ATT_TPS, tile, 0)


def _moba(q, k, v, cos, sin_lo, sin_hi, bsz, s):
    tq = ATT_TILE
    nb = s // tq
    q3, k3, v3 = (a.reshape(bsz, s, MOBA_INNER) for a in (q, k, v))
    whole = pl.BlockSpec((None, s, ATT_LANES), lambda b, p, i: (b, 0, p))
    tile = pl.BlockSpec((None, ATT_TPS * tq, ATT_LANES), lambda b, p, i: (b, i, p))
    tab = pl.BlockSpec((s, LANES), lambda b, p, i: (0, 0))
    out = pl.pallas_call(
        _moba_kernel,
        grid=(bsz, MOBA_INNER // ATT_LANES, nb // ATT_TPS),
        in_specs=[tile, whole, whole, tab, tab, tab],
        out_specs=tile,
        out_shape=jax.ShapeDtypeStruct((bsz, s, MOBA_INNER), BF16),
        scratch_shapes=[pltpu.VMEM((ATT_GROUPS, nb, tq, LANES), BF16),
                        pltpu.VMEM((ATT_GROUPS, nb, LANES), F32),
                        pltpu.VMEM((ATT_TPS, ATT_NH, nb, tq), F32),
                        pltpu.VMEM((ATT_TPS, ATT_NH, tq, LANES), BF16)] + _attend_scratch(nb, MOBA_TRIP),
        compiler_params=_cparams(("parallel", "parallel", "arbitrary")),
        name="moba",
    )(q3, k3, v3, cos, sin_lo, sin_hi)
    return out.reshape(bsz * s, MOBA_INNER)


def _fox_gate_kernel(f_ref, b_ref, c_ref, ct_ref):
    tb = ATT_TILE
    s = f_ref.shape[0]
    row = lax.broadcasted_iota(jnp.int32, (tb, tb), 0)
    col = lax.broadcasted_iota(jnp.int32, (tb, tb), 1)
    tril = (row >= col).astype(BF16)
    carry = jnp.zeros((1, LANES), F32)
    for jb in range(s // tb):
        rows = slice(jb * tb, (jb + 1) * tb)
        log_f = -_softplus(-(f_ref[rows, :] + b_ref[...]))
        cblk = _sel_dot_rhs(tril, log_f) + carry
        c_ref[rows, :] = cblk
        ct_ref[jb] = cblk.T
        carry = cblk[tb - 1:tb, :]


def _fox_gate(f, bias, bsz, s):
    return pl.pallas_call(
        _fox_gate_kernel,
        grid=(bsz,),
        in_specs=[pl.BlockSpec((None, s, LANES), lambda b: (b, 0, 0)),
                  pl.BlockSpec((1, LANES), lambda b: (0, 0))],
        out_specs=[pl.BlockSpec((None, s, LANES), lambda b: (b, 0, 0)),
                   pl.BlockSpec((None, s // ATT_TILE, LANES, ATT_TILE), lambda b: (b, 0, 0, 0))],
        out_shape=[jax.ShapeDtypeStruct((bsz, s, LANES), F32),
                   jax.ShapeDtypeStruct((bsz, s // ATT_TILE, LANES, ATT_TILE), F32)],
        compiler_params=_cparams(("parallel",)),
        name="fox_gate",
    )(f.reshape(bsz, s, LANES), _pad_lanes(bias))


def _fox_kernel(q_ref, k_ref, v_ref, c_ref, ct_ref, o_ref, k_scr, ck_scr,
                vt_scr, st_scr, m_scr, acc_scr, ot_scr):
    tq = ATT_TILE
    nb = k_scr.shape[1]
    p = pl.program_id(1)
    step = pl.program_id(2)
    scale = HEAD_DIM ** -0.5
    hpg = LANES // HEAD_DIM

    @pl.when(step == 0)
    def _():
        lane = lax.broadcasted_iota(jnp.int32, (tq, LANES), 1)
        for jb in range(nb):
            rows = slice(jb * tq, (jb + 1) * tq)
            for g in range(ATT_GROUPS):
                k_scr[g, jb] = k_ref[rows, g * LANES:(g + 1) * LANES].astype(BF16)
            _store_vt(vt_scr, jb, v_ref[rows, :])
            cblk = c_ref[rows, :] * LOG2E
            for h in range(ATT_NH):
                colv = jnp.sum(jnp.where(lane == p * ATT_NH + h, cblk, 0.0), axis=1, keepdims=True)
                ck_scr[h, jb] = jnp.broadcast_to(colv, (tq, LANES))

    def tile(t, carry):
        i = step * ATT_TPS + t
        trows = pl.ds(pl.multiple_of(t * tq, tq), tq)
        qs = [qm.astype(BF16) for qm in _masked_heads(q_ref[trows, :], scale * LOG2E)]
        cqs = [ct_ref[t, pl.ds(p * ATT_NH + h, 1), :] * LOG2E for h in range(ATT_NH)]

        def score_fn(h, j):
            ck = ck_scr[h, j]
            return _dot_nt(k_scr[h // hpg, j], qs[h]) - jnp.concatenate([ck] * (tq // LANES), axis=1)

        _attend(i, ATT_NH, score_fn, cqs, vt_scr, o_ref, trows, st_scr, m_scr, acc_scr, ot_scr)
        return carry

    lax.fori_loop(0, ATT_TPS, tile, 0)


def _fox(q, k, v, c, ct, bsz, s):
    tq = ATT_TILE
    nb = s // tq
    q3, k3, v3 = (a.reshape(bsz, s, FOX_INNER) for a in (q, k, v))
    whole = pl.BlockSpec((None, s, ATT_LANES), lambda b, p, i: (b, 0, p))
    tile = pl.BlockSpec((None, ATT_TPS * tq, ATT_LANES), lambda b, p, i: (b, i, p))
    out = pl.pallas_call(
        _fox_kernel,
        grid=(bsz, FOX_INNER // ATT_LANES, nb // ATT_TPS),
        in_specs=[tile, whole, whole,
                  pl.BlockSpec((None, s, LANES), lambda b, p, i: (b, 0, 0)),
                  pl.BlockSpec((None, ATT_TPS, LANES, tq), lambda b, p, i: (b, i, 0, 0))],
        out_specs=tile,
        out_shape=jax.ShapeDtypeStruct((bsz, s, FOX_INNER), BF16),
        scratch_shapes=[pltpu.VMEM((ATT_GROUPS, nb, tq, LANES), BF16),
                        pltpu.VMEM((ATT_NH, nb, tq, LANES), F32)] + _attend_scratch(nb, FOX_TRIP),
        compiler_params=_cparams(("parallel", "parallel", "arbitrary")),
        name="fox",
    )(q3, k3, v3, c, ct)
    return out.reshape(bsz * s, FOX_INNER)


def _rope_tables(s):
    half = HEAD_DIM // 2
    inv = jnp.power(ROPE_THETA, -jnp.arange(half, dtype=F32) / half)
    ang = jnp.arange(s, dtype=F32)[:, None] * inv[None, :]
    cos, sin = jnp.cos(ang), jnp.sin(ang)
    reps = LANES // HEAD_DIM
    zero = jnp.zeros_like(sin)
    return (jnp.tile(jnp.concatenate([cos, cos], axis=1), (1, reps)),
            jnp.tile(jnp.concatenate([-sin, zero], axis=1), (1, reps)),
            jnp.tile(jnp.concatenate([zero, sin], axis=1), (1, reps)))


def _even_layer(x, bsz, s, g, w_in, conv_w, conv_b, dt_bias, a_log, d_skip, gate_norm, w_out, rope_tabs):
    o1 = SSD_INNER
    o2 = o1 + SSD_XBC
    o3 = o2 + SSD_HEADS
    ws = [w_in[:, :o2].astype(BF16), w_in[:, o3:].astype(BF16), _pad_cols(w_in[:, o2:o3]).astype(BF16)]
    z, xbc, q, k, v, dt = _norm_matmul(
        x, g, ws, ((SSD_INNER, SSD_XBC), (MOBA_INNER, MOBA_INNER, MOBA_INNER), (LANES,)))
    y_ssd = _ssd(z, xbc, dt, conv_w, conv_b, dt_bias, a_log, d_skip, gate_norm, bsz, s)
    y_att = _moba(q, k, v, *rope_tabs, bsz, s)
    w_o = w_out.astype(BF16)
    return [y_ssd, y_att], [w_o[:SSD_INNER], w_o[SSD_INNER:]]


def _odd_layer(x, bsz, s, g, w_in, fgate_bias, w_out):
    qkv = 3 * FOX_INNER
    ws = [w_in[:, :qkv].astype(BF16), _pad_cols(w_in[:, qkv:]).astype(BF16)]
    q, k, v, f = _norm_matmul(x, g, ws, ((FOX_INNER, FOX_INNER, FOX_INNER), (LANES,)))
    c, ct = _fox_gate(f, fgate_bias, bsz, s)
    y = _fox(q, k, v, c, ct, bsz, s)
    return [y], [w_out.astype(BF16)]


def kernel(x, norm_mix_even, w_in_even, conv_w, conv_b, dt_bias, a_log, d_skip, ssd_gate_norm, w_out_even,
           norm_mix_odd, w_in_odd, fgate_bias, w_out_odd, norm_mlp, w_up, w_down, final_norm):
    bsz, s, d = x.shape
    depth = norm_mlp.shape[0]
    assert s % (ATT_TILE * ATT_TPS) == 0 and s % SSD_STEP == 0
    h = x.reshape(bsz * s, d)
    rope_tabs = _rope_tables(s)
    for layer in range(depth):
        i = layer // 2
        if layer % 2 == 0:
            acts, ws = _even_layer(h, bsz, s, norm_mix_even[i], w_in_even[i], conv_w[i], conv_b[i], dt_bias[i],
                            a_log[i], d_skip[i], ssd_gate_norm[i], w_out_even[i], rope_tabs)
        else:
            acts, ws = _odd_layer(h, bsz, s, norm_mix_odd[i], w_in_odd[i], fgate_bias[i], w_out_odd[i])
        h = _proj_mlp(acts, ws, h, norm_mlp[layer], w_up[layer].astype(BF16), w_down[layer].astype(BF16),
                      final_norm, final_norm=(layer == depth - 1))
    return h.reshape(bsz, s, d)
```

```python
import functools
import math

import jax
import jax.numpy as jnp
from jax import lax
from jax.experimental import pallas as pl
from jax.experimental.pallas import tpu as pltpu

NORM_EPS = 1e-5
ROPE_THETA = 10000.0
LANES = 128
HEAD_DIM = 64
SSD_HEADS = 16
SSD_GROUPS = 4
SSD_STATE = 128
SSD_CONV = 4
SSD_CHUNK = 128
SSD_INNER = SSD_HEADS * HEAD_DIM
SSD_GN = SSD_GROUPS * SSD_STATE
SSD_XBC = SSD_INNER + 2 * SSD_GN
MOBA_HEADS = 8
MOBA_INNER = MOBA_HEADS * HEAD_DIM
MOBA_BLOCK = 256
MOBA_TOPK = 3
FOX_HEADS = 16
FOX_INNER = FOX_HEADS * HEAD_DIM
ATT_TILE = MOBA_BLOCK
VMEM_LIMIT = 56 * 1024 * 1024

F32 = jnp.float32
BF16 = jnp.bfloat16
NEG_INF = float("-inf")
LOG2E = math.log2(math.e)


def _cparams(sem):
    return pltpu.CompilerParams(dimension_semantics=sem, vmem_limit_bytes=VMEM_LIMIT)


def _resident(shape):
    return pl.BlockSpec(shape, lambda i: (0,) * len(shape), pipeline_mode=pl.Buffered(1))


def _rms(x, g):
    return x * lax.rsqrt(jnp.mean(x * x, axis=-1, keepdims=True) + NORM_EPS) * g


def _silu(x):
    h = 0.5 * x
    return h + h * jnp.tanh(h)


def _softplus(x):
    return jnp.maximum(x, 0.0) + jnp.log1p(jnp.exp(-jnp.abs(x)))


def _split3(x):
    hi = x.astype(BF16)
    r = x - hi.astype(F32)
    mid = r.astype(BF16)
    lo = (r - mid.astype(F32)).astype(BF16)
    return hi, mid, lo


def _dot(a, b):
    return jnp.dot(a, b, preferred_element_type=F32)


def _dot_nt(a, b):
    return lax.dot_general(a, b, (((1,), (1,)), ((), ())), preferred_element_type=F32)


def _sel_dot_rhs(m01, x):
    hi, mid, lo = _split3(x)
    return _dot(m01, hi) + _dot(m01, mid) + _dot(m01, lo)


def _sel_dot_lhs(x, m01):
    hi, mid, lo = _split3(x)
    return _dot(hi, m01) + _dot(mid, m01) + _dot(lo, m01)


def _norm_matmul_kernel(*refs, splits, nw):
    x_ref, g_ref = refs[:2]
    w_refs, out_refs = refs[2:2 + nw], refs[2 + nw:]
    half = x_ref.shape[0] // 2
    for r in range(2):
        rows = slice(r * half, (r + 1) * half)
        xn = _rms(x_ref[rows, :], g_ref[...]).astype(BF16)
        for o_ref, (wi, a, b) in zip(out_refs, splits):
            o_ref[rows, :] = _dot(xn, w_refs[wi][:, a:b]).astype(o_ref.dtype)


def _norm_matmul(x, g, ws, widths, tm=512):
    t, d = x.shape
    splits = []
    for wi, (w, wds) in enumerate(zip(ws, widths)):
        a = 0
        for wd in wds:
            splits.append((wi, a, a + wd))
            a += wd
        assert a == w.shape[1]
    assert t % tm == 0
    out_w = [b - a for _, a, b in splits]
    return pl.pallas_call(
        functools.partial(_norm_matmul_kernel, splits=tuple(splits), nw=len(ws)),
        grid=(t // tm,),
        in_specs=[pl.BlockSpec((tm, d), lambda i: (i, 0)), _resident((1, d))] + [_resident(w.shape) for w in ws],
        out_specs=[pl.BlockSpec((tm, wd), lambda i: (i, 0)) for wd in out_w],
        out_shape=[jax.ShapeDtypeStruct((t, wd), F32) for wd in out_w],
        compiler_params=_cparams(("parallel",)),
        name="norm_matmul",
    )(x, g.reshape(1, d), *ws)


def _pad_cols(w):
    return jnp.pad(w, ((0, 0), (0, LANES - w.shape[1])))


MLP_CHUNK = 1024


def _proj_mlp_kernel(*refs, n, final_norm):
    a_refs, w_refs = refs[:n], refs[n:2 * n]
    x_ref, g_ref, wu_ref, wd_ref, gf_ref, o_ref = refs[2 * n:]
    proj = _dot(a_refs[0][...], w_refs[0][...])
    for a_ref, w_ref in zip(a_refs[1:], w_refs[1:]):
        proj = proj + _dot(a_ref[...], w_ref[...])
    x = x_ref[...] + proj
    xn = _rms(x, g_ref[...]).astype(BF16)
    acc = x
    for c in range(wu_ref.shape[1] // MLP_CHUNK):
        cols = slice(c * MLP_CHUNK, (c + 1) * MLP_CHUNK)
        h = _dot(xn, wu_ref[:, cols])
        h = jnp.square(jnp.maximum(h, 0.0)).astype(BF16)
        acc = acc + _dot(h, wd_ref[cols, :])
    if final_norm:
        acc = _rms(acc, gf_ref[...])
    o_ref[...] = acc


def _proj_mlp(acts, ws, x, g, w_up, w_down, g_final, final_norm, tm=512):
    t, d = x.shape
    n = len(acts)
    row = lambda width: pl.BlockSpec((tm, width), lambda i: (i, 0))
    const = lambda shape: pl.BlockSpec(shape, lambda i: (0, 0))
    in_specs = ([row(a.shape[1]) for a in acts] + [const(w.shape) for w in ws]
                + [row(d), const((1, d)), _resident(w_up.shape), _resident(w_down.shape), const((1, d))])
    return pl.pallas_call(
        functools.partial(_proj_mlp_kernel, n=n, final_norm=final_norm),
        grid=(t // tm,),
        in_specs=in_specs,
        out_specs=row(d),
        out_shape=jax.ShapeDtypeStruct((t, d), F32),
        compiler_params=_cparams(("parallel",)),
        name="proj_mlp",
    )(*acts, *ws, x, g.reshape(1, d), w_up, w_down, g_final.reshape(1, d))


SSD_STEP = 8 * SSD_CHUNK
CONV_STRIDE = SSD_STEP // 8 + 1
CONV_ROWS = 8 * CONV_STRIDE
SSD_SLABS = SSD_XBC // LANES


def _ssd_kernel(z_ref, xbc_ref, dt_ref, cw_ref, cb_ref, dtb_ref, alog_ref, dskip_ref, gn_ref,
                e_ref, o_ref, ubuf, xa_scr, hst):
    L = SSD_CHUNK
    TL = SSD_STEP
    c = pl.program_id(1)

    @pl.when(c == 0)
    def _():
        hst[...] = jnp.zeros_like(hst)
        ubuf[...] = jnp.zeros_like(ubuf)

    for sl in range(SSD_SLABS):
        ubuf[sl, 8:8 + TL, :] = xbc_ref[:, sl * LANES:(sl + 1) * LANES]

    def conv_slab(sl, carry):
        w = cw_ref[sl]
        b = cb_ref[sl]
        u = [ubuf[sl, pl.ds(5 + m, 8, stride=CONV_STRIDE), :] for m in range(CONV_STRIDE + SSD_CONV - 1)]
        for r in range(CONV_STRIDE):
            acc = b
            for k in range(SSD_CONV):
                acc = acc + w[k:k + 1, :] * u[r + k]
            xa_scr[sl, pl.ds(r, 8, stride=CONV_STRIDE), :] = acc + acc * jnp.tanh(acc)
        ubuf[sl, 0:8, :] = ubuf[sl, TL:TL + 8, :]
        return carry

    lax.fori_loop(0, SSD_SLABS, conv_slab, 0)
    nxs = SSD_INNER // LANES
    ngs = SSD_GN // LANES
    row = lax.broadcasted_iota(jnp.int32, (L, L), 0)
    col = lax.broadcasted_iota(jnp.int32, (L, L), 1)
    causal = row >= col
    tril = causal.astype(BF16)
    e01 = e_ref[...]
    lane = lax.broadcasted_iota(jnp.int32, (L, LANES), 1)
    neg_a = -jnp.exp(alog_ref[...])
    gw = SSD_INNER // SSD_GROUPS
    hpl = LANES // HEAD_DIM

    for ci in range(TL // L):
        rows = slice(ci * L, (ci + 1) * L)
        xs = jnp.concatenate([xa_scr[sl, rows, :] for sl in range(nxs)], axis=1)
        dt = _softplus(dt_ref[rows, :] + dtb_ref[...])
        acum = _sel_dot_rhs(tril, dt * neg_a)
        alast = acum[L - 1:L, :]
        dtx = _dot(dt.astype(BF16), e01)
        eacx = _dot(jnp.exp(acum).astype(BF16), e01)
        dtex = _dot(jnp.exp(alast - acum).astype(BF16), e01)
        cdx = _sel_dot_lhs(jnp.exp(acum[L - 8:L, :]), e01)[7:8, :]
        xc = xs * dtx
        xcd = (xc * dtex).astype(BF16)
        acum2 = acum * LOG2E
        act2 = acum2.T

        cbs, y_offs = [], []
        for g in range(SSD_GROUPS):
            b_f32 = xa_scr[nxs + g, rows, :]
            bg = b_f32.astype(BF16)
            bgt = b_f32.T.astype(BF16)
            cg = xa_scr[nxs + ngs + g, rows, :].astype(BF16)
            cbs.append(_dot_nt(cg, bg))
            hprev = hst[g]
            y_offs.append(_dot(cg, hprev.astype(BF16)) * eacx[:, g * gw:(g + 1) * gw])
            hst[g] = hprev * cdx[:, g * gw:(g + 1) * gw] + _dot(bgt, xcd[:, g * gw:(g + 1) * gw])
        ms = []
        for h in range(SSD_HEADS):
            dec = jnp.exp2(jnp.where(causal, acum2[:, h:h + 1] - act2[h:h + 1, :], NEG_INF))
            ms.append((cbs[h * HEAD_DIM // gw] * dec).astype(BF16))
        y_diags = []
        for pr in range(SSD_INNER // LANES):
            xcp = xc[:, pr * LANES:(pr + 1) * LANES]
            xm = [jnp.where((lane >= hh * HEAD_DIM) & (lane < (hh + 1) * HEAD_DIM), xcp, 0.0).astype(BF16)
                  for hh in range(hpl)]
            y_diags.append(_dot(jnp.concatenate(ms[pr * hpl:(pr + 1) * hpl], axis=1), jnp.concatenate(xm, axis=0)))

        ppg = gw // LANES
        for g in range(SSD_GROUPS):
            lanes = slice(g * gw, (g + 1) * gw)
            y = jnp.concatenate(y_diags[g * ppg:(g + 1) * ppg], axis=1) + y_offs[g] + xs[:, lanes] * dskip_ref[:, lanes]
            yg = y * _silu(z_ref[rows, lanes])
            ms_g = jnp.mean(yg * yg, axis=-1, keepdims=True)
            o_ref[rows, lanes] = (yg * lax.rsqrt(ms_g + NORM_EPS) * gn_ref[:, lanes]).astype(o_ref.dtype)


def _pad_lanes(v):
    return jnp.pad(v.astype(F32), (0, LANES - v.shape[0])).reshape(1, LANES)


def _ssd(z, xbc, dt, conv_w, conv_b, dt_bias, a_log, d_skip, gate_norm, bsz, s):
    L = SSD_STEP
    nc = s // L
    e01 = (jnp.arange(LANES)[:, None] == (jnp.arange(SSD_INNER) // HEAD_DIM)[None, :]).astype(BF16)
    dsx = jnp.repeat(d_skip.astype(F32), HEAD_DIM).reshape(1, SSD_INNER)
    cw = (0.5 * conv_w.astype(F32)).reshape(SSD_CONV, SSD_SLABS, LANES).transpose(1, 0, 2)
    cw = jnp.pad(cw, ((0, 0), (0, 8 - SSD_CONV), (0, 0)))
    cb = (0.5 * conv_b.astype(F32)).reshape(SSD_SLABS, 1, LANES)
    tok = lambda b, c: (b * nc + c, 0)
    const = lambda b, c: (0, 0)
    const3 = lambda b, c: (0, 0, 0)
    return pl.pallas_call(
        _ssd_kernel,
        grid=(bsz, nc),
        in_specs=[pl.BlockSpec((L, SSD_INNER), tok),
                  pl.BlockSpec((L, SSD_XBC), tok),
                  pl.BlockSpec((L, LANES), tok),
                  pl.BlockSpec((SSD_SLABS, 8, LANES), const3),
                  pl.BlockSpec((SSD_SLABS, 1, LANES), const3),
                  pl.BlockSpec((1, LANES), const),
                  pl.BlockSpec((1, LANES), const),
                  pl.BlockSpec((1, SSD_INNER), const),
                  pl.BlockSpec((1, SSD_INNER), const),
                  pl.BlockSpec((LANES, SSD_INNER), const)],
        out_specs=pl.BlockSpec((L, SSD_INNER), tok),
        out_shape=jax.ShapeDtypeStruct((bsz * s, SSD_INNER), BF16),
        scratch_shapes=[pltpu.VMEM((SSD_SLABS, 8 + CONV_ROWS + 8, LANES), F32),
                        pltpu.VMEM((SSD_SLABS, CONV_ROWS, LANES), F32),
                        pltpu.VMEM((SSD_GROUPS, SSD_STATE, SSD_INNER // SSD_GROUPS), F32)],
        compiler_params=_cparams(("parallel", "arbitrary")),
        name="ssd",
    )(z, xbc, dt, cw, cb, _pad_lanes(dt_bias), _pad_lanes(a_log), dsx,
      gate_norm.astype(F32).reshape(1, SSD_INNER), e01)


def _rope(x, cos, sin_lo, sin_hi):
    half = HEAD_DIM // 2
    return x * cos + pltpu.roll(x, LANES - half, 1) * sin_lo + pltpu.roll(x, half, 1) * sin_hi


def _fold8(x, op):
    parts = [x[r * 8:(r + 1) * 8, :] for r in range(x.shape[0] // 8)]
    return _tree(parts, op)


def _tree(parts, op):
    while len(parts) > 1:
        parts = [op(parts[a], parts[a + 1]) if a + 1 < len(parts) else parts[a]
                 for a in range(0, len(parts), 2)]
    return parts[0]


def _attend(i, nh, score_fn, row_shift, vt_scr, st_scr, m_scr, acc_scr, ot_ref):
    tq = ATT_TILE
    krow = lax.broadcasted_iota(jnp.int32, (tq, tq), 0)
    qcol = lax.broadcasted_iota(jnp.int32, (tq, tq), 1)
    m_scr[...] = jnp.full(m_scr.shape, NEG_INF, F32)
    acc_scr[...] = jnp.zeros(acc_scr.shape, F32)

    def trip(js, diagonal_last):
        shifts, alphas = [], []
        for h in range(nh):
            folds = []
            for u, j in enumerate(js):
                s_t = score_fn(h, j)
                if diagonal_last and u == len(js) - 1:
                    s_t = jnp.where(krow <= qcol, s_t, NEG_INF)
                st_scr[h, u] = s_t
                folds.append(_fold8(s_t, jnp.maximum))
            m_trip = jnp.max(_tree(folds, jnp.maximum), axis=0, keepdims=True)
            rs = row_shift[h]
            m_old = m_scr[h]
            m_new = jnp.maximum(m_old, m_trip if rs is None else m_trip + rs)
            m_scr[h] = m_new
            shifts.append(m_new if rs is None else m_new - rs)
            alphas.append(jnp.exp2(m_old - m_new))
        for h in range(nh):
            pv = [_dot(vt_scr[j, h], jnp.exp2(st_scr[h, u] - shifts[h]).astype(BF16)) for u, j in enumerate(js)]
            acc_scr[h] = alphas[h] * acc_scr[h] + _tree(pv, jnp.add)

    group = st_scr.shape[1]
    rem = i % group
    for r in range(group):
        @pl.when(rem == r)
        def _(r=r):
            trip([i - r + u for u in range(r + 1)], True)

    def full_trip(g, carry):
        trip([group * g + u for u in range(group)], False)
        return carry

    lax.fori_loop(0, i // group, full_trip, 0)
    for h in range(nh):
        ot_ref[h * HEAD_DIM:(h + 1) * HEAD_DIM, :] = acc_scr[h, :HEAD_DIM, :] / acc_scr[h, HEAD_DIM:HEAD_DIM + 1, :]


def _store_tiles(o_ref, ot_scr):
    tq = ATT_TILE
    for t in range(ot_scr.shape[0]):
        for g in range(ot_scr.shape[1] // LANES):
            o_ref[t * tq:(t + 1) * tq, g * LANES:(g + 1) * LANES] = (
                ot_scr[t, g * LANES:(g + 1) * LANES, :].T.astype(o_ref.dtype))


V_ROWS = HEAD_DIM + 16
ATT_GROUPS = 2
ATT_LANES = ATT_GROUPS * LANES
ATT_NH = ATT_LANES // HEAD_DIM
ATT_TPS = 4
MOBA_TRIP = 8
FOX_TRIP = 8


def _store_vt(vt_scr, jb, v_blk):
    n = v_blk.shape[1]
    eye = (lax.broadcasted_iota(jnp.int32, (n, n), 0) == lax.broadcasted_iota(jnp.int32, (n, n), 1)).astype(BF16)
    vt = _dot_nt(eye, v_blk.astype(BF16))
    for h in range(v_blk.shape[1] // HEAD_DIM):
        vt_scr[jb, h, :HEAD_DIM, :] = vt[h * HEAD_DIM:(h + 1) * HEAD_DIM, :].astype(BF16)
        vt_scr[jb, h, HEAD_DIM:, :] = jnp.ones((V_ROWS - HEAD_DIM, v_blk.shape[0]), BF16)


def _attend_scratch(nb, group):
    tq = ATT_TILE
    return [pltpu.VMEM((nb, ATT_NH, V_ROWS, tq), BF16),
            pltpu.VMEM((ATT_NH, group, tq, tq), F32),
            pltpu.VMEM((ATT_NH, 1, tq), F32),
            pltpu.VMEM((ATT_NH, V_ROWS, tq), F32),
            pltpu.VMEM((ATT_TPS, ATT_LANES, tq), F32)]


def _masked_heads(x, scale):
    lane = lax.broadcasted_iota(jnp.int32, (x.shape[0], LANES), 1)
    out = []
    for h in range(x.shape[1] // HEAD_DIM):
        g, hh = divmod(h, LANES // HEAD_DIM)
        xg = x[:, g * LANES:(g + 1) * LANES]
        in_head = (lane >= hh * HEAD_DIM) & (lane < (hh + 1) * HEAD_DIM)
        out.append(jnp.where(in_head, xg * scale, 0.0))
    return out


def _moba_kernel(q_ref, k_ref, v_ref, cos_ref, sinl_ref, sinh_ref, o_ref, kr_scr, km_scr, bias_scr, qs_scr,
                 vt_scr, st_scr, m_scr, acc_scr, ot_scr):
    tq = ATT_TILE
    nb = kr_scr.shape[1]
    step = pl.program_id(2)
    scale = HEAD_DIM ** -0.5
    hpg = LANES // HEAD_DIM

    @pl.when(step == 0)
    def _():
        for jb in range(nb):
            rows = slice(jb * tq, (jb + 1) * tq)
            for g in range(ATT_GROUPS):
                lanes = slice(g * LANES, (g + 1) * LANES)
                kr = _rope(k_ref[rows, lanes], cos_ref[rows, :], sinl_ref[rows, :], sinh_ref[rows, :])
                kr_scr[g, jb] = kr.astype(BF16)
                km_scr[g, jb:jb + 1, :] = jnp.mean(kr, axis=0, keepdims=True)
            _store_vt(vt_scr, jb, v_ref[rows, :])

    klane = lax.broadcasted_iota(jnp.int32, (nb, LANES), 1)
    jrow = lax.broadcasted_iota(jnp.int32, (nb, ATT_NH * tq), 0)
    kms = []
    for g in range(ATT_GROUPS):
        for hh in range(hpg):
            km = jnp.where((klane >= hh * HEAD_DIM) & (klane < (hh + 1) * HEAD_DIM), km_scr[g], 0.0)
            k_hi = km.astype(BF16)
            kms.append((k_hi, (km - k_hi.astype(F32)).astype(BF16)))
    for t in range(ATT_TPS):
        i = step * ATT_TPS + t
        trows = slice(t * tq, (t + 1) * tq)
        qrows = pl.ds(pl.multiple_of(i * tq, tq), tq)
        cos, sin_lo, sin_hi = cos_ref[qrows, :], sinl_ref[qrows, :], sinh_ref[qrows, :]
        qgs = [_rope(q_ref[trows, g * LANES:(g + 1) * LANES], cos, sin_lo, sin_hi) for g in range(ATT_GROUPS)]
        gates = []
        for g in range(ATT_GROUPS):
            q_hi = qgs[g].astype(BF16)
            q_lo = (qgs[g] - q_hi.astype(F32)).astype(BF16)
            for hh in range(hpg):
                k_hi, k_lo = kms[g * hpg + hh]
                gates.append(_dot_nt(k_hi, q_hi) + _dot_nt(k_hi, q_lo) + _dot_nt(k_lo, q_hi))
        valid = jrow < i
        gate = jnp.where(valid, jnp.concatenate(gates, axis=1), NEG_INF)
        cnt = jnp.zeros(gate.shape, F32)
        for jp in range(nb):
            gj = gate[jp:jp + 1, :]
            cnt = cnt + jnp.where((gj > gate) | ((gj == gate) & (jp < jrow)), 1.0, 0.0)
        chosen = valid & (cnt < MOBA_TOPK)
        bias = jnp.where(chosen | (jrow == i), 0.0, NEG_INF)
        for h, qm in enumerate(_masked_heads(jnp.concatenate(qgs, axis=1), 1.0)):
            bias_scr[t, h] = bias[:, h * tq:(h + 1) * tq]
            qs_scr[t, h] = (qm * (scale * LOG2E)).astype(BF16)

    def tile(t, carry):
        def score_fn(h, j):
            return _dot_nt(kr_scr[h // hpg, j], qs_scr[t, h]) + bias_scr[t, h, pl.ds(j, 1), :]

        _attend(step * ATT_TPS + t, ATT_NH, score_fn, [None] * ATT_NH, vt_scr, st_scr, m_scr, acc_scr, ot_scr.at[t])
        return carry

    lax.fori_loop(0, ATT_TPS, tile, 0)
    _store_tiles(o_ref, ot_scr)


def _moba(q, k, v, cos, sin_lo, sin_hi, bsz, s):
    tq = ATT_TILE
    nb = s // tq
    q3, k3, v3 = (a.reshape(bsz, s, MOBA_INNER) for a in (q, k, v))
    whole = pl.BlockSpec((None, s, ATT_LANES), lambda b, p, i: (b, 0, p))
    tile = pl.BlockSpec((None, ATT_TPS * tq, ATT_LANES), lambda b, p, i: (b, i, p))
    tab = pl.BlockSpec((s, LANES), lambda b, p, i: (0, 0))
    out = pl.pallas_call(
        _moba_kernel,
        grid=(bsz, MOBA_INNER // ATT_LANES, nb // ATT_TPS),
        in_specs=[tile, whole, whole, tab, tab, tab],
        out_specs=tile,
        out_shape=jax.ShapeDtypeStruct((bsz, s, MOBA_INNER), BF16),
        scratch_shapes=[pltpu.VMEM((ATT_GROUPS, nb, tq, LANES), BF16),
                        pltpu.VMEM((ATT_GROUPS, nb, LANES), F32),
                        pltpu.VMEM((ATT_TPS, ATT_NH, nb, tq), F32),
                        pltpu.VMEM((ATT_TPS, ATT_NH, tq, LANES), BF16)] + _attend_scratch(nb, MOBA_TRIP),
        compiler_params=_cparams(("parallel", "parallel", "arbitrary")),
        name="moba",
    )(q3, k3, v3, cos, sin_lo, sin_hi)
    return out.reshape(bsz * s, MOBA_INNER)


def _fox_gate_kernel(f_ref, b_ref, c_ref, ct_ref):
    tb = ATT_TILE
    s = f_ref.shape[0]
    row = lax.broadcasted_iota(jnp.int32, (tb, tb), 0)
    col = lax.broadcasted_iota(jnp.int32, (tb, tb), 1)
    tril = (row >= col).astype(BF16)
    carry = jnp.zeros((1, LANES), F32)
    for jb in range(s // tb):
        rows = slice(jb * tb, (jb + 1) * tb)
        log_f = -_softplus(-(f_ref[rows, :] + b_ref[...]))
        cblk = _sel_dot_rhs(tril, log_f) + carry
        c_ref[rows, :] = cblk
        ct_ref[jb] = cblk.T
        carry = cblk[tb - 1:tb, :]


def _fox_gate(f, bias, bsz, s):
    return pl.pallas_call(
        _fox_gate_kernel,
        grid=(bsz,),
        in_specs=[pl.BlockSpec((None, s, LANES), lambda b: (b, 0, 0)),
                  pl.BlockSpec((1, LANES), lambda b: (0, 0))],
        out_specs=[pl.BlockSpec((None, s, LANES), lambda b: (b, 0, 0)),
                   pl.BlockSpec((None, s // ATT_TILE, LANES, ATT_TILE), lambda b: (b, 0, 0, 0))],
        out_shape=[jax.ShapeDtypeStruct((bsz, s, LANES), F32),
                   jax.ShapeDtypeStruct((bsz, s // ATT_TILE, LANES, ATT_TILE), F32)],
        compiler_params=_cparams(("parallel",)),
        name="fox_gate",
    )(f.reshape(bsz, s, LANES), _pad_lanes(bias))


def _fox_kernel(q_ref, k_ref, v_ref, c_ref, ct_ref, o_ref, k_scr, ck_scr,
                vt_scr, st_scr, m_scr, acc_scr, ot_scr):
    tq = ATT_TILE
    nb = k_scr.shape[1]
    p = pl.program_id(1)
    step = pl.program_id(2)
    scale = HEAD_DIM ** -0.5
    hpg = LANES // HEAD_DIM

    @pl.when(step == 0)
    def _():
        lane = lax.broadcasted_iota(jnp.int32, (tq, LANES), 1)
        for jb in range(nb):
            rows = slice(jb * tq, (jb + 1) * tq)
            for g in range(ATT_GROUPS):
                k_scr[g, jb] = k_ref[rows, g * LANES:(g + 1) * LANES].astype(BF16)
            _store_vt(vt_scr, jb, v_ref[rows, :])
            cblk = c_ref[rows, :] * LOG2E
            for h in range(ATT_NH):
                colv = jnp.sum(jnp.where(lane == p * ATT_NH + h, cblk, 0.0), axis=1, keepdims=True)
                ck_scr[h, jb] = jnp.broadcast_to(colv, (tq, LANES))

    def tile(t, carry):
        i = step * ATT_TPS + t
        trows = pl.ds(pl.multiple_of(t * tq, tq), tq)
        qs = [qm.astype(BF16) for qm in _masked_heads(q_ref[trows, :], scale * LOG2E)]
        cqs = [ct_ref[t, pl.ds(p * ATT_NH + h, 1), :] * LOG2E for h in range(ATT_NH)]

        def score_fn(h, j):
            ck = ck_scr[h, j]
            return _dot_nt(k_scr[h // hpg, j], qs[h]) - jnp.concatenate([ck] * (tq // LANES), axis=1)

        _attend(i, ATT_NH, score_fn, cqs, vt_scr, st_scr, m_scr, acc_scr, ot_scr.at[t])
        return carry

    lax.fori_loop(0, ATT_TPS, tile, 0)
    _store_tiles(o_ref, ot_scr)


def _fox(q, k, v, c, ct, bsz, s):
    tq = ATT_TILE
    nb = s // tq
    q3, k3, v3 = (a.reshape(bsz, s, FOX_INNER) for a in (q, k, v))
    whole = pl.BlockSpec((None, s, ATT_LANES), lambda b, p, i: (b, 0, p))
    tile = pl.BlockSpec((None, ATT_TPS * tq, ATT_LANES), lambda b, p, i: (b, i, p))
    out = pl.pallas_call(
        _fox_kernel,
        grid=(bsz, FOX_INNER // ATT_LANES, nb // ATT_TPS),
        in_specs=[tile, whole, whole,
                  pl.BlockSpec((None, s, LANES), lambda b, p, i: (b, 0, 0)),
                  pl.BlockSpec((None, ATT_TPS, LANES, tq), lambda b, p, i: (b, i, 0, 0))],
        out_specs=tile,
        out_shape=jax.ShapeDtypeStruct((bsz, s, FOX_INNER), BF16),
        scratch_shapes=[pltpu.VMEM((ATT_GROUPS, nb, tq, LANES), BF16),
                        pltpu.VMEM((ATT_NH, nb, tq, LANES), F32)] + _attend_scratch(nb, FOX_TRIP),
        compiler_params=_cparams(("parallel", "parallel", "arbitrary")),
        name="fox",
    )(q3, k3, v3, c, ct)
    return out.reshape(bsz * s, FOX_INNER)


def _rope_tables(s):
    half = HEAD_DIM // 2
    inv = jnp.power(ROPE_THETA, -jnp.arange(half, dtype=F32) / half)
    ang = jnp.arange(s, dtype=F32)[:, None] * inv[None, :]
    cos, sin = jnp.cos(ang), jnp.sin(ang)
    reps = LANES // HEAD_DIM
    zero = jnp.zeros_like(sin)
    return (jnp.tile(jnp.concatenate([cos, cos], axis=1), (1, reps)),
            jnp.tile(jnp.concatenate([-sin, zero], axis=1), (1, reps)),
            jnp.tile(jnp.concatenate([zero, sin], axis=1), (1, reps)))


def _even_layer(x, bsz, s, g, w_in, conv_w, conv_b, dt_bias, a_log, d_skip, gate_norm, w_out, rope_tabs):
    o1 = SSD_INNER
    o2 = o1 + SSD_XBC
    o3 = o2 + SSD_HEADS
    ws = [w_in[:, :o2].astype(BF16), w_in[:, o3:].astype(BF16), _pad_cols(w_in[:, o2:o3]).astype(BF16)]
    z, xbc, q, k, v, dt = _norm_matmul(
        x, g, ws, ((SSD_INNER, SSD_XBC), (MOBA_INNER, MOBA_INNER, MOBA_INNER), (LANES,)))
    y_ssd = _ssd(z, xbc, dt, conv_w, conv_b, dt_bias, a_log, d_skip, gate_norm, bsz, s)
    y_att = _moba(q, k, v, *rope_tabs, bsz, s)
    w_o = w_out.astype(BF16)
    return [y_ssd, y_att], [w_o[:SSD_INNER], w_o[SSD_INNER:]]


def _odd_layer(x, bsz, s, g, w_in, fgate_bias, w_out):
    qkv = 3 * FOX_INNER
    ws = [w_in[:, :qkv].astype(BF16), _pad_cols(w_in[:, qkv:]).astype(BF16)]
    q, k, v, f = _norm_matmul(x, g, ws, ((FOX_INNER, FOX_INNER, FOX_INNER), (LANES,)))
    c, ct = _fox_gate(f, fgate_bias, bsz, s)
    y = _fox(q, k, v, c, ct, bsz, s)
    return [y], [w_out.astype(BF16)]


def kernel(x, norm_mix_even, w_in_even, conv_w, conv_b, dt_bias, a_log, d_skip, ssd_gate_norm, w_out_even,
           norm_mix_odd, w_in_odd, fgate_bias, w_out_odd, norm_mlp, w_up, w_down, final_norm):
    bsz, s, d = x.shape
    depth = norm_mlp.shape[0]
    assert s % (ATT_TILE * ATT_TPS) == 0 and s % SSD_STEP == 0
    h = x.reshape(bsz * s, d)
    rope_tabs = _rope_tables(s)
    for layer in range(depth):
        i = layer // 2
        if layer % 2 == 0:
            acts, ws = _even_layer(h, bsz, s, norm_mix_even[i], w_in_even[i], conv_w[i], conv_b[i], dt_bias[i],
                            a_log[i], d_skip[i], ssd_gate_norm[i], w_out_even[i], rope_tabs)
        else:
            acts, ws = _odd_layer(h, bsz, s, norm_mix_odd[i], w_in_odd[i], fgate_bias[i], w_out_odd[i])
        h = _proj_mlp(acts, ws, h, norm_mlp[layer], w_up[layer].astype(BF16), w_down[layer].astype(BF16),
                      final_norm, final_norm=(layer == depth - 1))
    return h.reshape(bsz, s, d)
```

```python
import functools
import math

import jax
import jax.numpy as jnp
from jax import lax
from jax.experimental import pallas as pl
from jax.experimental.pallas import tpu as pltpu

NORM_EPS = 1e-5
ROPE_THETA = 10000.0
LANES = 128
HEAD_DIM = 64
SSD_HEADS = 16
SSD_GROUPS = 4
SSD_STATE = 128
SSD_CONV = 4
SSD_CHUNK = 128
SSD_INNER = SSD_HEADS * HEAD_DIM
SSD_GN = SSD_GROUPS * SSD_STATE
SSD_XBC = SSD_INNER + 2 * SSD_GN
MOBA_HEADS = 8
MOBA_INNER = MOBA_HEADS * HEAD_DIM
MOBA_BLOCK = 256
MOBA_TOPK = 3
FOX_HEADS = 16
FOX_INNER = FOX_HEADS * HEAD_DIM
ATT_TILE = MOBA_BLOCK
VMEM_LIMIT = 56 * 1024 * 1024

F32 = jnp.float32
BF16 = jnp.bfloat16
NEG_INF = float("-inf")
LOG2E = math.log2(math.e)


def _cparams(sem):
    return pltpu.CompilerParams(dimension_semantics=sem, vmem_limit_bytes=VMEM_LIMIT)


def _resident(shape):
    return pl.BlockSpec(shape, lambda i: (0,) * len(shape), pipeline_mode=pl.Buffered(1))


def _rms(x, g):
    return x * lax.rsqrt(jnp.mean(x * x, axis=-1, keepdims=True) + NORM_EPS) * g


def _silu(x):
    h = 0.5 * x
    return h + h * jnp.tanh(h)


def _softplus(x):
    return jnp.maximum(x, 0.0) + jnp.log1p(jnp.exp(-jnp.abs(x)))


def _split3(x):
    hi = x.astype(BF16)
    r = x - hi.astype(F32)
    mid = r.astype(BF16)
    lo = (r - mid.astype(F32)).astype(BF16)
    return hi, mid, lo


def _dot(a, b):
    return jnp.dot(a, b, preferred_element_type=F32)


def _dot_nt(a, b):
    return lax.dot_general(a, b, (((1,), (1,)), ((), ())), preferred_element_type=F32)


def _sel_dot_rhs(m01, x):
    hi, mid, lo = _split3(x)
    return _dot(m01, hi) + _dot(m01, mid) + _dot(m01, lo)


def _sel_dot_lhs(x, m01):
    hi, mid, lo = _split3(x)
    return _dot(hi, m01) + _dot(mid, m01) + _dot(lo, m01)


def _norm_matmul_kernel(*refs, splits, nw):
    x_ref, g_ref = refs[:2]
    w_refs, out_refs = refs[2:2 + nw], refs[2 + nw:]
    half = x_ref.shape[0] // 2
    for r in range(2):
        rows = slice(r * half, (r + 1) * half)
        xn = _rms(x_ref[rows, :], g_ref[...]).astype(BF16)
        for o_ref, (wi, a, b) in zip(out_refs, splits):
            o_ref[rows, :] = _dot(xn, w_refs[wi][:, a:b]).astype(o_ref.dtype)


def _norm_matmul(x, g, ws, widths, tm=512):
    t, d = x.shape
    splits = []
    for wi, (w, wds) in enumerate(zip(ws, widths)):
        a = 0
        for wd in wds:
            splits.append((wi, a, a + wd))
            a += wd
        assert a == w.shape[1]
    assert t % tm == 0
    out_w = [b - a for _, a, b in splits]
    return pl.pallas_call(
        functools.partial(_norm_matmul_kernel, splits=tuple(splits), nw=len(ws)),
        grid=(t // tm,),
        in_specs=[pl.BlockSpec((tm, d), lambda i: (i, 0)), _resident((1, d))] + [_resident(w.shape) for w in ws],
        out_specs=[pl.BlockSpec((tm, wd), lambda i: (i, 0)) for wd in out_w],
        out_shape=[jax.ShapeDtypeStruct((t, wd), F32) for wd in out_w],
        compiler_params=_cparams(("parallel",)),
        name="norm_matmul",
    )(x, g.reshape(1, d), *ws)


def _pad_cols(w):
    return jnp.pad(w, ((0, 0), (0, LANES - w.shape[1])))


MLP_CHUNK = 1024


def _proj_mlp_kernel(*refs, n, final_norm):
    a_refs, w_refs = refs[:n], refs[n:2 * n]
    x_ref, g_ref, wu_ref, wd_ref, gf_ref, o_ref = refs[2 * n:]
    proj = _dot(a_refs[0][...], w_refs[0][...])
    for a_ref, w_ref in zip(a_refs[1:], w_refs[1:]):
        proj = proj + _dot(a_ref[...], w_ref[...])
    x = x_ref[...] + proj
    xn = _rms(x, g_ref[...]).astype(BF16)
    acc = x
    for c in range(wu_ref.shape[1] // MLP_CHUNK):
        cols = slice(c * MLP_CHUNK, (c + 1) * MLP_CHUNK)
        h = _dot(xn, wu_ref[:, cols])
        h = jnp.square(jnp.maximum(h, 0.0)).astype(BF16)
        acc = acc + _dot(h, wd_ref[cols, :])
    if final_norm:
        acc = _rms(acc, gf_ref[...])
    o_ref[...] = acc


def _proj_mlp(acts, ws, x, g, w_up, w_down, g_final, final_norm, tm=512):
    t, d = x.shape
    n = len(acts)
    row = lambda width: pl.BlockSpec((tm, width), lambda i: (i, 0))
    const = lambda shape: pl.BlockSpec(shape, lambda i: (0, 0))
    in_specs = ([row(a.shape[1]) for a in acts] + [const(w.shape) for w in ws]
                + [row(d), const((1, d)), _resident(w_up.shape), _resident(w_down.shape), const((1, d))])
    return pl.pallas_call(
        functools.partial(_proj_mlp_kernel, n=n, final_norm=final_norm),
        grid=(t // tm,),
        in_specs=in_specs,
        out_specs=row(d),
        out_shape=jax.ShapeDtypeStruct((t, d), F32),
        compiler_params=_cparams(("parallel",)),
        name="proj_mlp",
    )(*acts, *ws, x, g.reshape(1, d), w_up, w_down, g_final.reshape(1, d))


SSD_STEP = 8 * SSD_CHUNK
CONV_STRIDE = SSD_STEP // 8 + 1
CONV_ROWS = 8 * CONV_STRIDE
SSD_SLABS = SSD_XBC // LANES


def _ssd_kernel(z_ref, xbc_ref, dt_ref, cw_ref, cb_ref, dtb_ref, alog_ref, dskip_ref, gn_ref,
                e_ref, o_ref, ubuf, xa_scr, hst):
    L = SSD_CHUNK
    TL = SSD_STEP
    c = pl.program_id(1)

    @pl.when(c == 0)
    def _():
        hst[...] = jnp.zeros_like(hst)
        ubuf[...] = jnp.zeros_like(ubuf)

    for sl in range(SSD_SLABS):
        ubuf[sl, 8:8 + TL, :] = xbc_ref[:, sl * LANES:(sl + 1) * LANES]

    def conv_slab(sl, carry):
        w = cw_ref[sl]
        b = cb_ref[sl]
        u = [ubuf[sl, pl.ds(5 + m, 8, stride=CONV_STRIDE), :] for m in range(CONV_STRIDE + SSD_CONV - 1)]
        for r in range(CONV_STRIDE):
            acc = b
            for k in range(SSD_CONV):
                acc = acc + w[k:k + 1, :] * u[r + k]
            xa_scr[sl, pl.ds(r, 8, stride=CONV_STRIDE), :] = acc + acc * jnp.tanh(acc)
        ubuf[sl, 0:8, :] = ubuf[sl, TL:TL + 8, :]
        return carry

    lax.fori_loop(0, SSD_SLABS, conv_slab, 0)
    nxs = SSD_INNER // LANES
    ngs = SSD_GN // LANES
    row = lax.broadcasted_iota(jnp.int32, (L, L), 0)
    col = lax.broadcasted_iota(jnp.int32, (L, L), 1)
    causal = row >= col
    tril = causal.astype(BF16)
    e01 = e_ref[...]
    lane = lax.broadcasted_iota(jnp.int32, (L, LANES), 1)
    neg_a = -jnp.exp(alog_ref[...])
    gw = SSD_INNER // SSD_GROUPS
    hpl = LANES // HEAD_DIM

    for ci in range(TL // L):
        rows = slice(ci * L, (ci + 1) * L)
        xs = jnp.concatenate([xa_scr[sl, rows, :] for sl in range(nxs)], axis=1)
        dt = _softplus(dt_ref[rows, :] + dtb_ref[...])
        acum = _sel_dot_rhs(tril, dt * neg_a)
        alast = acum[L - 1:L, :]
        dtx = _dot(dt.astype(BF16), e01)
        eacx = _dot(jnp.exp(acum).astype(BF16), e01)
        dtex = _dot(jnp.exp(alast - acum).astype(BF16), e01)
        cdx = _sel_dot_lhs(jnp.exp(acum[L - 8:L, :]), e01)[7:8, :]
        xc = xs * dtx
        xcd = (xc * dtex).astype(BF16)
        acum2 = acum * LOG2E
        act2 = acum2.T

        cbs, y_offs = [], []
        for g in range(SSD_GROUPS):
            b_f32 = xa_scr[nxs + g, rows, :]
            bg = b_f32.astype(BF16)
            bgt = b_f32.T.astype(BF16)
            cg = xa_scr[nxs + ngs + g, rows, :].astype(BF16)
            cbs.append(_dot_nt(cg, bg))
            hprev = hst[g]
            y_offs.append(_dot(cg, hprev.astype(BF16)) * eacx[:, g * gw:(g + 1) * gw])
            hst[g] = hprev * cdx[:, g * gw:(g + 1) * gw] + _dot(bgt, xcd[:, g * gw:(g + 1) * gw])
        ms = []
        for h in range(SSD_HEADS):
            dec = jnp.exp2(jnp.where(causal, acum2[:, h:h + 1] - act2[h:h + 1, :], NEG_INF))
            ms.append((cbs[h * HEAD_DIM // gw] * dec).astype(BF16))
        y_diags = []
        for pr in range(SSD_INNER // LANES):
            xcp = xc[:, pr * LANES:(pr + 1) * LANES]
            xm = [jnp.where((lane >= hh * HEAD_DIM) & (lane < (hh + 1) * HEAD_DIM), xcp, 0.0).astype(BF16)
                  for hh in range(hpl)]
            y_diags.append(_dot(jnp.concatenate(ms[pr * hpl:(pr + 1) * hpl], axis=1), jnp.concatenate(xm, axis=0)))

        ppg = gw // LANES
        for g in range(SSD_GROUPS):
            lanes = slice(g * gw, (g + 1) * gw)
            y = jnp.concatenate(y_diags[g * ppg:(g + 1) * ppg], axis=1) + y_offs[g] + xs[:, lanes] * dskip_ref[:, lanes]
            yg = y * _silu(z_ref[rows, lanes])
            ms_g = jnp.mean(yg * yg, axis=-1, keepdims=True)
            o_ref[rows, lanes] = (yg * lax.rsqrt(ms_g + NORM_EPS) * gn_ref[:, lanes]).astype(o_ref.dtype)


def _pad_lanes(v):
    return jnp.pad(v.astype(F32), (0, LANES - v.shape[0])).reshape(1, LANES)


def _ssd(z, xbc, dt, conv_w, conv_b, dt_bias, a_log, d_skip, gate_norm, bsz, s):
    L = SSD_STEP
    nc = s // L
    e01 = (jnp.arange(LANES)[:, None] == (jnp.arange(SSD_INNER) // HEAD_DIM)[None, :]).astype(BF16)
    dsx = jnp.repeat(d_skip.astype(F32), HEAD_DIM).reshape(1, SSD_INNER)
    cw = (0.5 * conv_w.astype(F32)).reshape(SSD_CONV, SSD_SLABS, LANES).transpose(1, 0, 2)
    cw = jnp.pad(cw, ((0, 0), (0, 8 - SSD_CONV), (0, 0)))
    cb = (0.5 * conv_b.astype(F32)).reshape(SSD_SLABS, 1, LANES)
    tok = lambda b, c: (b * nc + c, 0)
    const = lambda b, c: (0, 0)
    const3 = lambda b, c: (0, 0, 0)
    return pl.pallas_call(
        _ssd_kernel,
        grid=(bsz, nc),
        in_specs=[pl.BlockSpec((L, SSD_INNER), tok),
                  pl.BlockSpec((L, SSD_XBC), tok),
                  pl.BlockSpec((L, LANES), tok),
                  pl.BlockSpec((SSD_SLABS, 8, LANES), const3),
                  pl.BlockSpec((SSD_SLABS, 1, LANES), const3),
                  pl.BlockSpec((1, LANES), const),
                  pl.BlockSpec((1, LANES), const),
                  pl.BlockSpec((1, SSD_INNER), const),
                  pl.BlockSpec((1, SSD_INNER), const),
                  pl.BlockSpec((LANES, SSD_INNER), const)],
        out_specs=pl.BlockSpec((L, SSD_INNER), tok),
        out_shape=jax.ShapeDtypeStruct((bsz * s, SSD_INNER), BF16),
        scratch_shapes=[pltpu.VMEM((SSD_SLABS, 8 + CONV_ROWS + 8, LANES), F32),
                        pltpu.VMEM((SSD_SLABS, CONV_ROWS, LANES), F32),
                        pltpu.VMEM((SSD_GROUPS, SSD_STATE, SSD_INNER // SSD_GROUPS), F32)],
        compiler_params=_cparams(("parallel", "arbitrary")),
        name="ssd",
    )(z, xbc, dt, cw, cb, _pad_lanes(dt_bias), _pad_lanes(a_log), dsx,
      gate_norm.astype(F32).reshape(1, SSD_INNER), e01)


def _rope(x, cos, sin_lo, sin_hi):
    half = HEAD_DIM // 2
    return x * cos + pltpu.roll(x, LANES - half, 1) * sin_lo + pltpu.roll(x, half, 1) * sin_hi


def _fold8(x, op):
    parts = [x[r * 8:(r + 1) * 8, :] for r in range(x.shape[0] // 8)]
    return _tree(parts, op)


def _tree(parts, op):
    while len(parts) > 1:
        parts = [op(parts[a], parts[a + 1]) if a + 1 < len(parts) else parts[a]
                 for a in range(0, len(parts), 2)]
    return parts[0]


def _attend(i, nh, score_fn, row_shift, vt_scr, st_scr, m_scr, acc_scr, ot_ref, block_bias=None):
    tq = ATT_TILE
    krow = lax.broadcasted_iota(jnp.int32, (tq, tq), 0)
    qcol = lax.broadcasted_iota(jnp.int32, (tq, tq), 1)
    m_scr[...] = jnp.full(m_scr.shape, NEG_INF, F32)
    acc_scr[...] = jnp.zeros(acc_scr.shape, F32)

    def trip(js, diagonal_last):
        shifts, alphas = [], []
        for h in range(nh):
            folds = []
            for u, j in enumerate(js):
                s_t = score_fn(h, j)
                if diagonal_last and u == len(js) - 1:
                    s_t = jnp.where(krow <= qcol, s_t, NEG_INF)
                st_scr[h, u] = s_t
                fold = _fold8(s_t, jnp.maximum)
                folds.append(fold if block_bias is None else fold + block_bias(h, j))
            m_trip = jnp.max(_tree(folds, jnp.maximum), axis=0, keepdims=True)
            rs = row_shift[h]
            m_old = m_scr[h]
            m_new = jnp.maximum(m_old, m_trip if rs is None else m_trip + rs)
            m_scr[h] = m_new
            shifts.append(m_new if rs is None else m_new - rs)
            alphas.append(jnp.exp2(m_old - m_new))
        for h in range(nh):
            pv = []
            for u, j in enumerate(js):
                shift = shifts[h] if block_bias is None else shifts[h] - block_bias(h, j)
                pv.append(_dot(vt_scr[j, h], jnp.exp2(st_scr[h, u] - shift).astype(BF16)))
            acc_scr[h] = alphas[h] * acc_scr[h] + _tree(pv, jnp.add)

    group = st_scr.shape[1]
    rem = i % group
    for r in range(group):
        @pl.when(rem == r)
        def _(r=r):
            trip([i - r + u for u in range(r + 1)], True)

    def full_trip(g, carry):
        trip([group * g + u for u in range(group)], False)
        return carry

    lax.fori_loop(0, i // group, full_trip, 0)
    for h in range(nh):
        ot_ref[h * HEAD_DIM:(h + 1) * HEAD_DIM, :] = acc_scr[h, :HEAD_DIM, :] / acc_scr[h, HEAD_DIM:HEAD_DIM + 1, :]


def _store_tiles(o_ref, ot_scr):
    tq = ATT_TILE
    for t in range(ot_scr.shape[0]):
        for g in range(ot_scr.shape[1] // LANES):
            o_ref[t * tq:(t + 1) * tq, g * LANES:(g + 1) * LANES] = (
                ot_scr[t, g * LANES:(g + 1) * LANES, :].T.astype(o_ref.dtype))


V_ROWS = HEAD_DIM + 16
ATT_GROUPS = 2
ATT_LANES = ATT_GROUPS * LANES
ATT_NH = ATT_LANES // HEAD_DIM
ATT_TPS = 4
MOBA_TRIP = 8
FOX_TRIP = 8


def _store_vt(vt_scr, jb, v_blk):
    n = v_blk.shape[1]
    eye = (lax.broadcasted_iota(jnp.int32, (n, n), 0) == lax.broadcasted_iota(jnp.int32, (n, n), 1)).astype(BF16)
    vt = _dot_nt(eye, v_blk.astype(BF16))
    for h in range(v_blk.shape[1] // HEAD_DIM):
        vt_scr[jb, h, :HEAD_DIM, :] = vt[h * HEAD_DIM:(h + 1) * HEAD_DIM, :].astype(BF16)
        vt_scr[jb, h, HEAD_DIM:, :] = jnp.ones((V_ROWS - HEAD_DIM, v_blk.shape[0]), BF16)


def _attend_scratch(nb, group):
    tq = ATT_TILE
    return [pltpu.VMEM((nb, ATT_NH, V_ROWS, tq), BF16),
            pltpu.VMEM((ATT_NH, group, tq, tq), F32),
            pltpu.VMEM((ATT_NH, 1, tq), F32),
            pltpu.VMEM((ATT_NH, V_ROWS, tq), F32),
            pltpu.VMEM((ATT_TPS, ATT_LANES, tq), F32)]


def _masked_heads(x, scale):
    lane = lax.broadcasted_iota(jnp.int32, (x.shape[0], LANES), 1)
    out = []
    for h in range(x.shape[1] // HEAD_DIM):
        g, hh = divmod(h, LANES // HEAD_DIM)
        xg = x[:, g * LANES:(g + 1) * LANES]
        in_head = (lane >= hh * HEAD_DIM) & (lane < (hh + 1) * HEAD_DIM)
        out.append(jnp.where(in_head, xg * scale, 0.0))
    return out


def _moba_kernel(q_ref, k_ref, v_ref, cos_ref, sinl_ref, sinh_ref, o_ref, kr_scr, km_scr, bias_scr, qs_scr,
                 vt_scr, st_scr, m_scr, acc_scr, ot_scr):
    tq = ATT_TILE
    nb = kr_scr.shape[1]
    step = pl.program_id(2)
    scale = HEAD_DIM ** -0.5
    hpg = LANES // HEAD_DIM

    @pl.when(step == 0)
    def _():
        for jb in range(nb):
            rows = slice(jb * tq, (jb + 1) * tq)
            for g in range(ATT_GROUPS):
                lanes = slice(g * LANES, (g + 1) * LANES)
                kr = _rope(k_ref[rows, lanes], cos_ref[rows, :], sinl_ref[rows, :], sinh_ref[rows, :])
                kr_scr[g, jb] = kr.astype(BF16)
                km_scr[g, jb:jb + 1, :] = jnp.mean(kr, axis=0, keepdims=True)
            _store_vt(vt_scr, jb, v_ref[rows, :])

    klane = lax.broadcasted_iota(jnp.int32, (nb, LANES), 1)
    jrow = lax.broadcasted_iota(jnp.int32, (nb, ATT_NH * tq), 0)
    kms = []
    for g in range(ATT_GROUPS):
        for hh in range(hpg):
            km = jnp.where((klane >= hh * HEAD_DIM) & (klane < (hh + 1) * HEAD_DIM), km_scr[g], 0.0)
            k_hi = km.astype(BF16)
            kms.append((k_hi, (km - k_hi.astype(F32)).astype(BF16)))
    for t in range(ATT_TPS):
        i = step * ATT_TPS + t
        trows = slice(t * tq, (t + 1) * tq)
        qrows = pl.ds(pl.multiple_of(i * tq, tq), tq)
        cos, sin_lo, sin_hi = cos_ref[qrows, :], sinl_ref[qrows, :], sinh_ref[qrows, :]
        qgs = [_rope(q_ref[trows, g * LANES:(g + 1) * LANES], cos, sin_lo, sin_hi) for g in range(ATT_GROUPS)]
        gates = []
        for g in range(ATT_GROUPS):
            q_hi = qgs[g].astype(BF16)
            q_lo = (qgs[g] - q_hi.astype(F32)).astype(BF16)
            for hh in range(hpg):
                k_hi, k_lo = kms[g * hpg + hh]
                gates.append(_dot_nt(k_hi, q_hi) + _dot_nt(k_hi, q_lo) + _dot_nt(k_lo, q_hi))
        valid = jrow < i
        gate = jnp.where(valid, jnp.concatenate(gates, axis=1), NEG_INF)
        cnt = jnp.zeros(gate.shape, F32)
        for jp in range(nb):
            gj = gate[jp:jp + 1, :]
            cnt = cnt + jnp.where((gj > gate) | ((gj == gate) & (jp < jrow)), 1.0, 0.0)
        chosen = valid & (cnt < MOBA_TOPK)
        bias = jnp.where(chosen | (jrow == i), 0.0, NEG_INF)
        for h, qm in enumerate(_masked_heads(jnp.concatenate(qgs, axis=1), 1.0)):
            bias_scr[t, h] = bias[:, h * tq:(h + 1) * tq]
            qs_scr[t, h] = (qm * (scale * LOG2E)).astype(BF16)

    def tile(t, carry):
        def score_fn(h, j):
            return _dot_nt(kr_scr[h // hpg, j], qs_scr[t, h])

        def block_bias(h, j):
            return bias_scr[t, h, pl.ds(j, 1), :]

        _attend(step * ATT_TPS + t, ATT_NH, score_fn, [None] * ATT_NH, vt_scr, st_scr, m_scr, acc_scr, ot_scr.at[t],
                block_bias)
        return carry

    lax.fori_loop(0, ATT_TPS, tile, 0)
    _store_tiles(o_ref, ot_scr)


def _moba(q, k, v, cos, sin_lo, sin_hi, bsz, s):
    tq = ATT_TILE
    nb = s // tq
    q3, k3, v3 = (a.reshape(bsz, s, MOBA_INNER) for a in (q, k, v))
    whole = pl.BlockSpec((None, s, ATT_LANES), lambda b, p, i: (b, 0, p))
    tile = pl.BlockSpec((None, ATT_TPS * tq, ATT_LANES), lambda b, p, i: (b, i, p))
    tab = pl.BlockSpec((s, LANES), lambda b, p, i: (0, 0))
    out = pl.pallas_call(
        _moba_kernel,
        grid=(bsz, MOBA_INNER // ATT_LANES, nb // ATT_TPS),
        in_specs=[tile, whole, whole, tab, tab, tab],
        out_specs=tile,
        out_shape=jax.ShapeDtypeStruct((bsz, s, MOBA_INNER), BF16),
        scratch_shapes=[pltpu.VMEM((ATT_GROUPS, nb, tq, LANES), BF16),
                        pltpu.VMEM((ATT_GROUPS, nb, LANES), F32),
                        pltpu.VMEM((ATT_TPS, ATT_NH, nb, tq), F32),
                        pltpu.VMEM((ATT_TPS, ATT_NH, tq, LANES), BF16)] + _attend_scratch(nb, MOBA_TRIP),
        compiler_params=_cparams(("parallel", "parallel", "arbitrary")),
        name="moba",
    )(q3, k3, v3, cos, sin_lo, sin_hi)
    return out.reshape(bsz * s, MOBA_INNER)


def _fox_gate_kernel(f_ref, b_ref, c_ref, ct_ref):
    tb = ATT_TILE
    s = f_ref.shape[0]
    row = lax.broadcasted_iota(jnp.int32, (tb, tb), 0)
    col = lax.broadcasted_iota(jnp.int32, (tb, tb), 1)
    tril = (row >= col).astype(BF16)
    carry = jnp.zeros((1, LANES), F32)
    for jb in range(s // tb):
        rows = slice(jb * tb, (jb + 1) * tb)
        log_f = -_softplus(-(f_ref[rows, :] + b_ref[...]))
        cblk = _sel_dot_rhs(tril, log_f) + carry
        c_ref[rows, :] = cblk
        ct_ref[jb] = cblk.T
        carry = cblk[tb - 1:tb, :]


def _fox_gate(f, bias, bsz, s):
    return pl.pallas_call(
        _fox_gate_kernel,
        grid=(bsz,),
        in_specs=[pl.BlockSpec((None, s, LANES), lambda b: (b, 0, 0)),
                  pl.BlockSpec((1, LANES), lambda b: (0, 0))],
        out_specs=[pl.BlockSpec((None, s, LANES), lambda b: (b, 0, 0)),
                   pl.BlockSpec((None, s // ATT_TILE, LANES, ATT_TILE), lambda b: (b, 0, 0, 0))],
        out_shape=[jax.ShapeDtypeStruct((bsz, s, LANES), F32),
                   jax.ShapeDtypeStruct((bsz, s // ATT_TILE, LANES, ATT_TILE), F32)],
        compiler_params=_cparams(("parallel",)),
        name="fox_gate",
    )(f.reshape(bsz, s, LANES), _pad_lanes(bias))


def _fox_kernel(q_ref, k_ref, v_ref, c_ref, ct_ref, o_ref, k_scr, ck_scr,
                vt_scr, st_scr, m_scr, acc_scr, ot_scr):
    tq = ATT_TILE
    nb = k_scr.shape[1]
    p = pl.program_id(1)
    step = pl.program_id(2)
    scale = HEAD_DIM ** -0.5
    hpg = LANES // HEAD_DIM

    @pl.when(step == 0)
    def _():
        lane = lax.broadcasted_iota(jnp.int32, (tq, LANES), 1)
        for jb in range(nb):
            rows = slice(jb * tq, (jb + 1) * tq)
            for g in range(ATT_GROUPS):
                k_scr[g, jb] = k_ref[rows, g * LANES:(g + 1) * LANES].astype(BF16)
            _store_vt(vt_scr, jb, v_ref[rows, :])
            cblk = c_ref[rows, :] * LOG2E
            for h in range(ATT_NH):
                colv = jnp.sum(jnp.where(lane == p * ATT_NH + h, cblk, 0.0), axis=1, keepdims=True)
                ck_scr[h, jb] = jnp.broadcast_to(colv, (tq, LANES))

    def tile(t, carry):
        i = step * ATT_TPS + t
        trows = pl.ds(pl.multiple_of(t * tq, tq), tq)
        qs = [qm.astype(BF16) for qm in _masked_heads(q_ref[trows, :], scale * LOG2E)]
        cqs = [ct_ref[t, pl.ds(p * ATT_NH + h, 1), :] * LOG2E for h in range(ATT_NH)]

        def score_fn(h, j):
            ck = ck_scr[h, j]
            return _dot_nt(k_scr[h // hpg, j], qs[h]) - jnp.concatenate([ck] * (tq // LANES), axis=1)

        _attend(i, ATT_NH, score_fn, cqs, vt_scr, st_scr, m_scr, acc_scr, ot_scr.at[t])
        return carry

    lax.fori_loop(0, ATT_TPS, tile, 0)
    _store_tiles(o_ref, ot_scr)


def _fox(q, k, v, c, ct, bsz, s):
    tq = ATT_TILE
    nb = s // tq
    q3, k3, v3 = (a.reshape(bsz, s, FOX_INNER) for a in (q, k, v))
    whole = pl.BlockSpec((None, s, ATT_LANES), lambda b, p, i: (b, 0, p))
    tile = pl.BlockSpec((None, ATT_TPS * tq, ATT_LANES), lambda b, p, i: (b, i, p))
    out = pl.pallas_call(
        _fox_kernel,
        grid=(bsz, FOX_INNER // ATT_LANES, nb // ATT_TPS),
        in_specs=[tile, whole, whole,
                  pl.BlockSpec((None, s, LANES), lambda b, p, i: (b, 0, 0)),
                  pl.BlockSpec((None, ATT_TPS, LANES, tq), lambda b, p, i: (b, i, 0, 0))],
        out_specs=tile,
        out_shape=jax.ShapeDtypeStruct((bsz, s, FOX_INNER), BF16),
        scratch_shapes=[pltpu.VMEM((ATT_GROUPS, nb, tq, LANES), BF16),
                        pltpu.VMEM((ATT_NH, nb, tq, LANES), F32)] + _attend_scratch(nb, FOX_TRIP),
        compiler_params=_cparams(("parallel", "parallel", "arbitrary")),
        name="fox",
    )(q3, k3, v3, c, ct)
    return out.reshape(bsz * s, FOX_INNER)


def _rope_tables(s):
    half = HEAD_DIM // 2
    inv = jnp.power(ROPE_THETA, -jnp.arange(half, dtype=F32) / half)
    ang = jnp.arange(s, dtype=F32)[:, None] * inv[None, :]
    cos, sin = jnp.cos(ang), jnp.sin(ang)
    reps = LANES // HEAD_DIM
    zero = jnp.zeros_like(sin)
    return (jnp.tile(jnp.concatenate([cos, cos], axis=1), (1, reps)),
            jnp.tile(jnp.concatenate([-sin, zero], axis=1), (1, reps)),
            jnp.tile(jnp.concatenate([zero, sin], axis=1), (1, reps)))


def _even_layer(x, bsz, s, g, w_in, conv_w, conv_b, dt_bias, a_log, d_skip, gate_norm, w_out, rope_tabs):
    o1 = SSD_INNER
    o2 = o1 + SSD_XBC
    o3 = o2 + SSD_HEADS
    ws = [w_in[:, :o2].astype(BF16), w_in[:, o3:].astype(BF16), _pad_cols(w_in[:, o2:o3]).astype(BF16)]
    z, xbc, q, k, v, dt = _norm_matmul(
        x, g, ws, ((SSD_INNER, SSD_XBC), (MOBA_INNER, MOBA_INNER, MOBA_INNER), (LANES,)))
    y_ssd = _ssd(z, xbc, dt, conv_w, conv_b, dt_bias, a_log, d_skip, gate_norm, bsz, s)
    y_att = _moba(q, k, v, *rope_tabs, bsz, s)
    w_o = w_out.astype(BF16)
    return [y_ssd, y_att], [w_o[:SSD_INNER], w_o[SSD_INNER:]]


def _odd_layer(x, bsz, s, g, w_in, fgate_bias, w_out):
    qkv = 3 * FOX_INNER
    ws = [w_in[:, :qkv].astype(BF16), _pad_cols(w_in[:, qkv:]).astype(BF16)]
    q, k, v, f = _norm_matmul(x, g, ws, ((FOX_INNER, FOX_INNER, FOX_INNER), (LANES,)))
    c, ct = _fox_gate(f, fgate_bias, bsz, s)
    y = _fox(q, k, v, c, ct, bsz, s)
    return [y], [w_out.astype(BF16)]


def kernel(x, norm_mix_even, w_in_even, conv_w, conv_b, dt_bias, a_log, d_skip, ssd_gate_norm, w_out_even,
           norm_mix_odd, w_in_odd, fgate_bias, w_out_odd, norm_mlp, w_up, w_down, final_norm):
    bsz, s, d = x.shape
    depth = norm_mlp.shape[0]
    assert s % (ATT_TILE * ATT_TPS) == 0 and s % SSD_STEP == 0
    h = x.reshape(bsz * s, d)
    rope_tabs = _rope_tables(s)
    for layer in range(depth):
        i = layer // 2
        if layer % 2 == 0:
            acts, ws = _even_layer(h, bsz, s, norm_mix_even[i], w_in_even[i], conv_w[i], conv_b[i], dt_bias[i],
                            a_log[i], d_skip[i], ssd_gate_norm[i], w_out_even[i], rope_tabs)
        else:
            acts, ws = _odd_layer(h, bsz, s, norm_mix_odd[i], w_in_odd[i], fgate_bias[i], w_out_odd[i])
        h = _proj_mlp(acts, ws, h, norm_mlp[layer], w_up[layer].astype(BF16), w_down[layer].astype(BF16),
                      final_norm, final_norm=(layer == depth - 1))
    return h.reshape(bsz, s, d)
```

```python
import functools
import math

import jax
import jax.numpy as jnp
from jax import lax
from jax.experimental import pallas as pl
from jax.experimental.pallas import tpu as pltpu

NORM_EPS = 1e-5
ROPE_THETA = 10000.0
LANES = 128
HEAD_DIM = 64
SSD_HEADS = 16
SSD_GROUPS = 4
SSD_STATE = 128
SSD_CONV = 4
SSD_CHUNK = 128
SSD_INNER = SSD_HEADS * HEAD_DIM
SSD_GN = SSD_GROUPS * SSD_STATE
SSD_XBC = SSD_INNER + 2 * SSD_GN
MOBA_HEADS = 8
MOBA_INNER = MOBA_HEADS * HEAD_DIM
MOBA_BLOCK = 256
MOBA_TOPK = 3
FOX_HEADS = 16
FOX_INNER = FOX_HEADS * HEAD_DIM
ATT_TILE = MOBA_BLOCK
VMEM_LIMIT = 56 * 1024 * 1024

F32 = jnp.float32
BF16 = jnp.bfloat16
NEG_INF = float("-inf")
LOG2E = math.log2(math.e)


def _cparams(sem):
    return pltpu.CompilerParams(dimension_semantics=sem, vmem_limit_bytes=VMEM_LIMIT)


def _resident(shape):
    return pl.BlockSpec(shape, lambda i: (0,) * len(shape), pipeline_mode=pl.Buffered(1))


def _rms(x, g):
    return x * lax.rsqrt(jnp.mean(x * x, axis=-1, keepdims=True) + NORM_EPS) * g


def _silu(x):
    h = 0.5 * x
    return h + h * jnp.tanh(h)


def _softplus(x):
    return jnp.maximum(x, 0.0) + jnp.log1p(jnp.exp(-jnp.abs(x)))


def _split3(x):
    hi = x.astype(BF16)
    r = x - hi.astype(F32)
    mid = r.astype(BF16)
    lo = (r - mid.astype(F32)).astype(BF16)
    return hi, mid, lo


def _dot(a, b):
    return jnp.dot(a, b, preferred_element_type=F32)


def _dot_nt(a, b):
    return lax.dot_general(a, b, (((1,), (1,)), ((), ())), preferred_element_type=F32)


def _sel_dot_rhs(m01, x):
    hi, mid, lo = _split3(x)
    return _dot(m01, hi) + _dot(m01, mid) + _dot(m01, lo)


def _sel_dot_lhs(x, m01):
    hi, mid, lo = _split3(x)
    return _dot(hi, m01) + _dot(mid, m01) + _dot(lo, m01)


def _norm_matmul_kernel(*refs, splits, nw, normed):
    x_ref, g_ref = refs[:2]
    w_refs, out_refs = refs[2:2 + nw], refs[2 + nw:]
    half = x_ref.shape[0] // 2
    for r in range(2):
        rows = slice(r * half, (r + 1) * half)
        xn = x_ref[rows, :] if normed else _rms(x_ref[rows, :], g_ref[...]).astype(BF16)
        for o_ref, (wi, a, b) in zip(out_refs, splits):
            o_ref[rows, :] = _dot(xn, w_refs[wi][:, a:b]).astype(o_ref.dtype)


def _norm_matmul(x, g, ws, widths, normed=False, tm=512):
    t, d = x.shape
    splits = []
    for wi, (w, wds) in enumerate(zip(ws, widths)):
        a = 0
        for wd in wds:
            splits.append((wi, a, a + wd))
            a += wd
        assert a == w.shape[1]
    assert t % tm == 0
    out_w = [b - a for _, a, b in splits]
    return pl.pallas_call(
        functools.partial(_norm_matmul_kernel, splits=tuple(splits), nw=len(ws), normed=normed),
        grid=(t // tm,),
        in_specs=[pl.BlockSpec((tm, d), lambda i: (i, 0)), _resident((1, d))] + [_resident(w.shape) for w in ws],
        out_specs=[pl.BlockSpec((tm, wd), lambda i: (i, 0)) for wd in out_w],
        out_shape=[jax.ShapeDtypeStruct((t, wd), F32) for wd in out_w],
        compiler_params=_cparams(("parallel",)),
        name="norm_matmul",
    )(x, g.reshape(1, d), *ws)


def _pad_cols(w):
    return jnp.pad(w, ((0, 0), (0, LANES - w.shape[1])))


MLP_CHUNK = 1024


def _proj_mlp_kernel(*refs, n, final_norm):
    a_refs, w_refs = refs[:n], refs[n:2 * n]
    x_ref, g_ref, wu_ref, wd_ref, gt_ref, o_ref = refs[2 * n:2 * n + 6]
    proj = _dot(a_refs[0][...], w_refs[0][...])
    for a_ref, w_ref in zip(a_refs[1:], w_refs[1:]):
        proj = proj + _dot(a_ref[...], w_ref[...])
    x = x_ref[...] + proj
    xn = _rms(x, g_ref[...]).astype(BF16)
    acc = x
    for c in range(wu_ref.shape[1] // MLP_CHUNK):
        cols = slice(c * MLP_CHUNK, (c + 1) * MLP_CHUNK)
        h = _dot(xn, wu_ref[:, cols])
        h = jnp.square(jnp.maximum(h, 0.0)).astype(BF16)
        acc = acc + _dot(h, wd_ref[cols, :])
    if final_norm:
        o_ref[...] = _rms(acc, gt_ref[...])
    else:
        o_ref[...] = acc
        refs[2 * n + 6][...] = _rms(acc, gt_ref[...]).astype(BF16)


def _proj_mlp(acts, ws, x, g, w_up, w_down, g_tail, final_norm, tm=512):
    t, d = x.shape
    n = len(acts)
    row = lambda width: pl.BlockSpec((tm, width), lambda i: (i, 0))
    const = lambda shape: pl.BlockSpec(shape, lambda i: (0, 0))
    in_specs = ([row(a.shape[1]) for a in acts] + [const(w.shape) for w in ws]
                + [row(d), const((1, d)), _resident(w_up.shape), _resident(w_down.shape), const((1, d))])
    return pl.pallas_call(
        functools.partial(_proj_mlp_kernel, n=n, final_norm=final_norm),
        grid=(t // tm,),
        in_specs=in_specs,
        out_specs=row(d) if final_norm else [row(d), row(d)],
        out_shape=(jax.ShapeDtypeStruct((t, d), F32) if final_norm else
                   [jax.ShapeDtypeStruct((t, d), F32), jax.ShapeDtypeStruct((t, d), BF16)]),
        compiler_params=_cparams(("parallel",)),
        name="proj_mlp",
    )(*acts, *ws, x, g.reshape(1, d), w_up, w_down, g_tail.reshape(1, d))


SSD_STEP = 8 * SSD_CHUNK
CONV_STRIDE = SSD_STEP // 8 + 1
CONV_ROWS = 8 * CONV_STRIDE
SSD_SLABS = SSD_XBC // LANES


def _ssd_kernel(z_ref, xbc_ref, dt_ref, cw_ref, cb_ref, dtb_ref, alog_ref, dskip_ref, gn_ref,
                e_ref, o_ref, ubuf, xa_scr, hst):
    L = SSD_CHUNK
    TL = SSD_STEP
    c = pl.program_id(1)

    @pl.when(c == 0)
    def _():
        hst[...] = jnp.zeros_like(hst)
        ubuf[...] = jnp.zeros_like(ubuf)

    for sl in range(SSD_SLABS):
        ubuf[sl, 8:8 + TL, :] = xbc_ref[:, sl * LANES:(sl + 1) * LANES]

    def conv_slab(sl, carry):
        w = cw_ref[sl]
        b = cb_ref[sl]
        u = [ubuf[sl, pl.ds(5 + m, 8, stride=CONV_STRIDE), :] for m in range(CONV_STRIDE + SSD_CONV - 1)]
        for r in range(CONV_STRIDE):
            acc = b
            for k in range(SSD_CONV):
                acc = acc + w[k:k + 1, :] * u[r + k]
            xa_scr[sl, pl.ds(r, 8, stride=CONV_STRIDE), :] = acc + acc * jnp.tanh(acc)
        ubuf[sl, 0:8, :] = ubuf[sl, TL:TL + 8, :]
        return carry

    lax.fori_loop(0, SSD_SLABS, conv_slab, 0)
    nxs = SSD_INNER // LANES
    ngs = SSD_GN // LANES
    row = lax.broadcasted_iota(jnp.int32, (L, L), 0)
    col = lax.broadcasted_iota(jnp.int32, (L, L), 1)
    causal = row >= col
    tril = causal.astype(BF16)
    e01 = e_ref[...]
    lane = lax.broadcasted_iota(jnp.int32, (L, LANES), 1)
    neg_a = -jnp.exp(alog_ref[...])
    gw = SSD_INNER // SSD_GROUPS
    hpl = LANES // HEAD_DIM

    for ci in range(TL // L):
        rows = slice(ci * L, (ci + 1) * L)
        xs = jnp.concatenate([xa_scr[sl, rows, :] for sl in range(nxs)], axis=1)
        dt = _softplus(dt_ref[rows, :] + dtb_ref[...])
        acum = _sel_dot_rhs(tril, dt * neg_a)
        alast = acum[L - 1:L, :]
        dtx = _dot(dt.astype(BF16), e01)
        eacx = _dot(jnp.exp(acum).astype(BF16), e01)
        dtex = _dot(jnp.exp(alast - acum).astype(BF16), e01)
        cdx = _sel_dot_lhs(jnp.exp(acum[L - 8:L, :]), e01)[7:8, :]
        xc = xs * dtx
        xcd = (xc * dtex).astype(BF16)
        acum2 = acum * LOG2E
        act2 = acum2.T

        cbs, y_offs = [], []
        for g in range(SSD_GROUPS):
            b_f32 = xa_scr[nxs + g, rows, :]
            bg = b_f32.astype(BF16)
            bgt = b_f32.T.astype(BF16)
            cg = xa_scr[nxs + ngs + g, rows, :].astype(BF16)
            cbs.append(_dot_nt(cg, bg))
            hprev = hst[g]
            y_offs.append(_dot(cg, hprev.astype(BF16)) * eacx[:, g * gw:(g + 1) * gw])
            hst[g] = hprev * cdx[:, g * gw:(g + 1) * gw] + _dot(bgt, xcd[:, g * gw:(g + 1) * gw])
        ms = []
        for h in range(SSD_HEADS):
            dec = jnp.exp2(jnp.where(causal, acum2[:, h:h + 1] - act2[h:h + 1, :], NEG_INF))
            ms.append((cbs[h * HEAD_DIM // gw] * dec).astype(BF16))
        y_diags = []
        for pr in range(SSD_INNER // LANES):
            xcp = xc[:, pr * LANES:(pr + 1) * LANES]
            xm = [jnp.where((lane >= hh * HEAD_DIM) & (lane < (hh + 1) * HEAD_DIM), xcp, 0.0).astype(BF16)
                  for hh in range(hpl)]
            y_diags.append(_dot(jnp.concatenate(ms[pr * hpl:(pr + 1) * hpl], axis=1), jnp.concatenate(xm, axis=0)))

        ppg = gw // LANES
        for g in range(SSD_GROUPS):
            lanes = slice(g * gw, (g + 1) * gw)
            y = jnp.concatenate(y_diags[g * ppg:(g + 1) * ppg], axis=1) + y_offs[g] + xs[:, lanes] * dskip_ref[:, lanes]
            yg = y * _silu(z_ref[rows, lanes])
            ms_g = jnp.mean(yg * yg, axis=-1, keepdims=True)
            o_ref[rows, lanes] = (yg * lax.rsqrt(ms_g + NORM_EPS) * gn_ref[:, lanes]).astype(o_ref.dtype)


def _pad_lanes(v):
    return jnp.pad(v.astype(F32), (0, LANES - v.shape[0])).reshape(1, LANES)


def _ssd(z, xbc, dt, conv_w, conv_b, dt_bias, a_log, d_skip, gate_norm, bsz, s):
    L = SSD_STEP
    nc = s // L
    e01 = (jnp.arange(LANES)[:, None] == (jnp.arange(SSD_INNER) // HEAD_DIM)[None, :]).astype(BF16)
    dsx = jnp.repeat(d_skip.astype(F32), HEAD_DIM).reshape(1, SSD_INNER)
    cw = (0.5 * conv_w.astype(F32)).reshape(SSD_CONV, SSD_SLABS, LANES).transpose(1, 0, 2)
    cw = jnp.pad(cw, ((0, 0), (0, 8 - SSD_CONV), (0, 0)))
    cb = (0.5 * conv_b.astype(F32)).reshape(SSD_SLABS, 1, LANES)
    tok = lambda b, c: (b * nc + c, 0)
    const = lambda b, c: (0, 0)
    const3 = lambda b, c: (0, 0, 0)
    return pl.pallas_call(
        _ssd_kernel,
        grid=(bsz, nc),
        in_specs=[pl.BlockSpec((L, SSD_INNER), tok),
                  pl.BlockSpec((L, SSD_XBC), tok),
                  pl.BlockSpec((L, LANES), tok),
                  pl.BlockSpec((SSD_SLABS, 8, LANES), const3),
                  pl.BlockSpec((SSD_SLABS, 1, LANES), const3),
                  pl.BlockSpec((1, LANES), const),
                  pl.BlockSpec((1, LANES), const),
                  pl.BlockSpec((1, SSD_INNER), const),
                  pl.BlockSpec((1, SSD_INNER), const),
                  pl.BlockSpec((LANES, SSD_INNER), const)],
        out_specs=pl.BlockSpec((L, SSD_INNER), tok),
        out_shape=jax.ShapeDtypeStruct((bsz * s, SSD_INNER), BF16),
        scratch_shapes=[pltpu.VMEM((SSD_SLABS, 8 + CONV_ROWS + 8, LANES), F32),
                        pltpu.VMEM((SSD_SLABS, CONV_ROWS, LANES), F32),
                        pltpu.VMEM((SSD_GROUPS, SSD_STATE, SSD_INNER // SSD_GROUPS), F32)],
        compiler_params=_cparams(("parallel", "arbitrary")),
        name="ssd",
    )(z, xbc, dt, cw, cb, _pad_lanes(dt_bias), _pad_lanes(a_log), dsx,
      gate_norm.astype(F32).reshape(1, SSD_INNER), e01)


def _rope(x, cos, sin_lo, sin_hi):
    half = HEAD_DIM // 2
    return x * cos + pltpu.roll(x, LANES - half, 1) * sin_lo + pltpu.roll(x, half, 1) * sin_hi


def _fold8(x, op):
    parts = [x[r * 8:(r + 1) * 8, :] for r in range(x.shape[0] // 8)]
    return _tree(parts, op)


def _tree(parts, op):
    while len(parts) > 1:
        parts = [op(parts[a], parts[a + 1]) if a + 1 < len(parts) else parts[a]
                 for a in range(0, len(parts), 2)]
    return parts[0]


def _attend(i, nh, score_fn, row_shift, vt_scr, st_scr, m_scr, acc_scr, ot_ref, block_bias=None):
    tq = ATT_TILE
    krow = lax.broadcasted_iota(jnp.int32, (tq, tq), 0)
    qcol = lax.broadcasted_iota(jnp.int32, (tq, tq), 1)
    m_scr[...] = jnp.full(m_scr.shape, NEG_INF, F32)
    acc_scr[...] = jnp.zeros(acc_scr.shape, F32)

    def trip(js, diagonal_last):
        shifts, alphas = [], []
        for h in range(nh):
            folds = []
            for u, j in enumerate(js):
                s_t = score_fn(h, j)
                if diagonal_last and u == len(js) - 1:
                    s_t = jnp.where(krow <= qcol, s_t, NEG_INF)
                st_scr[h, u] = s_t
                fold = _fold8(s_t, jnp.maximum)
                folds.append(fold if block_bias is None else fold + block_bias(h, j))
            m_trip = jnp.max(_tree(folds, jnp.maximum), axis=0, keepdims=True)
            rs = row_shift[h]
            m_old = m_scr[h]
            m_new = jnp.maximum(m_old, m_trip if rs is None else m_trip + rs)
            m_scr[h] = m_new
            shifts.append(m_new if rs is None else m_new - rs)
            alphas.append(jnp.exp2(m_old - m_new))
        for h in range(nh):
            pv = []
            for u, j in enumerate(js):
                shift = shifts[h] if block_bias is None else shifts[h] - block_bias(h, j)
                pv.append(_dot(vt_scr[j, h], jnp.exp2(st_scr[h, u] - shift).astype(BF16)))
            acc_scr[h] = alphas[h] * acc_scr[h] + _tree(pv, jnp.add)

    group = st_scr.shape[1]
    rem = i % group
    for r in range(group):
        @pl.when(rem == r)
        def _(r=r):
            trip([i - r + u for u in range(r + 1)], True)

    def full_trip(g, carry):
        trip([group * g + u for u in range(group)], False)
        return carry

    lax.fori_loop(0, i // group, full_trip, 0)
    for h in range(nh):
        ot_ref[h * HEAD_DIM:(h + 1) * HEAD_DIM, :] = acc_scr[h, :HEAD_DIM, :] / acc_scr[h, HEAD_DIM:HEAD_DIM + 1, :]


def _store_tiles(o_ref, ot_scr):
    tq = ATT_TILE
    for t in range(ot_scr.shape[0]):
        for g in range(ot_scr.shape[1] // LANES):
            o_ref[t * tq:(t + 1) * tq, g * LANES:(g + 1) * LANES] = (
                ot_scr[t, g * LANES:(g + 1) * LANES, :].T.astype(o_ref.dtype))


V_ROWS = HEAD_DIM + 16
ATT_GROUPS = 2
ATT_LANES = ATT_GROUPS * LANES
ATT_NH = ATT_LANES // HEAD_DIM
ATT_TPS = 4
MOBA_TRIP = 8
FOX_TRIP = 8


def _store_vt(vt_scr, jb, v_blk):
    n = v_blk.shape[1]
    eye = (lax.broadcasted_iota(jnp.int32, (n, n), 0) == lax.broadcasted_iota(jnp.int32, (n, n), 1)).astype(BF16)
    vt = _dot_nt(eye, v_blk.astype(BF16))
    for h in range(v_blk.shape[1] // HEAD_DIM):
        vt_scr[jb, h, :HEAD_DIM, :] = vt[h * HEAD_DIM:(h + 1) * HEAD_DIM, :].astype(BF16)
        vt_scr[jb, h, HEAD_DIM:, :] = jnp.ones((V_ROWS - HEAD_DIM, v_blk.shape[0]), BF16)


def _attend_scratch(nb, group):
    tq = ATT_TILE
    return [pltpu.VMEM((nb, ATT_NH, V_ROWS, tq), BF16),
            pltpu.VMEM((ATT_NH, group, tq, tq), F32),
            pltpu.VMEM((ATT_NH, 1, tq), F32),
            pltpu.VMEM((ATT_NH, V_ROWS, tq), F32),
            pltpu.VMEM((ATT_TPS, ATT_LANES, tq), F32)]


def _masked_heads(x, scale):
    lane = lax.broadcasted_iota(jnp.int32, (x.shape[0], LANES), 1)
    out = []
    for h in range(x.shape[1] // HEAD_DIM):
        g, hh = divmod(h, LANES // HEAD_DIM)
        xg = x[:, g * LANES:(g + 1) * LANES]
        in_head = (lane >= hh * HEAD_DIM) & (lane < (hh + 1) * HEAD_DIM)
        out.append(jnp.where(in_head, xg * scale, 0.0))
    return out


def _moba_kernel(q_ref, k_ref, v_ref, cos_ref, sinl_ref, sinh_ref, o_ref, kr_scr, km_scr, bias_scr, qs_scr,
                 vt_scr, st_scr, m_scr, acc_scr, ot_scr):
    tq = ATT_TILE
    nb = kr_scr.shape[1]
    step = pl.program_id(2)
    scale = HEAD_DIM ** -0.5
    hpg = LANES // HEAD_DIM

    @pl.when(step == 0)
    def _():
        for jb in range(nb):
            rows = slice(jb * tq, (jb + 1) * tq)
            for g in range(ATT_GROUPS):
                lanes = slice(g * LANES, (g + 1) * LANES)
                kr = _rope(k_ref[rows, lanes], cos_ref[rows, :], sinl_ref[rows, :], sinh_ref[rows, :])
                kr_scr[g, jb] = kr.astype(BF16)
                km_scr[g, jb:jb + 1, :] = jnp.mean(kr, axis=0, keepdims=True)
            _store_vt(vt_scr, jb, v_ref[rows, :])

    klane = lax.broadcasted_iota(jnp.int32, (nb, LANES), 1)
    jrow = lax.broadcasted_iota(jnp.int32, (nb, ATT_NH * tq), 0)
    kms = []
    for g in range(ATT_GROUPS):
        for hh in range(hpg):
            km = jnp.where((klane >= hh * HEAD_DIM) & (klane < (hh + 1) * HEAD_DIM), km_scr[g], 0.0)
            k_hi = km.astype(BF16)
            kms.append((k_hi, (km - k_hi.astype(F32)).astype(BF16)))
    for t in range(ATT_TPS):
        i = step * ATT_TPS + t
        trows = slice(t * tq, (t + 1) * tq)
        qrows = pl.ds(pl.multiple_of(i * tq, tq), tq)
        cos, sin_lo, sin_hi = cos_ref[qrows, :], sinl_ref[qrows, :], sinh_ref[qrows, :]
        qgs = [_rope(q_ref[trows, g * LANES:(g + 1) * LANES], cos, sin_lo, sin_hi) for g in range(ATT_GROUPS)]
        gates = []
        for g in range(ATT_GROUPS):
            q_hi = qgs[g].astype(BF16)
            q_lo = (qgs[g] - q_hi.astype(F32)).astype(BF16)
            for hh in range(hpg):
                k_hi, k_lo = kms[g * hpg + hh]
                gates.append(_dot_nt(k_hi, q_hi) + _dot_nt(k_hi, q_lo) + _dot_nt(k_lo, q_hi))
        valid = jrow < i
        gate = jnp.where(valid, jnp.concatenate(gates, axis=1), NEG_INF)
        cnt = jnp.zeros(gate.shape, F32)
        for jp in range(nb):
            gj = gate[jp:jp + 1, :]
            cnt = cnt + jnp.where((gj > gate) | ((gj == gate) & (jp < jrow)), 1.0, 0.0)
        chosen = valid & (cnt < MOBA_TOPK)
        bias = jnp.where(chosen | (jrow == i), 0.0, NEG_INF)
        for h, qm in enumerate(_masked_heads(jnp.concatenate(qgs, axis=1), 1.0)):
            bias_scr[t, h] = bias[:, h * tq:(h + 1) * tq]
            qs_scr[t, h] = (qm * (scale * LOG2E)).astype(BF16)

    def tile(t, carry):
        def score_fn(h, j):
            return _dot_nt(kr_scr[h // hpg, j], qs_scr[t, h])

        def block_bias(h, j):
            return bias_scr[t, h, pl.ds(j, 1), :]

        _attend(step * ATT_TPS + t, ATT_NH, score_fn, [None] * ATT_NH, vt_scr, st_scr, m_scr, acc_scr, ot_scr.at[t],
                block_bias)
        return carry

    lax.fori_loop(0, ATT_TPS, tile, 0)
    _store_tiles(o_ref, ot_scr)


def _moba(q, k, v, cos, sin_lo, sin_hi, bsz, s):
    tq = ATT_TILE
    nb = s // tq
    q3, k3, v3 = (a.reshape(bsz, s, MOBA_INNER) for a in (q, k, v))
    whole = pl.BlockSpec((None, s, ATT_LANES), lambda b, p, i: (b, 0, p))
    tile = pl.BlockSpec((None, ATT_TPS * tq, ATT_LANES), lambda b, p, i: (b, i, p))
    tab = pl.BlockSpec((s, LANES), lambda b, p, i: (0, 0))
    out = pl.pallas_call(
        _moba_kernel,
        grid=(bsz, MOBA_INNER // ATT_LANES, nb // ATT_TPS),
        in_specs=[tile, whole, whole, tab, tab, tab],
        out_specs=tile,
        out_shape=jax.ShapeDtypeStruct((bsz, s, MOBA_INNER), BF16),
        scratch_shapes=[pltpu.VMEM((ATT_GROUPS, nb, tq, LANES), BF16),
                        pltpu.VMEM((ATT_GROUPS, nb, LANES), F32),
                        pltpu.VMEM((ATT_TPS, ATT_NH, nb, tq), F32),
                        pltpu.VMEM((ATT_TPS, ATT_NH, tq, LANES), BF16)] + _attend_scratch(nb, MOBA_TRIP),
        compiler_params=_cparams(("parallel", "parallel", "arbitrary")),
        name="moba",
    )(q3, k3, v3, cos, sin_lo, sin_hi)
    return out.reshape(bsz * s, MOBA_INNER)


def _fox_gate_kernel(f_ref, b_ref, c_ref, ct_ref):
    tb = ATT_TILE
    s = f_ref.shape[0]
    row = lax.broadcasted_iota(jnp.int32, (tb, tb), 0)
    col = lax.broadcasted_iota(jnp.int32, (tb, tb), 1)
    tril = (row >= col).astype(BF16)
    carry = jnp.zeros((1, LANES), F32)
    for jb in range(s // tb):
        rows = slice(jb * tb, (jb + 1) * tb)
        log_f = -_softplus(-(f_ref[rows, :] + b_ref[...]))
        cblk = _sel_dot_rhs(tril, log_f) + carry
        c_ref[rows, :] = cblk
        ct_ref[jb] = cblk.T
        carry = cblk[tb - 1:tb, :]


def _fox_gate(f, bias, bsz, s):
    return pl.pallas_call(
        _fox_gate_kernel,
        grid=(bsz,),
        in_specs=[pl.BlockSpec((None, s, LANES), lambda b: (b, 0, 0)),
                  pl.BlockSpec((1, LANES), lambda b: (0, 0))],
        out_specs=[pl.BlockSpec((None, s, LANES), lambda b: (b, 0, 0)),
                   pl.BlockSpec((None, s // ATT_TILE, LANES, ATT_TILE), lambda b: (b, 0, 0, 0))],
        out_shape=[jax.ShapeDtypeStruct((bsz, s, LANES), F32),
                   jax.ShapeDtypeStruct((bsz, s // ATT_TILE, LANES, ATT_TILE), F32)],
        compiler_params=_cparams(("parallel",)),
        name="fox_gate",
    )(f.reshape(bsz, s, LANES), _pad_lanes(bias))


def _fox_kernel(q_ref, k_ref, v_ref, c_ref, ct_ref, o_ref, k_scr, ck_scr,
                vt_scr, st_scr, m_scr, acc_scr, ot_scr):
    tq = ATT_TILE
    nb = k_scr.shape[1]
    p = pl.program_id(1)
    step = pl.program_id(2)
    scale = HEAD_DIM ** -0.5
    hpg = LANES // HEAD_DIM

    @pl.when(step == 0)
    def _():
        lane = lax.broadcasted_iota(jnp.int32, (tq, LANES), 1)
        for jb in range(nb):
            rows = slice(jb * tq, (jb + 1) * tq)
            for g in range(ATT_GROUPS):
                k_scr[g, jb] = k_ref[rows, g * LANES:(g + 1) * LANES].astype(BF16)
            _store_vt(vt_scr, jb, v_ref[rows, :])
            cblk = c_ref[rows, :] * LOG2E
            for h in range(ATT_NH):
                colv = jnp.sum(jnp.where(lane == p * ATT_NH + h, cblk, 0.0), axis=1, keepdims=True)
                ck_scr[h, jb] = jnp.broadcast_to(colv, (tq, LANES))

    def tile(t, carry):
        i = step * ATT_TPS + t
        trows = pl.ds(pl.multiple_of(t * tq, tq), tq)
        qs = [qm.astype(BF16) for qm in _masked_heads(q_ref[trows, :], scale * LOG2E)]
        cqs = [ct_ref[t, pl.ds(p * ATT_NH + h, 1), :] * LOG2E for h in range(ATT_NH)]

        def score_fn(h, j):
            ck = ck_scr[h, j]
            return _dot_nt(k_scr[h // hpg, j], qs[h]) - jnp.concatenate([ck] * (tq // LANES), axis=1)

        _attend(i, ATT_NH, score_fn, cqs, vt_scr, st_scr, m_scr, acc_scr, ot_scr.at[t])
        return carry

    lax.fori_loop(0, ATT_TPS, tile, 0)
    _store_tiles(o_ref, ot_scr)


def _fox(q, k, v, c, ct, bsz, s):
    tq = ATT_TILE
    nb = s // tq
    q3, k3, v3 = (a.reshape(bsz, s, FOX_INNER) for a in (q, k, v))
    whole = pl.BlockSpec((None, s, ATT_LANES), lambda b, p, i: (b, 0, p))
    tile = pl.BlockSpec((None, ATT_TPS * tq, ATT_LANES), lambda b, p, i: (b, i, p))
    out = pl.pallas_call(
        _fox_kernel,
        grid=(bsz, FOX_INNER // ATT_LANES, nb // ATT_TPS),
        in_specs=[tile, whole, whole,
                  pl.BlockSpec((None, s, LANES), lambda b, p, i: (b, 0, 0)),
                  pl.BlockSpec((None, ATT_TPS, LANES, tq), lambda b, p, i: (b, i, 0, 0))],
        out_specs=tile,
        out_shape=jax.ShapeDtypeStruct((bsz, s, FOX_INNER), BF16),
        scratch_shapes=[pltpu.VMEM((ATT_GROUPS, nb, tq, LANES), BF16),
                        pltpu.VMEM((ATT_NH, nb, tq, LANES), F32)] + _attend_scratch(nb, FOX_TRIP),
        compiler_params=_cparams(("parallel", "parallel", "arbitrary")),
        name="fox",
    )(q3, k3, v3, c, ct)
    return out.reshape(bsz * s, FOX_INNER)


def _rope_tables(s):
    half = HEAD_DIM // 2
    inv = jnp.power(ROPE_THETA, -jnp.arange(half, dtype=F32) / half)
    ang = jnp.arange(s, dtype=F32)[:, None] * inv[None, :]
    cos, sin = jnp.cos(ang), jnp.sin(ang)
    reps = LANES // HEAD_DIM
    zero = jnp.zeros_like(sin)
    return (jnp.tile(jnp.concatenate([cos, cos], axis=1), (1, reps)),
            jnp.tile(jnp.concatenate([-sin, zero], axis=1), (1, reps)),
            jnp.tile(jnp.concatenate([zero, sin], axis=1), (1, reps)))


def _even_layer(x, normed, bsz, s, g, w_in, conv_w, conv_b, dt_bias, a_log, d_skip, gate_norm, w_out, rope_tabs):
    o1 = SSD_INNER
    o2 = o1 + SSD_XBC
    o3 = o2 + SSD_HEADS
    ws = [w_in[:, :o2].astype(BF16), w_in[:, o3:].astype(BF16), _pad_cols(w_in[:, o2:o3]).astype(BF16)]
    z, xbc, q, k, v, dt = _norm_matmul(
        x, g, ws, ((SSD_INNER, SSD_XBC), (MOBA_INNER, MOBA_INNER, MOBA_INNER), (LANES,)), normed)
    y_ssd = _ssd(z, xbc, dt, conv_w, conv_b, dt_bias, a_log, d_skip, gate_norm, bsz, s)
    y_att = _moba(q, k, v, *rope_tabs, bsz, s)
    w_o = w_out.astype(BF16)
    return [y_ssd, y_att], [w_o[:SSD_INNER], w_o[SSD_INNER:]]


def _odd_layer(x, normed, bsz, s, g, w_in, fgate_bias, w_out):
    qkv = 3 * FOX_INNER
    ws = [w_in[:, :qkv].astype(BF16), _pad_cols(w_in[:, qkv:]).astype(BF16)]
    q, k, v, f = _norm_matmul(x, g, ws, ((FOX_INNER, FOX_INNER, FOX_INNER), (LANES,)), normed)
    c, ct = _fox_gate(f, fgate_bias, bsz, s)
    y = _fox(q, k, v, c, ct, bsz, s)
    return [y], [w_out.astype(BF16)]


def kernel(x, norm_mix_even, w_in_even, conv_w, conv_b, dt_bias, a_log, d_skip, ssd_gate_norm, w_out_even,
           norm_mix_odd, w_in_odd, fgate_bias, w_out_odd, norm_mlp, w_up, w_down, final_norm):
    bsz, s, d = x.shape
    depth = norm_mlp.shape[0]
    assert s % (ATT_TILE * ATT_TPS) == 0 and s % SSD_STEP == 0
    h = x.reshape(bsz * s, d)
    rope_tabs = _rope_tables(s)
    hn = None
    for layer in range(depth):
        i = layer // 2
        xin, normed = (h, False) if hn is None else (hn, True)
        if layer % 2 == 0:
            acts, ws = _even_layer(xin, normed, bsz, s, norm_mix_even[i], w_in_even[i], conv_w[i], conv_b[i], dt_bias[i],
                            a_log[i], d_skip[i], ssd_gate_norm[i], w_out_even[i], rope_tabs)
        else:
            acts, ws = _odd_layer(xin, normed, bsz, s, norm_mix_odd[i], w_in_odd[i], fgate_bias[i], w_out_odd[i])
        last = layer == depth - 1
        nxt = layer + 1
        g_tail = final_norm if last else (norm_mix_even if nxt % 2 == 0 else norm_mix_odd)[nxt // 2]
        out = _proj_mlp(acts, ws, h, norm_mlp[layer], w_up[layer].astype(BF16), w_down[layer].astype(BF16),
                        g_tail, final_norm=last)
        h, hn = (out, None) if last else out
    return h.reshape(bsz, s, d)
```
